```python
import jax
import jax.numpy as jnp
from jax import lax
import numpy as np

D_MODEL = 2048
BATCH = 2
SEQ = 4096
DEPTH = 4
DEC_BATCH = 8
DEC_SEQ = 8
PAST_LEN = 16384
PAGE_SIZE = 128

HEAD_DIM = 128
A_HEADS = 6
A_KV_HEADS = 2
A_GROUP = A_HEADS // A_KV_HEADS
IDX_HEADS = 8
IDX_DIM = 64
DSA_TOPK = 256
B_HEADS = 6
B_KV_HEADS = 2
B_GROUP = B_HEADS // B_KV_HEADS
MOBA_BLOCK = 256
MOBA_TOPK = 3
POOL_WINDOWS = (2, 4, 8, 16)
POOL_GROUP_DIM = 128
POOL_DIM = len(POOL_WINDOWS) * POOL_GROUP_DIM
POOL_STATE = max(POOL_WINDOWS) - 1
A_DIM = A_HEADS * HEAD_DIM
A_KV_DIM = A_KV_HEADS * HEAD_DIM
B_DIM = B_HEADS * HEAD_DIM
B_KV_DIM = B_KV_HEADS * HEAD_DIM
MIX_DIM = A_DIM + B_DIM + POOL_DIM
IN_SPLITS = (A_DIM, A_KV_DIM, A_KV_DIM, IDX_HEADS * IDX_DIM, IDX_DIM, IDX_HEADS,
             B_DIM, B_KV_DIM, B_KV_DIM, POOL_DIM)
IN_DIM = sum(IN_SPLITS)
MEM_TOKENS = 256
MEM_HEADS = 4
MEM_HEAD_DIM = 128
MEM_DIM = MEM_HEADS * MEM_HEAD_DIM
FFN_HIDDEN = -(-(8 * D_MODEL) // (3 * 256)) * 256
ALPHA = (2.0 * DEPTH) ** 0.25
BETA = (8.0 * DEPTH) ** -0.25
LN_EPS = 1e-5
NEG_INF = -1e30
DSA_QBLOCK = 128
MOBA_QBLOCK = 32

kernel_name = "hymba_dsa_moba_pool_decoder_step"

F32 = jnp.float32


def layer_norm(x, g, b):
    xf = x.astype(F32)
    mu = jnp.mean(xf, axis=-1, keepdims=True)
    var = jnp.mean(jnp.square(xf - mu), axis=-1, keepdims=True)
    y = (xf - mu) * lax.rsqrt(var + LN_EPS)
    return (y * g.astype(F32) + b.astype(F32)).astype(x.dtype)


def to_chunks(a, qb):
    return a.reshape((a.shape[0], a.shape[1] // qb, qb) + a.shape[2:]).swapaxes(0, 1)


def from_chunks(o):
    o = o.swapaxes(0, 1)
    return o.reshape((o.shape[0], o.shape[1] * o.shape[2]) + o.shape[3:])


def project_in(x, w_in):
    b, t = x.shape[:2]
    h = jnp.einsum("btd,de->bte", x, w_in)
    offsets = np.cumsum(IN_SPLITS)[:-1].tolist()
    aq, ak, av, iq, ik, iw, bq, bk, bv, pu = jnp.split(h, offsets, axis=-1)
    return (aq.reshape(b, t, A_KV_HEADS, A_GROUP, HEAD_DIM),
            ak.reshape(b, t, A_KV_HEADS, HEAD_DIM),
            av.reshape(b, t, A_KV_HEADS, HEAD_DIM),
            iq.reshape(b, t, IDX_HEADS, IDX_DIM), ik, iw,
            bq.reshape(b, t, B_KV_HEADS, B_GROUP, HEAD_DIM),
            bk.reshape(b, t, B_KV_HEADS, HEAD_DIM),
            bv.reshape(b, t, B_KV_HEADS, HEAD_DIM), pu)


def dsa_select(iq, iw, ik_all, pos, k_sel):
    s = jnp.einsum("bqhd,bld->bqhl", iq.astype(F32), ik_all.astype(F32)) * (IDX_DIM ** -0.5)
    score = jnp.einsum("bqhl,bqh->bql", jax.nn.relu(s), iw.astype(F32)) * (IDX_HEADS ** -0.5)
    key_pos = jnp.arange(ik_all.shape[1])
    score = jnp.where(key_pos[None, None, :] <= pos[None, :, None], score, -jnp.inf)
    _, idx = lax.top_k(score, k_sel)
    valid = idx <= pos[None, :, None]
    return idx, valid


def dsa_attend(q, kg, vg, valid):
    b, t = q.shape[:2]
    logits = jnp.einsum("bqcgd,bqjcd->bqcgj", q.astype(F32), kg.astype(F32)) * (HEAD_DIM ** -0.5)
    logits = jnp.where(valid[:, :, None, None, :], logits, NEG_INF)
    p = jax.nn.softmax(logits, axis=-1)
    o = jnp.einsum("bqcgj,bqjcd->bqcgd", p, vg.astype(F32))
    return o.reshape(b, t, A_DIM).astype(q.dtype)


def dsa_prompt(aq, ak, av, iq, iw, ik):
    b, t = aq.shape[:2]
    k_sel = min(DSA_TOPK, t // 4)
    bidx = jnp.arange(b)[:, None, None]

    def block(args):
        q_b, iq_b, iw_b, pos_b = args
        idx, valid = dsa_select(iq_b, iw_b, ik, pos_b, k_sel)
        return dsa_attend(q_b, ak[bidx, idx], av[bidx, idx], valid)

    pos = jnp.arange(t).reshape(-1, DSA_QBLOCK)
    out = lax.map(block, (to_chunks(aq, DSA_QBLOCK), to_chunks(iq, DSA_QBLOCK),
                          to_chunks(iw, DSA_QBLOCK), pos))
    return from_chunks(out)


def dsa_sample(aq, ak, av, iq, iw, ik, pool_k, pool_v, pool_idx, page_table):
    db, t = aq.shape[:2]
    n_past = page_table.shape[1] * PAGE_SIZE
    ik_past = pool_idx[page_table].reshape(db, n_past, IDX_DIM)
    ik_all = jnp.concatenate([ik_past, ik], axis=1)
    k_sel = min(DSA_TOPK, (n_past + t) // 4)
    pos = n_past + jnp.arange(t)
    idx, valid = dsa_select(iq, iw, ik_all, pos, k_sel)
    bidx = jnp.arange(db)[:, None, None]
    in_past = (idx < n_past)[..., None, None]
    li = jnp.minimum(idx, n_past - 1)
    phys = page_table[bidx, li // PAGE_SIZE]
    slot = li % PAGE_SIZE
    ni = jnp.clip(idx - n_past, 0, t - 1)
    kg = jnp.where(in_past, pool_k[phys, slot], ak[bidx, ni])
    vg = jnp.where(in_past, pool_v[phys, slot], av[bidx, ni])
    return dsa_attend(aq, kg, vg, valid)


def to_blocks(k):
    b, L = k.shape[:2]
    nb = -(-L // MOBA_BLOCK)
    k = jnp.pad(k, ((0, 0), (0, nb * MOBA_BLOCK - L), (0, 0), (0, 0)))
    return k.reshape(b, nb, MOBA_BLOCK, k.shape[2], k.shape[3]).transpose(0, 3, 1, 2, 4)


def moba_queries(q, pos, kb, vb, kmean):
    b, t = q.shape[:2]
    n_kv, nb = kb.shape[1], kb.shape[2]
    k_sel = min(MOBA_TOPK, nb)
    qf = q.astype(F32)
    scale = HEAD_DIM ** -0.5
    own = pos // MOBA_BLOCK
    gate = jnp.einsum("bqcgd,bcnd->bqcgn", qf, kmean)
    past_blk = jnp.arange(nb)[None, :] < own[:, None]
    gate = jnp.where(past_blk[None, :, None, None, :], gate, -jnp.inf)
    _, sel = lax.top_k(gate, k_sel)
    sel_valid = sel < own[None, :, None, None, None]
    bi = jnp.arange(b)[:, None, None, None, None]
    ci = jnp.arange(n_kv)[None, None, :, None, None]
    k_rows = kb[bi, ci, sel].astype(F32)
    v_rows = vb[bi, ci, sel].astype(F32)
    l_sel = jnp.einsum("bqcgd,bqcgjsd->bqcgjs", qf, k_rows) * scale
    l_sel = jnp.where(sel_valid[..., None], l_sel, NEG_INF)
    bo = jnp.arange(b)[:, None, None]
    co = jnp.arange(n_kv)[None, None, :]
    oo = own[None, :, None]
    k_own = kb[bo, co, oo].astype(F32)
    v_own = vb[bo, co, oo].astype(F32)
    l_own = jnp.einsum("bqcgd,bqcsd->bqcgs", qf, k_own) * scale
    own_pos = own[:, None] * MOBA_BLOCK + jnp.arange(MOBA_BLOCK)[None, :]
    l_own = jnp.where((own_pos <= pos[:, None])[None, :, None, None, :], l_own, NEG_INF)
    n_sel = k_sel * MOBA_BLOCK
    logits = jnp.concatenate([l_sel.reshape(l_sel.shape[:4] + (n_sel,)), l_own], axis=-1)
    p = jax.nn.softmax(logits, axis=-1)
    p_sel = p[..., :n_sel].reshape(l_sel.shape)
    p_own = p[..., n_sel:]
    o = (jnp.einsum("bqcgjs,bqcgjsd->bqcgd", p_sel, v_rows)
         + jnp.einsum("bqcgs,bqcsd->bqcgd", p_own, v_own))
    return o.reshape(b, t, B_DIM).astype(q.dtype)


def moba_prompt(bq, bk, bv):
    kb, vb = to_blocks(bk), to_blocks(bv)
    kmean = jnp.mean(kb.astype(F32), axis=3)
    pos = jnp.arange(bq.shape[1]).reshape(-1, MOBA_QBLOCK)
    out = lax.map(lambda a: moba_queries(a[0], a[1], kb, vb, kmean),
                  (to_chunks(bq, MOBA_QBLOCK), pos))
    return from_chunks(out)


def moba_sample(bq, bk, bv, pool_k, pool_v, page_table):
    db, t = bq.shape[:2]
    n_past = page_table.shape[1] * PAGE_SIZE
    k_all = jnp.concatenate([pool_k[page_table].reshape(db, n_past, B_KV_HEADS, HEAD_DIM), bk], axis=1)
    v_all = jnp.concatenate([pool_v[page_table].reshape(db, n_past, B_KV_HEADS, HEAD_DIM), bv], axis=1)
    kb, vb = to_blocks(k_all), to_blocks(v_all)
    kmean = jnp.mean(kb.astype(F32), axis=3)
    pos = n_past + jnp.arange(t)
    return moba_queries(bq, pos, kb, vb, kmean)


def pool_mixer(u, prev, pos0, pool_w, pool_scale):
    b, t = u.shape[:2]
    up = jnp.concatenate([prev, u], axis=1)
    cs = jnp.cumsum(up.astype(F32), axis=1)
    cs = jnp.concatenate([jnp.zeros((b, 1, POOL_DIM), F32), cs], axis=1)
    uf = u.astype(F32)
    outs = []
    for g, w in enumerate(POOL_WINDOWS):
        c0, c1 = g * POOL_GROUP_DIM, (g + 1) * POOL_GROUP_DIM
        top = cs[:, POOL_STATE + 1: POOL_STATE + 1 + t, c0:c1]
        bot = cs[:, POOL_STATE + 1 - w: POOL_STATE + 1 - w + t, c0:c1]
        cnt = jnp.minimum(w, pos0 + jnp.arange(t) + 1).astype(F32)
        d = (top - bot) / cnt[None, :, None] - uf[..., c0:c1]
        outs.append(jnp.einsum("btc,ce->bte", d, pool_w[g].astype(F32)))
    y = jnp.concatenate(outs, axis=-1) * pool_scale.astype(F32)
    return y.astype(u.dtype), up[:, -POOL_STATE:]


def memory_kv(mem, w_k, w_v):
    b = mem.shape[0]
    mk = jnp.einsum("bmd,de->bme", mem, w_k).reshape(b, MEM_TOKENS, MEM_HEADS, MEM_HEAD_DIM)
    mv = jnp.einsum("bmd,de->bme", mem, w_v).reshape(b, MEM_TOKENS, MEM_HEADS, MEM_HEAD_DIM)
    return mk, mv


def memory_attention(x, mk, mv, w_q, w_o):
    b, t = x.shape[:2]
    q = jnp.einsum("btd,de->bte", x, w_q).reshape(b, t, MEM_HEADS, MEM_HEAD_DIM)
    logits = jnp.einsum("bthd,bmhd->bhtm", q.astype(F32), mk.astype(F32)) * (MEM_HEAD_DIM ** -0.5)
    p = jax.nn.softmax(logits, axis=-1)
    o = jnp.einsum("bhtm,bmhd->bthd", p, mv.astype(F32)).reshape(b, t, MEM_DIM).astype(x.dtype)
    return jnp.einsum("bte,ed->btd", o, w_o)


def post_mixer_sublayers(x, mix, mk, mv, w_out, ln1_g, ln1_b, w_mem_q, w_mem_o, ln2_g, ln2_b,
                         w1, w3, w2, ln3_g, ln3_b):
    x = layer_norm(ALPHA * x + jnp.einsum("bte,ed->btd", mix, w_out), ln1_g, ln1_b)
    x = layer_norm(ALPHA * x + memory_attention(x, mk, mv, w_mem_q, w_mem_o), ln2_g, ln2_b)
    h = jax.nn.silu(jnp.einsum("btd,df->btf", x, w1)) * jnp.einsum("btd,df->btf", x, w3)
    return layer_norm(ALPHA * x + jnp.einsum("btf,fd->btd", h, w2), ln3_g, ln3_b)


def setup_inputs(seed: int = 0) -> dict:
    key = jax.random.key(seed)
    ks = iter(jax.random.split(key, 40))
    n_pages = PAST_LEN // PAGE_SIZE
    n_pool = (5 * DEC_BATCH * n_pages + 3) // 4

    def nrm(shape, scale=1.0):
        return jax.random.normal(next(ks), shape, F32) * scale

    perm = jax.random.permutation(next(ks), n_pool)
    page_table = perm[: DEC_BATCH * n_pages].reshape(DEC_BATCH, n_pages).astype(jnp.int32)
    kv_shape = (DEPTH, n_pool, PAGE_SIZE, A_KV_HEADS, HEAD_DIM)
    return {
        "x_prompt": nrm((BATCH, SEQ, D_MODEL)),
        "x_sample": nrm((DEC_BATCH, DEC_SEQ, D_MODEL)),
        "cache_a_k": nrm(kv_shape),
        "cache_a_v": nrm(kv_shape),
        "cache_a_idx": nrm((DEPTH, n_pool, PAGE_SIZE, IDX_DIM)),
        "cache_b_k": nrm((DEPTH, n_pool, PAGE_SIZE, B_KV_HEADS, HEAD_DIM)),
        "cache_b_v": nrm((DEPTH, n_pool, PAGE_SIZE, B_KV_HEADS, HEAD_DIM)),
        "state_pool": nrm((DEPTH, DEC_BATCH, POOL_STATE, POOL_DIM)),
        "cache_mem_k": nrm((DEPTH, DEC_BATCH, MEM_TOKENS, MEM_HEADS, MEM_HEAD_DIM)),
        "cache_mem_v": nrm((DEPTH, DEC_BATCH, MEM_TOKENS, MEM_HEADS, MEM_HEAD_DIM)),
        "page_table": page_table,
        "mem_prompt": nrm((BATCH, MEM_TOKENS, D_MODEL)),
        "w_in": nrm((DEPTH, D_MODEL, IN_DIM), D_MODEL ** -0.5),
        "w_out": nrm((DEPTH, MIX_DIM, D_MODEL), BETA * MIX_DIM ** -0.5),
        "pool_w": nrm((DEPTH, len(POOL_WINDOWS), POOL_GROUP_DIM, POOL_GROUP_DIM), POOL_GROUP_DIM ** -0.5),
        "pool_scale": 1.0 + nrm((DEPTH, POOL_DIM), 0.05),
        "ln1_g": 1.0 + nrm((DEPTH, D_MODEL), 0.05),
        "ln1_b": nrm((DEPTH, D_MODEL), 0.02),
        "w_mem_q": nrm((DEPTH, D_MODEL, MEM_DIM), D_MODEL ** -0.5),
        "w_mem_k": nrm((DEPTH, D_MODEL, MEM_DIM), D_MODEL ** -0.5),
        "w_mem_v": nrm((DEPTH, D_MODEL, MEM_DIM), D_MODEL ** -0.5),
        "w_mem_o": nrm((DEPTH, MEM_DIM, D_MODEL), BETA * MEM_DIM ** -0.5),
        "ln2_g": 1.0 + nrm((DEPTH, D_MODEL), 0.05),
        "ln2_b": nrm((DEPTH, D_MODEL), 0.02),
        "w_ffn_1": nrm((DEPTH, D_MODEL, FFN_HIDDEN), D_MODEL ** -0.5),
        "w_ffn_3": nrm((DEPTH, D_MODEL, FFN_HIDDEN), D_MODEL ** -0.5),
        "w_ffn_2": nrm((DEPTH, FFN_HIDDEN, D_MODEL), BETA * FFN_HIDDEN ** -0.5),
        "ln3_g": 1.0 + nrm((DEPTH, D_MODEL), 0.05),
        "ln3_b": nrm((DEPTH, D_MODEL), 0.02),
    }


def reference(x_prompt, x_sample, cache_a_k, cache_a_v, cache_a_idx, cache_b_k, cache_b_v,
              state_pool, cache_mem_k, cache_mem_v, page_table, mem_prompt,
              w_in, w_out, pool_w, pool_scale, ln1_g, ln1_b, w_mem_q, w_mem_k, w_mem_v, w_mem_o,
              ln2_g, ln2_b, w_ffn_1, w_ffn_3, w_ffn_2, ln3_g, ln3_b):
    yp, ys = x_prompt, x_sample
    n_past = page_table.shape[1] * PAGE_SIZE
    pa_k, pa_v, pa_i, pb_k, pb_v, p_pool, pm_k, pm_v = [], [], [], [], [], [], [], []
    sa_k, sa_v, sa_i, sb_k, sb_v, s_pool = [], [], [], [], [], []
    for l in range(DEPTH):
        aq, ak, av, iq, ik, iw, bq, bk, bv, pu = project_in(yp, w_in[l])
        prev0 = jnp.zeros((yp.shape[0], POOL_STATE, POOL_DIM), yp.dtype)
        oc, pool_new = pool_mixer(pu, prev0, 0, pool_w[l], pool_scale[l])
        mix = jnp.concatenate([dsa_prompt(aq, ak, av, iq, iw, ik), moba_prompt(bq, bk, bv), oc], axis=-1)
        mk, mv = memory_kv(mem_prompt, w_mem_k[l], w_mem_v[l])
        yp = post_mixer_sublayers(yp, mix, mk, mv, w_out[l], ln1_g[l], ln1_b[l], w_mem_q[l], w_mem_o[l],
                                  ln2_g[l], ln2_b[l], w_ffn_1[l], w_ffn_3[l], w_ffn_2[l], ln3_g[l], ln3_b[l])
        pa_k.append(ak); pa_v.append(av); pa_i.append(ik)
        pb_k.append(bk); pb_v.append(bv); p_pool.append(pool_new)
        pm_k.append(mk); pm_v.append(mv)
        aq, ak, av, iq, ik, iw, bq, bk, bv, pu = project_in(ys, w_in[l])
        oc, pool_new = pool_mixer(pu, state_pool[l], n_past, pool_w[l], pool_scale[l])
        oa = dsa_sample(aq, ak, av, iq, iw, ik, cache_a_k[l], cache_a_v[l], cache_a_idx[l], page_table)
        ob = moba_sample(bq, bk, bv, cache_b_k[l], cache_b_v[l], page_table)
        mix = jnp.concatenate([oa, ob, oc], axis=-1)
        ys = post_mixer_sublayers(ys, mix, cache_mem_k[l], cache_mem_v[l], w_out[l], ln1_g[l], ln1_b[l],
                                  w_mem_q[l], w_mem_o[l], ln2_g[l], ln2_b[l],
                                  w_ffn_1[l], w_ffn_3[l], w_ffn_2[l], ln3_g[l], ln3_b[l])
        sa_k.append(ak); sa_v.append(av); sa_i.append(ik)
        sb_k.append(bk); sb_v.append(bv); s_pool.append(pool_new)
    return (yp, ys,
            jnp.stack(pa_k), jnp.stack(pa_v), jnp.stack(pa_i), jnp.stack(pb_k), jnp.stack(pb_v),
            jnp.stack(p_pool), jnp.stack(pm_k), jnp.stack(pm_v),
            jnp.stack(sa_k), jnp.stack(sa_v), jnp.stack(sa_i), jnp.stack(sb_k), jnp.stack(sb_v),
            jnp.stack(s_pool))
```

```python
import functools

import jax
import jax.numpy as jnp
import numpy as np
from jax import lax
from jax.experimental import pallas as pl
from jax.experimental.pallas import tpu as pltpu

F32 = jnp.float32
BF16 = jnp.bfloat16
I32 = jnp.int32

D_MODEL = 2048
DEPTH = 4
PAGE_SIZE = 128
HEAD_DIM = 128
A_HEADS = 6
A_KV_HEADS = 2
A_GROUP = A_HEADS // A_KV_HEADS
IDX_HEADS = 8
IDX_DIM = 64
DSA_TOPK = 256
B_HEADS = 6
B_KV_HEADS = 2
B_GROUP = B_HEADS // B_KV_HEADS
MOBA_BLOCK = 256
MOBA_TOPK = 3
POOL_WINDOWS = (2, 4, 8, 16)
POOL_GROUP_DIM = 128
POOL_DIM = len(POOL_WINDOWS) * POOL_GROUP_DIM
POOL_STATE = max(POOL_WINDOWS) - 1
A_DIM = A_HEADS * HEAD_DIM
A_KV_DIM = A_KV_HEADS * HEAD_DIM
B_DIM = B_HEADS * HEAD_DIM
B_KV_DIM = B_KV_HEADS * HEAD_DIM
MIX_DIM = A_DIM + B_DIM + POOL_DIM
MEM_TOKENS = 256
MEM_HEADS = 4
MEM_HEAD_DIM = 128
MEM_DIM = MEM_HEADS * MEM_HEAD_DIM
FFN_HIDDEN = -(-(8 * D_MODEL) // (3 * 256)) * 256
ALPHA = (2.0 * DEPTH) ** 0.25
LN_EPS = 1e-5
NEG_INF = -1e30
INT_MIN = -(2 ** 31)

LANES = 128
SUBLANES = 8
VMEM_LIMIT_BYTES = 56 * 1024 * 1024

_SRC = dict(aq=(0, 768), ak=(768, 256), av=(1024, 256), iq=(1280, 512), ik=(1792, 64), iw=(1856, 8),
            bq=(1864, 768), bk=(2632, 256), bv=(2888, 256), pu=(3144, 512))
H_AQ, H_BQ, H_IQ, H_PU, H_AK, H_AV, H_BK, H_BV, H_IKW = 0, 768, 1536, 2048, 2560, 2816, 3072, 3328, 3584
H_DIM = 3840

DSA_TQ = 128
DSA_CK = 256
MOBA_TQ = 128
POOL_TT = 512
MEM_TR = 512
PAGES_PER_STEP = 16


def _cparams(sem):
    return pltpu.CompilerParams(dimension_semantics=sem, vmem_limit_bytes=VMEM_LIMIT_BYTES)


def _layer_norm_rows(y, g, b):
    mu = jnp.mean(y, axis=-1, keepdims=True)
    yc = y - mu
    var = jnp.mean(yc * yc, axis=-1, keepdims=True)
    return yc * lax.rsqrt(var + LN_EPS) * g + b


def _mm_kernel(x_ref, w_ref, o_ref, xb_ref):
    @pl.when(pl.program_id(1) == 0)
    def _():
        xb_ref[...] = x_ref[...].astype(BF16)

    o_ref[...] = jnp.dot(xb_ref[...], w_ref[...], preferred_element_type=F32)


def matmul(x, w, layer, tm, tn):
    m, k = x.shape
    n = w.shape[2]
    return pl.pallas_call(
        _mm_kernel,
        grid=(m // tm, n // tn),
        in_specs=[pl.BlockSpec((tm, k), lambda i, j: (i, 0)),
                  pl.BlockSpec((None, k, tn), lambda i, j: (layer, 0, j))],
        out_specs=pl.BlockSpec((tm, tn), lambda i, j: (i, j)),
        out_shape=jax.ShapeDtypeStruct((m, n), F32),
        scratch_shapes=[pltpu.VMEM((tm, k), BF16)],
        compiler_params=_cparams(("parallel", "arbitrary")),
        name="matmul",
    )(x, w)


def _mm_res_ln_kernel(a_ref, w_ref, x_ref, g_ref, b_ref, o_ref, acc_ref, *, nk):
    kk = pl.program_id(1)

    @pl.when(kk == 0)
    def _():
        acc_ref[...] = jnp.zeros_like(acc_ref)

    acc_ref[...] += jnp.dot(a_ref[...].astype(BF16), w_ref[...], preferred_element_type=F32)

    @pl.when(kk == nk - 1)
    def _():
        y = ALPHA * x_ref[...] + acc_ref[...]
        o_ref[...] = _layer_norm_rows(y, g_ref[...], b_ref[...])


def matmul_residual_ln(a, w, x, g, b, layer, tm, tk):
    m, k = a.shape
    n = w.shape[2]
    nk = k // tk
    return pl.pallas_call(
        functools.partial(_mm_res_ln_kernel, nk=nk),
        grid=(m // tm, nk),
        in_specs=[pl.BlockSpec((tm, tk), lambda i, j: (i, j)),
                  pl.BlockSpec((None, tk, n), lambda i, j: (layer, j, 0)),
                  pl.BlockSpec((tm, n), lambda i, j: (i, 0)),
                  pl.BlockSpec((None, 1, n), lambda i, j: (layer, 0, 0)),
                  pl.BlockSpec((None, 1, n), lambda i, j: (layer, 0, 0))],
        out_specs=pl.BlockSpec((tm, n), lambda i, j: (i, 0)),
        out_shape=jax.ShapeDtypeStruct((m, n), F32),
        scratch_shapes=[pltpu.VMEM((tm, n), F32)],
        compiler_params=_cparams(("parallel", "arbitrary")),
        name="matmul_residual_ln",
    )(a, w, x, g, b)


def _ffn_kernel(x_ref, w1_ref, w3_ref, w2_ref, g_ref, b_ref, o_ref, xb_ref, acc_ref, *, nf):
    j = pl.program_id(1)

    @pl.when(j == 0)
    def _():
        xb_ref[...] = x_ref[...].astype(BF16)
        acc_ref[...] = jnp.zeros_like(acc_ref)

    xb = xb_ref[...]
    h1 = jnp.dot(xb, w1_ref[...], preferred_element_type=F32)
    h3 = jnp.dot(xb, w3_ref[...], preferred_element_type=F32)
    h = (h1 / (1.0 + jnp.exp(-h1))) * h3
    acc_ref[...] += jnp.dot(h.astype(BF16), w2_ref[...], preferred_element_type=F32)

    @pl.when(j == nf - 1)
    def _():
        y = ALPHA * x_ref[...] + acc_ref[...]
        o_ref[...] = _layer_norm_rows(y, g_ref[...], b_ref[...])


def ffn_ln(x, w1, w3, w2, g, b, layer, tm, tf):
    m, d = x.shape
    f = w1.shape[2]
    nf = f // tf
    return pl.pallas_call(
        functools.partial(_ffn_kernel, nf=nf),
        grid=(m // tm, nf),
        in_specs=[pl.BlockSpec((tm, d), lambda i, j: (i, 0)),
                  pl.BlockSpec((None, d, tf), lambda i, j: (layer, 0, j)),
                  pl.BlockSpec((None, d, tf), lambda i, j: (layer, 0, j)),
                  pl.BlockSpec((None, tf, d), lambda i, j: (layer, j, 0)),
                  pl.BlockSpec((None, 1, d), lambda i, j: (layer, 0, 0)),
                  pl.BlockSpec((None, 1, d), lambda i, j: (layer, 0, 0))],
        out_specs=pl.BlockSpec((tm, d), lambda i, j: (i, 0)),
        out_shape=jax.ShapeDtypeStruct((m, d), F32),
        scratch_shapes=[pltpu.VMEM((tm, d), BF16), pltpu.VMEM((tm, d), F32)],
        compiler_params=_cparams(("parallel", "arbitrary")),
        name="ffn_ln",
    )(x, w1, w3, w2, g, b)


def _sortable_key(score):
    score = jnp.where(score == 0.0, 0.0, score)
    bits = pltpu.bitcast(score, I32)
    return jnp.where(bits < 0, bits ^ 0x7FFFFFFF, bits)


def _kth_largest(count_ge, rows, k_sel):
    c0 = count_ge(jnp.zeros((rows, 1), I32))
    ok0 = c0 >= k_sel
    base = jnp.where(ok0, 0, INT_MIN).astype(I32)
    cnt = jnp.where(ok0, c0, jnp.iinfo(jnp.int32).max).astype(I32)

    def bit_body(i, carry):
        base, cnt = carry
        cand = base | jnp.left_shift(jnp.int32(1), 30 - i)
        c = count_ge(cand)
        ok = c >= k_sel
        return jnp.where(ok, cand, base), jnp.where(ok, c, cnt)

    return lax.fori_loop(0, 31, bit_body, (base, cnt))


def _dsa_prompt_kernel(aq_ref, iq_ref, ikw_ref, ak_ref, av_ref, o_ref,
                       ikT_s, kT_s, v_s, iwb_s, keys_s, bias_s, *, seq, k_sel):
    tq, ck = DSA_TQ, DSA_CK
    qi = pl.program_id(1)
    q0 = qi * tq
    prep_rows = min(512, seq)

    @pl.when(qi == 0)
    def _prep():
        def body(r, _):
            rows = pl.ds(pl.multiple_of(r * prep_rows, prep_rows), prep_rows)
            ikT_s[:, rows] = ikw_ref[rows, :].T[:IDX_DIM].astype(BF16)
            kk = ak_ref[rows, :]
            for c in range(A_KV_HEADS):
                kT_s[c, :, rows] = kk[:, c * HEAD_DIM:(c + 1) * HEAD_DIM].T.astype(BF16)
            v_s[rows, :] = av_ref[rows, :].astype(BF16)
            return 0

        lax.fori_loop(0, seq // prep_rows, body, 0)

    n_ck = (q0 + tq + ck - 1) // ck
    ikw_q = ikw_ref[pl.ds(pl.multiple_of(q0, tq), tq), :]
    iw = ikw_q[:, IDX_DIM:IDX_DIM + IDX_HEADS] * (IDX_HEADS ** -0.5)
    for h in range(IDX_HEADS):
        iwb_s[h] = jnp.broadcast_to(iw[:, h:h + 1], (tq, ck))
    iq = iq_ref[...] * (IDX_DIM ** -0.5)
    iq_st = jnp.concatenate([iq[:, h * IDX_DIM:(h + 1) * IDX_DIM] for h in range(IDX_HEADS)],
                            axis=0).astype(BF16)
    qpos = q0 + lax.broadcasted_iota(I32, (tq, ck), 0)
    lane = lax.broadcasted_iota(I32, (tq, ck), 1)

    def score_body(c, _):
        off = pl.multiple_of(c * ck, ck)
        s = jnp.dot(iq_st, ikT_s[:, pl.ds(off, ck)], preferred_element_type=F32)
        acc = jnp.zeros((tq, ck), F32)
        for h in range(IDX_HEADS):
            acc = acc + jnp.maximum(s[h * tq:(h + 1) * tq], 0.0) * iwb_s[h]
        key = jnp.where(off + lane <= qpos, _sortable_key(acc), INT_MIN)
        keys_s[:, pl.ds(off, ck)] = key
        return 0

    lax.fori_loop(0, n_ck, score_body, 0)

    def count_ge(cand):
        candb = jnp.broadcast_to(cand, (tq, LANES))

        def body(c, acc):
            kchunk = keys_s[:, pl.ds(pl.multiple_of(c * LANES, LANES), LANES)]
            return acc + jnp.where(kchunk >= candb, 1, 0).astype(I32)

        acc = lax.fori_loop(0, n_ck * (ck // LANES), body, jnp.zeros((tq, LANES), I32))
        return jnp.sum(acc, axis=1, keepdims=True)

    thr, cnt_thr = _kth_largest(count_ge, tq, k_sel)
    has_tie = jnp.logical_and(cnt_thr > k_sel, thr > INT_MIN)
    any_tie = jnp.max(has_tie.astype(I32)) > 0
    thr_c = jnp.maximum(thr, INT_MIN + 1)

    @pl.when(jnp.logical_not(any_tie))
    def _fast():
        def body(c, _):
            cols = pl.ds(pl.multiple_of(c * ck, ck), ck)
            bias_s[:, cols] = jnp.where(keys_s[:, cols] >= thr_c, 0.0, NEG_INF)
            return 0

        lax.fori_loop(0, n_ck, body, 0)

    @pl.when(any_tie)
    def _ties():
        need = (k_sel - count_ge(thr_c + 1)).astype(F32)
        tri = jnp.where(lax.broadcasted_iota(I32, (LANES, LANES), 0)
                        <= lax.broadcasted_iota(I32, (LANES, LANES), 1), 1.0, 0.0).astype(BF16)

        def body(c, run):
            cols = pl.ds(pl.multiple_of(c * LANES, LANES), LANES)
            kchunk = keys_s[:, cols]
            eq = kchunk == thr_c
            pre = jnp.dot(jnp.where(eq, 1.0, 0.0).astype(BF16), tri, preferred_element_type=F32) + run
            sel = jnp.logical_or(kchunk > thr_c, jnp.logical_and(eq, pre <= need))
            bias_s[:, cols] = jnp.where(sel, 0.0, NEG_INF)
            return run + jnp.sum(jnp.where(eq, 1.0, 0.0), axis=1, keepdims=True)

        lax.fori_loop(0, n_ck * (ck // LANES), body, jnp.zeros((tq, 1), F32))

    scale = HEAD_DIM ** -0.5
    aq = aq_ref[...]
    rows = A_GROUP * tq
    for c in range(A_KV_HEADS):
        qc = jnp.concatenate(
            [aq[:, (c * A_GROUP + g) * HEAD_DIM:(c * A_GROUP + g + 1) * HEAD_DIM] for g in range(A_GROUP)],
            axis=0).astype(BF16)

        def att_body(j, carry, c=c, qc=qc):
            m, l, acc = carry
            off = pl.multiple_of(j * ck, ck)
            lg = jnp.dot(qc, kT_s[c, :, pl.ds(off, ck)], preferred_element_type=F32) * scale
            bs = bias_s[:, pl.ds(off, ck)]
            lg = lg + jnp.concatenate([bs] * A_GROUP, axis=0)
            m_new = jnp.maximum(m, jnp.max(lg, axis=1, keepdims=True))
            alpha = jnp.exp(m - m_new)
            p = jnp.exp(lg - m_new)
            l = alpha * l + jnp.sum(p, axis=1, keepdims=True)
            pv = jnp.dot(p.astype(BF16), v_s[pl.ds(off, ck), c * HEAD_DIM:(c + 1) * HEAD_DIM],
                         preferred_element_type=F32)
            return m_new, l, alpha * acc + pv

        init = (jnp.full((rows, 1), NEG_INF, F32), jnp.zeros((rows, 1), F32), jnp.zeros((rows, HEAD_DIM), F32))
        _, l, acc = lax.fori_loop(0, n_ck, att_body, init)
        out = acc / l
        for g in range(A_GROUP):
            hcol = (c * A_GROUP + g) * HEAD_DIM
            o_ref[:, hcol:hcol + HEAD_DIM] = out[g * tq:(g + 1) * tq]


def dsa_prompt(h, batch, seq, n_rows):
    tq = DSA_TQ
    nq = seq // tq
    k_sel = min(DSA_TOPK, seq // 4)
    kern = functools.partial(_dsa_prompt_kernel, seq=seq, k_sel=k_sel)
    return pl.pallas_call(
        kern,
        grid=(batch, nq),
        in_specs=[pl.BlockSpec((tq, A_DIM), lambda b, i: (b * nq + i, H_AQ // A_DIM)),
                  pl.BlockSpec((tq, IDX_HEADS * IDX_DIM), lambda b, i: (b * nq + i, H_IQ // (IDX_HEADS * IDX_DIM))),
                  pl.BlockSpec((seq, LANES), lambda b, i: (b, H_IKW // LANES)),
                  pl.BlockSpec((seq, A_KV_DIM), lambda b, i: (b, H_AK // A_KV_DIM)),
                  pl.BlockSpec((seq, A_KV_DIM), lambda b, i: (b, H_AV // A_KV_DIM))],
        out_specs=pl.BlockSpec((tq, A_DIM), lambda b, i: (b * nq + i, 0)),
        out_shape=jax.ShapeDtypeStruct((n_rows, MIX_DIM), F32),
        scratch_shapes=[pltpu.VMEM((IDX_DIM, seq), BF16),
                        pltpu.VMEM((A_KV_HEADS, HEAD_DIM, seq), BF16),
                        pltpu.VMEM((seq, A_KV_DIM), BF16),
                        pltpu.VMEM((IDX_HEADS, tq, DSA_CK), F32),
                        pltpu.VMEM((tq, seq), I32),
                        pltpu.VMEM((tq, seq), F32)],
        compiler_params=_cparams(("arbitrary", "arbitrary")),
        name="dsa_prompt",
    )(h, h, h, h, h)


def _rank_select(gate, n_past_blocks, axis):
    idx = lax.broadcasted_iota(I32, gate.shape, axis)

    def body(m, rank):
        gm = jnp.sum(jnp.where(idx == m, gate, 0.0), axis=axis, keepdims=True)
        beats = jnp.logical_or(gm > gate, jnp.logical_and(gm == gate, m < idx))
        return rank + jnp.where(beats, 1, 0).astype(I32)

    rank = lax.fori_loop(0, n_past_blocks, body, jnp.zeros(gate.shape, I32))
    sel = jnp.logical_and(idx < n_past_blocks, rank < MOBA_TOPK)
    return jnp.where(sel, 0.0, NEG_INF)


def _moba_prompt_kernel(bq_ref, bk_ref, bv_ref, mix_ref, o_ref, kT_s, v_s, km_s, kmT_s, *, seq):
    del mix_ref
    tq, blk = MOBA_TQ, MOBA_BLOCK
    qi = pl.program_id(1)
    q0 = qi * tq
    nb = seq // blk

    @pl.when(qi == 0)
    def _prep():
        km_s[...] = jnp.zeros_like(km_s)

        def body(n, _):
            rows = pl.ds(pl.multiple_of(n * blk, blk), blk)
            kk = bk_ref[rows, :]
            km_s[pl.ds(n, 1), :] = jnp.mean(kk, axis=0, keepdims=True)
            for c in range(B_KV_HEADS):
                kT_s[c, :, rows] = kk[:, c * HEAD_DIM:(c + 1) * HEAD_DIM].T.astype(BF16)
            v_s[rows, :] = bv_ref[rows, :].astype(BF16)
            return 0

        lax.fori_loop(0, nb, body, 0)
        for c in range(B_KV_HEADS):
            kmT_s[c] = km_s[:, c * HEAD_DIM:(c + 1) * HEAD_DIM].T.astype(BF16)

    own = q0 // blk
    off_own = pl.multiple_of(own * blk, blk)
    scale = HEAD_DIM ** -0.5
    bq = bq_ref[...]
    rows = B_GROUP * tq
    qpos = q0 + lax.broadcasted_iota(I32, (tq, blk), 0)
    kpos_own = off_own + lax.broadcasted_iota(I32, (tq, blk), 1)
    causal_bias = jnp.where(kpos_own <= qpos, 0.0, NEG_INF)
    causal_bias = jnp.concatenate([causal_bias] * B_GROUP, axis=0)
    lane = lax.broadcasted_iota(I32, (rows, LANES), 1)

    for c in range(B_KV_HEADS):
        qc = jnp.concatenate(
            [bq[:, (c * B_GROUP + g) * HEAD_DIM:(c * B_GROUP + g + 1) * HEAD_DIM] for g in range(B_GROUP)],
            axis=0).astype(BF16)
        gate = jnp.dot(qc, kmT_s[c], preferred_element_type=F32)
        selb = _rank_select(gate, own, axis=1)

        lg = jnp.dot(qc, kT_s[c, :, pl.ds(off_own, blk)], preferred_element_type=F32) * scale + causal_bias
        m = jnp.max(lg, axis=1, keepdims=True)
        p = jnp.exp(lg - m)
        l = jnp.sum(p, axis=1, keepdims=True)
        acc = jnp.dot(p.astype(BF16), v_s[pl.ds(off_own, blk), c * HEAD_DIM:(c + 1) * HEAD_DIM],
                      preferred_element_type=F32)

        def blk_body(n, carry, c=c, qc=qc, selb=selb):
            m, l, acc = carry
            off = pl.multiple_of(n * blk, blk)
            selcol = jnp.sum(jnp.where(lane == n, selb, 0.0), axis=1, keepdims=True)
            lg = jnp.dot(qc, kT_s[c, :, pl.ds(off, blk)], preferred_element_type=F32) * scale + selcol
            m_new = jnp.maximum(m, jnp.max(lg, axis=1, keepdims=True))
            alpha = jnp.exp(m - m_new)
            p = jnp.exp(lg - m_new)
            l = alpha * l + jnp.sum(p, axis=1, keepdims=True)
            pv = jnp.dot(p.astype(BF16), v_s[pl.ds(off, blk), c * HEAD_DIM:(c + 1) * HEAD_DIM],
                         preferred_element_type=F32)
            return m_new, l, alpha * acc + pv

        _, l, acc = lax.fori_loop(0, own, blk_body, (m, l, acc))
        out = acc / l
        for g in range(B_GROUP):
            hcol = (c * B_GROUP + g) * HEAD_DIM
            o_ref[:, hcol:hcol + HEAD_DIM] = out[g * tq:(g + 1) * tq]


def moba_prompt(h, mix, batch, seq):
    tq = MOBA_TQ
    nq = seq // tq
    return pl.pallas_call(
        functools.partial(_moba_prompt_kernel, seq=seq),
        grid=(batch, nq),
        in_specs=[pl.BlockSpec((tq, B_DIM), lambda b, i: (b * nq + i, H_BQ // B_DIM)),
                  pl.BlockSpec((seq, B_KV_DIM), lambda b, i: (b, H_BK // B_KV_DIM)),
                  pl.BlockSpec((seq, B_KV_DIM), lambda b, i: (b, H_BV // B_KV_DIM)),
                  pl.BlockSpec(memory_space=pl.ANY)],
        out_specs=pl.BlockSpec((tq, B_DIM), lambda b, i: (b * nq + i, A_DIM // B_DIM)),
        out_shape=jax.ShapeDtypeStruct(mix.shape, mix.dtype),
        input_output_aliases={3: 0},
        scratch_shapes=[pltpu.VMEM((B_KV_HEADS, HEAD_DIM, seq), BF16),
                        pltpu.VMEM((seq, B_KV_DIM), BF16),
                        pltpu.VMEM((LANES, B_KV_DIM), F32),
                        pltpu.VMEM((B_KV_HEADS, HEAD_DIM, LANES), BF16)],
        compiler_params=_cparams(("arbitrary", "arbitrary")),
        name="moba_prompt",
    )(h, h, h, mix)


POOL_HALO = 16


def _pool_groups(u, window_sum, cnt_of, w_ref, sc_ref, o_ref):
    for g, w in enumerate(POOL_WINDOWS):
        cols = slice(g * POOL_GROUP_DIM, (g + 1) * POOL_GROUP_DIM)
        ug = u[:, cols]
        d = window_sum(g, w, ug) / cnt_of(w) - ug
        y = jnp.dot(d.astype(BF16), w_ref[g], preferred_element_type=F32)
        o_ref[:, cols] = y * sc_ref[:, cols]


def _pool_prompt_kernel(u_ref, halo_ref, w_ref, sc_ref, mix_ref, o_ref, ext_s):
    del mix_ref
    tt = POOL_TT
    i = pl.program_id(1)
    u = u_ref[...]
    ext_s[0:POOL_HALO, :] = jnp.where(i == 0, 0.0, halo_ref[...])
    ext_s[POOL_HALO:POOL_HALO + tt, :] = u
    t = i * tt + lax.broadcasted_iota(I32, (tt, 1), 0)

    def window_sum(g, w, ug):
        s = ug
        for k in range(1, w):
            s = s + ext_s[pl.ds(POOL_HALO - k, tt), g * POOL_GROUP_DIM:(g + 1) * POOL_GROUP_DIM]
        return s

    _pool_groups(u, window_sum, lambda w: jnp.minimum(w, t + 1).astype(F32), w_ref, sc_ref, o_ref)


def pool_prompt(h, mix, pool_w, pool_scale, layer, batch, seq):
    tt = POOL_TT
    nt = seq // tt
    halo_blocks = tt // POOL_HALO
    return pl.pallas_call(
        _pool_prompt_kernel,
        grid=(batch, nt),
        in_specs=[pl.BlockSpec((tt, POOL_DIM), lambda b, i: (b * nt + i, H_PU // POOL_DIM)),
                  pl.BlockSpec((POOL_HALO, POOL_DIM),
                               lambda b, i: (jnp.maximum((b * nt + i) * halo_blocks - 1, 0), H_PU // POOL_DIM)),
                  pl.BlockSpec((None, len(POOL_WINDOWS), POOL_GROUP_DIM, POOL_GROUP_DIM),
                               lambda b, i: (layer, 0, 0, 0)),
                  pl.BlockSpec((None, 1, POOL_DIM), lambda b, i: (layer, 0, 0)),
                  pl.BlockSpec(memory_space=pl.ANY)],
        out_specs=pl.BlockSpec((tt, POOL_DIM), lambda b, i: (b * nt + i, (A_DIM + B_DIM) // POOL_DIM)),
        out_shape=jax.ShapeDtypeStruct(mix.shape, mix.dtype),
        input_output_aliases={4: 0},
        scratch_shapes=[pltpu.VMEM((POOL_HALO + tt, POOL_DIM), F32)],
        compiler_params=_cparams(("arbitrary", "arbitrary")),
        name="pool_prompt",
    )(h, h, pool_w, pool_scale, mix)


def _pool_sample_kernel(u_ref, st_ref, w_ref, sc_ref, mix_ref, o_ref, ext_s, *, db, ts, n_past):
    del mix_ref
    u = u_ref[...]
    for b in range(db):
        ext_s[b, POOL_HALO - POOL_STATE:POOL_HALO, :] = st_ref[b]
        ext_s[b, POOL_HALO:POOL_HALO + ts, :] = u[b * ts:(b + 1) * ts]
    t = n_past + lax.broadcasted_iota(I32, (db * ts, 1), 0) % ts

    def window_sum(g, w, ug):
        parts = []
        for b in range(db):
            s = ug[b * ts:(b + 1) * ts]
            for k in range(1, w):
                s = s + ext_s[b, pl.ds(POOL_HALO - k, ts), g * POOL_GROUP_DIM:(g + 1) * POOL_GROUP_DIM]
            parts.append(s)
        return jnp.concatenate(parts, axis=0)

    _pool_groups(u, window_sum, lambda w: jnp.minimum(w, t + 1).astype(F32), w_ref, sc_ref, o_ref)


def pool_sample(h, mix, state_pool, pool_w, pool_scale, layer, n_prompt, db, ts, n_past):
    rows = db * ts
    return pl.pallas_call(
        functools.partial(_pool_sample_kernel, db=db, ts=ts, n_past=n_past),
        grid=(1,),
        in_specs=[pl.BlockSpec((rows, POOL_DIM), lambda i: (n_prompt // rows, H_PU // POOL_DIM)),
                  pl.BlockSpec((None, db, POOL_STATE, POOL_DIM), lambda i: (layer, 0, 0, 0)),
                  pl.BlockSpec((None, len(POOL_WINDOWS), POOL_GROUP_DIM, POOL_GROUP_DIM),
                               lambda i: (layer, 0, 0, 0)),
                  pl.BlockSpec((None, 1, POOL_DIM), lambda i: (layer, 0, 0)),
                  pl.BlockSpec(memory_space=pl.ANY)],
        out_specs=pl.BlockSpec((rows, POOL_DIM), lambda i: (n_prompt // rows, (A_DIM + B_DIM) // POOL_DIM)),
        out_shape=jax.ShapeDtypeStruct(mix.shape, mix.dtype),
        input_output_aliases={4: 0},
        scratch_shapes=[pltpu.VMEM((db, POOL_HALO + ts, POOL_DIM), F32)],
        compiler_params=_cparams(("arbitrary",)),
        name="pool_sample",
    )(h, state_pool, pool_w, pool_scale, mix)


def _memattn_kernel(q_ref, mk_ref, mv_ref, *rest):
    o_ref = rest[-1]
    scale = MEM_HEAD_DIM ** -0.5
    for hh in range(MEM_HEADS):
        cols = slice(hh * MEM_HEAD_DIM, (hh + 1) * MEM_HEAD_DIM)
        qh = q_ref[:, cols].astype(BF16)
        kh = mk_ref[:, cols].astype(BF16)
        vh = mv_ref[:, cols].astype(BF16)
        lg = lax.dot_general(qh, kh, (((1,), (1,)), ((), ())), preferred_element_type=F32) * scale
        m = jnp.max(lg, axis=1, keepdims=True)
        p = jnp.exp(lg - m)
        l = jnp.sum(p, axis=1, keepdims=True)
        o_ref[:, cols] = jnp.dot(p.astype(BF16), vh, preferred_element_type=F32) / l


def memattn_prompt(q, mkv, batch, seq):
    tr = min(MEM_TR, seq)
    nt = seq // tr
    return pl.pallas_call(
        _memattn_kernel,
        grid=(batch, nt),
        in_specs=[pl.BlockSpec((tr, MEM_DIM), lambda b, i: (b * nt + i, 0)),
                  pl.BlockSpec((MEM_TOKENS, MEM_DIM), lambda b, i: (b, 0)),
                  pl.BlockSpec((MEM_TOKENS, MEM_DIM), lambda b, i: (b, 1))],
        out_specs=pl.BlockSpec((tr, MEM_DIM), lambda b, i: (b * nt + i, 0)),
        out_shape=jax.ShapeDtypeStruct(q.shape, F32),
        compiler_params=_cparams(("arbitrary", "arbitrary")),
        name="memattn_prompt",
    )(q, mkv, mkv)


def memattn_sample(q, cache_k, cache_v, o, layer, n_prompt, db, ts):
    return pl.pallas_call(
        _memattn_kernel,
        grid=(db,),
        in_specs=[pl.BlockSpec((ts, MEM_DIM), lambda b: (n_prompt // ts + b, 0)),
                  pl.BlockSpec((None, MEM_TOKENS, MEM_DIM), lambda b: (layer * db + b, 0, 0)),
                  pl.BlockSpec((None, MEM_TOKENS, MEM_DIM), lambda b: (layer * db + b, 0, 0)),
                  pl.BlockSpec(memory_space=pl.ANY)],
        out_specs=pl.BlockSpec((ts, MEM_DIM), lambda b: (n_prompt // ts + b, 0)),
        out_shape=jax.ShapeDtypeStruct(o.shape, o.dtype),
        input_output_aliases={3: 0},
        compiler_params=_cparams(("arbitrary",)),
        name="memattn_sample",
    )(q, cache_k, cache_v, o)


def _page_specs(npg, page_cols, layer_base, step_of):
    def spec(i):
        return pl.BlockSpec((None, PAGE_SIZE, page_cols),
                            lambda b, j, pt: (layer_base + pt[b, step_of(j) * npg + i], 0, 0))
    return [spec(i) for i in range(npg)]


def _pad_rows(x, rows):
    return jnp.concatenate([x, jnp.zeros((rows - x.shape[0], x.shape[1]), x.dtype)], axis=0)


def _dsa_sample_select_kernel(pt_ref, iq_ref, ikw_ref, *refs, npg, n_steps, n_past, ts, k_sel):
    del pt_ref
    pages = refs[:npg]
    bias_ref, keys_s = refs[npg], refs[npg + 1]
    j = pl.program_id(1)
    ck = npg * PAGE_SIZE
    lp = n_past + LANES
    iq = iq_ref[...] * (IDX_DIM ** -0.5)
    iq_st = jnp.concatenate([iq[:, h * IDX_DIM:(h + 1) * IDX_DIM] for h in range(IDX_HEADS)],
                            axis=0).astype(BF16)
    ikw = ikw_ref[...]
    iw = ikw[:, IDX_DIM:IDX_DIM + IDX_HEADS] * (IDX_HEADS ** -0.5)

    def scores(ik_rows):
        s = lax.dot_general(iq_st, ik_rows.astype(BF16), (((1,), (1,)), ((), ())), preferred_element_type=F32)
        acc = jnp.zeros((ts, ik_rows.shape[0]), F32)
        for h in range(IDX_HEADS):
            acc = acc + jnp.maximum(s[h * ts:(h + 1) * ts], 0.0) * iw[:, h:h + 1]
        return acc

    ik_past = jnp.concatenate([pages[i][...] for i in range(npg)], axis=0)
    keys_s[:, pl.ds(pl.multiple_of(j * ck, ck), ck)] = _sortable_key(scores(ik_past))

    @pl.when(j == n_steps - 1)
    def _select():
        sc_new = scores(_pad_rows(ikw[:, 0:IDX_DIM], LANES))
        causal = lax.broadcasted_iota(I32, (ts, LANES), 1) <= lax.broadcasted_iota(I32, (ts, LANES), 0)
        keys_s[:, n_past:lp] = jnp.where(causal, _sortable_key(sc_new), INT_MIN)

        def count_ge(cand):
            return jnp.sum(jnp.where(keys_s[...] >= cand, 1, 0).astype(I32), axis=1, keepdims=True)

        thr, cnt_thr = _kth_largest(count_ge, ts, k_sel)
        has_tie = jnp.logical_and(cnt_thr > k_sel, thr > INT_MIN)
        any_tie = jnp.max(has_tie.astype(I32)) > 0
        thr_c = jnp.maximum(thr, INT_MIN + 1)

        @pl.when(jnp.logical_not(any_tie))
        def _fast():
            bias_ref[...] = jnp.where(keys_s[...] >= thr_c, 0.0, NEG_INF)

        @pl.when(any_tie)
        def _ties():
            need = (k_sel - count_ge(thr_c + 1)).astype(F32)
            tri = jnp.where(lax.broadcasted_iota(I32, (LANES, LANES), 0)
                            <= lax.broadcasted_iota(I32, (LANES, LANES), 1), 1.0, 0.0).astype(BF16)

            def body(c, run):
                cols = pl.ds(pl.multiple_of(c * LANES, LANES), LANES)
                kchunk = keys_s[:, cols]
                eq = kchunk == thr_c
                pre = jnp.dot(jnp.where(eq, 1.0, 0.0).astype(BF16), tri, preferred_element_type=F32) + run
                sel = jnp.logical_or(kchunk > thr_c, jnp.logical_and(eq, pre <= need))
                bias_ref[:, cols] = jnp.where(sel, 0.0, NEG_INF)
                return run + jnp.sum(jnp.where(eq, 1.0, 0.0), axis=1, keepdims=True)

            lax.fori_loop(0, lp // LANES, body, jnp.zeros((ts, 1), F32))


def dsa_sample_select(page_table, h, cache_idx, layer, n_pool, n_prompt, db, ts):
    n_pages = page_table.shape[1]
    n_past = n_pages * PAGE_SIZE
    npg = min(PAGES_PER_STEP, n_pages)
    n_steps = n_pages // npg
    k_sel = min(DSA_TOPK, (n_past + ts) // 4)
    lp = n_past + LANES
    kern = functools.partial(_dsa_sample_select_kernel, npg=npg, n_steps=n_steps, n_past=n_past, ts=ts, k_sel=k_sel)
    row_blk = n_prompt // ts
    grid_spec = pltpu.PrefetchScalarGridSpec(
        num_scalar_prefetch=1,
        grid=(db, n_steps),
        in_specs=[pl.BlockSpec((ts, IDX_HEADS * IDX_DIM), lambda b, j, pt: (row_blk + b, H_IQ // (IDX_HEADS * IDX_DIM))),
                  pl.BlockSpec((ts, LANES), lambda b, j, pt: (row_blk + b, H_IKW // LANES))]
        + _page_specs(npg, IDX_DIM, layer * n_pool, lambda j: j),
        out_specs=pl.BlockSpec((None, ts, lp), lambda b, j, pt: (b, 0, 0)),
        scratch_shapes=[pltpu.VMEM((ts, lp), I32)],
    )
    return pl.pallas_call(
        kern,
        grid_spec=grid_spec,
        out_shape=jax.ShapeDtypeStruct((db, ts, lp), F32),
        compiler_params=_cparams(("arbitrary", "arbitrary")),
        name="dsa_sample_select",
    )(page_table, h, h, *([cache_idx] * npg))


def _tile_bias_t(bs, group, ts):
    full = jnp.concatenate([bs] * group + [jnp.zeros((LANES - group * ts, bs.shape[1]), bs.dtype)], axis=0)
    return full.T


def _stack_heads_t(q, c, group, ts):
    qc = jnp.concatenate(
        [q[:, (c * group + g) * HEAD_DIM:(c * group + g + 1) * HEAD_DIM] for g in range(group)]
        + [jnp.zeros((LANES - group * ts, HEAD_DIM), q.dtype)], axis=0)
    return qc.T.astype(BF16)


def _dsa_sample_attend_kernel(pt_ref, q_ref, kn_ref, vn_ref, bias_ref, *refs, npg, n_steps, n_past, ts):
    del pt_ref
    kp, vp = refs[:npg], refs[npg:2 * npg]
    o_ref = refs[2 * npg + 1]
    qT_s, m_s, l_s, accT_s = refs[2 * npg + 2:]
    j = pl.program_id(1)
    ck = npg * PAGE_SIZE
    scale = HEAD_DIM ** -0.5

    @pl.when(j == 0)
    def _init():
        q = q_ref[...]
        for c in range(A_KV_HEADS):
            qT_s[c] = _stack_heads_t(q, c, A_GROUP, ts)
        m_s[...] = jnp.full(m_s.shape, NEG_INF, F32)
        l_s[...] = jnp.zeros_like(l_s)
        accT_s[...] = jnp.zeros_like(accT_s)

    def update(c, kc, vc, bias_t):
        lg = jnp.dot(kc.astype(BF16), qT_s[c], preferred_element_type=F32) * scale + bias_t
        m_old = m_s[c]
        m_new = jnp.maximum(m_old, jnp.max(lg, axis=0, keepdims=True))
        alpha = jnp.exp(m_old - m_new)
        p = jnp.exp(lg - m_new)
        l_s[c] = alpha * l_s[c] + jnp.sum(p, axis=0, keepdims=True)
        accT_s[c] = alpha * accT_s[c] + jnp.dot(vc.T.astype(BF16), p.astype(BF16), preferred_element_type=F32)
        m_s[c] = m_new

    bias_t = _tile_bias_t(bias_ref[:, pl.ds(pl.multiple_of(j * ck, ck), ck)], A_GROUP, ts)
    for c in range(A_KV_HEADS):
        cols = slice(c * HEAD_DIM, (c + 1) * HEAD_DIM)
        kc = jnp.concatenate([kp[i][:, cols] for i in range(npg)], axis=0)
        vc = jnp.concatenate([vp[i][:, cols] for i in range(npg)], axis=0)
        update(c, kc, vc, bias_t)

    @pl.when(j == n_steps - 1)
    def _finish():
        bias_n = _tile_bias_t(bias_ref[:, n_past:n_past + LANES], A_GROUP, ts)
        for c in range(A_KV_HEADS):
            cols = slice(c * HEAD_DIM, (c + 1) * HEAD_DIM)
            update(c, _pad_rows(kn_ref[:, cols], LANES), _pad_rows(vn_ref[:, cols], LANES), bias_n)
            out = (accT_s[c] / l_s[c]).T
            for g in range(A_GROUP):
                hcol = (c * A_GROUP + g) * HEAD_DIM
                o_ref[:, hcol:hcol + HEAD_DIM] = out[g * ts:(g + 1) * ts]


def dsa_sample_attend(page_table, h, bias, cache_k, cache_v, mix, layer, n_pool, n_prompt, db, ts):
    n_pages = page_table.shape[1]
    n_past = n_pages * PAGE_SIZE
    npg = min(PAGES_PER_STEP, n_pages)
    n_steps = n_pages // npg
    lp = n_past + LANES
    kern = functools.partial(_dsa_sample_attend_kernel, npg=npg, n_steps=n_steps, n_past=n_past, ts=ts)
    row_blk = n_prompt // ts
    grid_spec = pltpu.PrefetchScalarGridSpec(
        num_scalar_prefetch=1,
        grid=(db, n_steps),
        in_specs=[pl.BlockSpec((ts, A_DIM), lambda b, j, pt: (row_blk + b, H_AQ // A_DIM)),
                  pl.BlockSpec((ts, A_KV_DIM), lambda b, j, pt: (row_blk + b, H_AK // A_KV_DIM)),
                  pl.BlockSpec((ts, A_KV_DIM), lambda b, j, pt: (row_blk + b, H_AV // A_KV_DIM)),
                  pl.BlockSpec((None, ts, lp), lambda b, j, pt: (b, 0, 0))]
        + _page_specs(npg, A_KV_DIM, layer * n_pool, lambda j: j)
        + _page_specs(npg, A_KV_DIM, layer * n_pool, lambda j: j)
        + [pl.BlockSpec(memory_space=pl.ANY)],
        out_specs=pl.BlockSpec((ts, A_DIM), lambda b, j, pt: (row_blk + b, 0)),
        scratch_shapes=[pltpu.VMEM((A_KV_HEADS, HEAD_DIM, LANES), BF16),
                        pltpu.VMEM((A_KV_HEADS, 1, LANES), F32),
                        pltpu.VMEM((A_KV_HEADS, 1, LANES), F32),
                        pltpu.VMEM((A_KV_HEADS, HEAD_DIM, LANES), F32)],
    )
    return pl.pallas_call(
        kern,
        grid_spec=grid_spec,
        out_shape=jax.ShapeDtypeStruct(mix.shape, mix.dtype),
        input_output_aliases={5 + 2 * npg: 0},
        compiler_params=_cparams(("arbitrary", "arbitrary")),
        name="dsa_sample_attend",
    )(page_table, h, h, h, bias, *([cache_k] * npg), *([cache_v] * npg), mix)


def _moba_sample_kernel(pt_ref, q_ref, kn_ref, vn_ref, *refs, npg, n_kv_steps, ts):
    del pt_ref
    kp, vp = refs[:npg], refs[npg:2 * npg]
    o_ref = refs[2 * npg + 1]
    qT_s, p_s, km_s, sb_s, pn_s, l_s, accT_s = refs[2 * npg + 2:]
    j = pl.program_id(1)
    ck = npg * PAGE_SIZE
    blk = MOBA_BLOCK
    bps = ck // blk
    nbk = n_kv_steps * bps
    scale = HEAD_DIM ** -0.5

    @pl.when(j == 0)
    def _init():
        q = q_ref[...]
        for c in range(B_KV_HEADS):
            qT_s[c] = _stack_heads_t(q, c, B_GROUP, ts)
        km_s[...] = jnp.zeros_like(km_s)
        accT_s[...] = jnp.zeros_like(accT_s)

    @pl.when(j < n_kv_steps)
    def _k_phase():
        for c in range(B_KV_HEADS):
            cols = slice(c * HEAD_DIM, (c + 1) * HEAD_DIM)
            kc = jnp.concatenate([kp[i][:, cols] for i in range(npg)], axis=0)
            means = jnp.concatenate(
                [jnp.mean(kc[n * blk:(n + 1) * blk], axis=0, keepdims=True) for n in range(bps)], axis=0)
            km_s[c, pl.ds(pl.multiple_of(j * bps, bps), bps), :] = means
            p_s[c, pl.ds(pl.multiple_of(j * ck, ck), ck), :] = (
                jnp.dot(kc.astype(BF16), qT_s[c], preferred_element_type=F32) * scale)

    @pl.when(j == n_kv_steps - 1)
    def _select_softmax():
        key_i = lax.broadcasted_iota(I32, (LANES, LANES), 0)
        q_of_lane = lax.broadcasted_iota(I32, (LANES, LANES), 1) % ts
        new_ok = jnp.logical_and(key_i <= q_of_lane, key_i < ts)
        for c in range(B_KV_HEADS):
            cols = slice(c * HEAD_DIM, (c + 1) * HEAD_DIM)
            gate_t = jnp.dot(km_s[c].astype(BF16), qT_s[c], preferred_element_type=F32)
            sb_s[c] = _rank_select(gate_t, nbk, axis=0)
            lg_new = jnp.dot(_pad_rows(kn_ref[:, cols], LANES).astype(BF16), qT_s[c],
                             preferred_element_type=F32) * scale
            lg_new = jnp.where(new_ok, lg_new, NEG_INF)

            def blk_logits(n, c=c):
                rows = pl.ds(pl.multiple_of(n * blk, blk), blk)
                return rows, p_s[c, rows, :] + sb_s[c, pl.ds(n, 1), :]

            def max_body(n, m):
                _, lg = blk_logits(n)
                return jnp.maximum(m, jnp.max(lg, axis=0, keepdims=True))

            m = lax.fori_loop(0, nbk, max_body, jnp.max(lg_new, axis=0, keepdims=True))

            def exp_body(n, l, c=c, m=m):
                rows, lg = blk_logits(n)
                p = jnp.exp(lg - m)
                p_s[c, rows, :] = p
                return l + jnp.sum(p, axis=0, keepdims=True)

            p_new = jnp.exp(lg_new - m)
            pn_s[c] = p_new
            l_s[c] = lax.fori_loop(0, nbk, exp_body, jnp.sum(p_new, axis=0, keepdims=True))

    @pl.when(j >= n_kv_steps)
    def _v_phase():
        jj = j - n_kv_steps
        for c in range(B_KV_HEADS):
            cols = slice(c * HEAD_DIM, (c + 1) * HEAD_DIM)
            vc = jnp.concatenate([vp[i][:, cols] for i in range(npg)], axis=0)
            p = p_s[c, pl.ds(pl.multiple_of(jj * ck, ck), ck), :]
            accT_s[c] += jnp.dot(vc.T.astype(BF16), p.astype(BF16), preferred_element_type=F32)

    @pl.when(j == 2 * n_kv_steps - 1)
    def _finish():
        for c in range(B_KV_HEADS):
            cols = slice(c * HEAD_DIM, (c + 1) * HEAD_DIM)
            vn = _pad_rows(vn_ref[:, cols], LANES)
            acc = accT_s[c] + jnp.dot(vn.T.astype(BF16), pn_s[c].astype(BF16), preferred_element_type=F32)
            out = (acc / l_s[c]).T
            for g in range(B_GROUP):
                hcol = (c * B_GROUP + g) * HEAD_DIM
                o_ref[:, hcol:hcol + HEAD_DIM] = out[g * ts:(g + 1) * ts]


def moba_sample(page_table, h, cache_k, cache_v, mix, layer, n_pool, n_prompt, db, ts):
    n_pages = page_table.shape[1]
    n_past = n_pages * PAGE_SIZE
    npg = min(PAGES_PER_STEP, n_pages)
    n_kv = n_pages // npg
    kern = functools.partial(_moba_sample_kernel, npg=npg, n_kv_steps=n_kv, ts=ts)
    row_blk = n_prompt // ts
    grid_spec = pltpu.PrefetchScalarGridSpec(
        num_scalar_prefetch=1,
        grid=(db, 2 * n_kv),
        in_specs=[pl.BlockSpec((ts, B_DIM), lambda b, j, pt: (row_blk + b, H_BQ // B_DIM)),
                  pl.BlockSpec((ts, B_KV_DIM), lambda b, j, pt: (row_blk + b, H_BK // B_KV_DIM)),
                  pl.BlockSpec((ts, B_KV_DIM), lambda b, j, pt: (row_blk + b, H_BV // B_KV_DIM))]
        + _page_specs(npg, B_KV_DIM, layer * n_pool, lambda j: jnp.minimum(j, n_kv - 1))
        + _page_specs(npg, B_KV_DIM, layer * n_pool, lambda j: jnp.maximum(j - n_kv, 0))
        + [pl.BlockSpec(memory_space=pl.ANY)],
        out_specs=pl.BlockSpec((ts, B_DIM), lambda b, j, pt: (row_blk + b, A_DIM // B_DIM)),
        scratch_shapes=[pltpu.VMEM((B_KV_HEADS, HEAD_DIM, LANES), BF16),
                        pltpu.VMEM((B_KV_HEADS, n_past, LANES), F32),
                        pltpu.VMEM((B_KV_HEADS, LANES, HEAD_DIM), F32),
                        pltpu.VMEM((B_KV_HEADS, LANES, LANES), F32),
                        pltpu.VMEM((B_KV_HEADS, LANES, LANES), F32),
                        pltpu.VMEM((B_KV_HEADS, 1, LANES), F32),
                        pltpu.VMEM((B_KV_HEADS, HEAD_DIM, LANES), F32)],
    )
    return pl.pallas_call(
        kern,
        grid_spec=grid_spec,
        out_shape=jax.ShapeDtypeStruct(mix.shape, mix.dtype),
        input_output_aliases={4 + 2 * npg: 0},
        compiler_params=_cparams(("arbitrary", "arbitrary")),
        name="moba_sample",
    )(page_table, h, h, h, *([cache_k] * npg), *([cache_v] * npg), mix)


def _row_tile(n_rows, target):
    best = None
    for t in range(16, target + 1, 16):
        if n_rows % t == 0:
            best = t
    assert best is not None, n_rows
    return best


def _pack_w_in(w_in):
    order = ("aq", "bq", "iq", "pu", "ak", "av", "bk", "bv", "ik", "iw")
    parts = [w_in[..., _SRC[n][0]:_SRC[n][0] + _SRC[n][1]] for n in order]
    used = sum(_SRC[n][1] for n in order)
    parts.append(jnp.zeros(w_in.shape[:-1] + (H_DIM - used,), w_in.dtype))
    return jnp.concatenate(parts, axis=-1).astype(BF16)


def kernel(x_prompt, x_sample, cache_a_k, cache_a_v, cache_a_idx, cache_b_k, cache_b_v, state_pool,
           cache_mem_k, cache_mem_v, page_table, mem_prompt, w_in, w_out, pool_w, pool_scale, ln1_g, ln1_b,
           w_mem_q, w_mem_k, w_mem_v, w_mem_o, ln2_g, ln2_b, w_ffn_1, w_ffn_3, w_ffn_2, ln3_g, ln3_b):
    batch, seq, d = x_prompt.shape
    db, ts, _ = x_sample.shape
    depth = w_in.shape[0]
    n_pool = cache_a_k.shape[1]
    n_past = page_table.shape[1] * PAGE_SIZE
    n_prompt = batch * seq
    n_rows = n_prompt + db * ts
    tm = _row_tile(n_rows, 700)

    w_in_p = _pack_w_in(w_in)
    w_out_b = w_out.astype(BF16)
    w_mq_b = w_mem_q.astype(BF16)
    w_mkv_b = jnp.concatenate([w_mem_k, w_mem_v], axis=-1).astype(BF16)
    w_mo_b = w_mem_o.astype(BF16)
    w1_b, w3_b, w2_b = w_ffn_1.astype(BF16), w_ffn_3.astype(BF16), w_ffn_2.astype(BF16)
    pool_w_b = pool_w.astype(BF16)
    pool_scale3 = pool_scale.reshape(depth, 1, POOL_DIM)
    ln = [a.reshape(depth, 1, d) for a in (ln1_g, ln1_b, ln2_g, ln2_b, ln3_g, ln3_b)]

    ca_k = cache_a_k.reshape(depth * n_pool, PAGE_SIZE, A_KV_DIM)
    ca_v = cache_a_v.reshape(depth * n_pool, PAGE_SIZE, A_KV_DIM)
    ca_i = cache_a_idx.reshape(depth * n_pool, PAGE_SIZE, IDX_DIM)
    cb_k = cache_b_k.reshape(depth * n_pool, PAGE_SIZE, B_KV_DIM)
    cb_v = cache_b_v.reshape(depth * n_pool, PAGE_SIZE, B_KV_DIM)
    cm_k = cache_mem_k.reshape(depth * db, MEM_TOKENS, MEM_DIM)
    cm_v = cache_mem_v.reshape(depth * db, MEM_TOKENS, MEM_DIM)
    mem = mem_prompt.reshape(batch * MEM_TOKENS, d)

    x = jnp.concatenate([x_prompt.reshape(n_prompt, d), x_sample.reshape(db * ts, d)], axis=0)
    outs = {k: [] for k in ("pa_k", "pa_v", "pa_i", "pb_k", "pb_v", "p_pool", "pm_k", "pm_v",
                            "sa_k", "sa_v", "sa_i", "sb_k", "sb_v", "s_pool")}
    for l in range(depth):
        h = matmul(x, w_in_p, l, tm, 768)
        mkv = matmul(mem, w_mkv_b, l, _row_tile(mem.shape[0], 512), MEM_DIM)

        mix = dsa_prompt(h, batch, seq, n_rows)
        mix = moba_prompt(h, mix, batch, seq)
        mix = pool_prompt(h, mix, pool_w_b, pool_scale3, l, batch, seq)
        bias = dsa_sample_select(page_table, h, ca_i, l, n_pool, n_prompt, db, ts)
        mix = dsa_sample_attend(page_table, h, bias, ca_k, ca_v, mix, l, n_pool, n_prompt, db, ts)
        mix = moba_sample(page_table, h, cb_k, cb_v, mix, l, n_pool, n_prompt, db, ts)
        mix = pool_sample(h, mix, state_pool, pool_w_b, pool_scale3, l, n_prompt, db, ts, n_past)

        x1 = matmul_residual_ln(mix, w_out_b, x, ln[0], ln[1], l, tm, 512)
        q = matmul(x1, w_mq_b, l, tm, MEM_DIM)
        o = memattn_prompt(q, mkv, batch, seq)
        o = memattn_sample(q, cm_k, cm_v, o, l, n_prompt, db, ts)
        x2 = matmul_residual_ln(o, w_mo_b, x1, ln[2], ln[3], l, tm, MEM_DIM)
        x = ffn_ln(x2, w1_b, w3_b, w2_b, ln[4], ln[5], l, tm, 512)

        hp, hs = h[:n_prompt], h[n_prompt:]

        def cols(a, off, width, lead):
            return a[:, off:off + width].reshape(lead)

        kv_p, kv_s = (batch, seq, A_KV_HEADS, HEAD_DIM), (db, ts, A_KV_HEADS, HEAD_DIM)
        outs["pa_k"].append(cols(hp, H_AK, A_KV_DIM, kv_p))
        outs["pa_v"].append(cols(hp, H_AV, A_KV_DIM, kv_p))
        outs["pa_i"].append(cols(hp, H_IKW, IDX_DIM, (batch, seq, IDX_DIM)))
        outs["pb_k"].append(cols(hp, H_BK, B_KV_DIM, kv_p))
        outs["pb_v"].append(cols(hp, H_BV, B_KV_DIM, kv_p))
        pu_p = cols(hp, H_PU, POOL_DIM, (batch, seq, POOL_DIM))
        outs["p_pool"].append(pu_p[:, seq - POOL_STATE:])
        outs["pm_k"].append(mkv[:, :MEM_DIM].reshape(batch, MEM_TOKENS, MEM_HEADS, MEM_HEAD_DIM))
        outs["pm_v"].append(mkv[:, MEM_DIM:].reshape(batch, MEM_TOKENS, MEM_HEADS, MEM_HEAD_DIM))
        outs["sa_k"].append(cols(hs, H_AK, A_KV_DIM, kv_s))
        outs["sa_v"].append(cols(hs, H_AV, A_KV_DIM, kv_s))
        outs["sa_i"].append(cols(hs, H_IKW, IDX_DIM, (db, ts, IDX_DIM)))
        outs["sb_k"].append(cols(hs, H_BK, B_KV_DIM, kv_s))
        outs["sb_v"].append(cols(hs, H_BV, B_KV_DIM, kv_s))
        pu_s = cols(hs, H_PU, POOL_DIM, (db, ts, POOL_DIM))
        outs["s_pool"].append(jnp.concatenate([state_pool[l], pu_s], axis=1)[:, -POOL_STATE:])

    st = {k: jnp.stack(v) for k, v in outs.items()}
    return (x[:n_prompt].reshape(batch, seq, d), x[n_prompt:].reshape(db, ts, d),
            st["pa_k"], st["pa_v"], st["pa_i"], st["pb_k"], st["pb_v"], st["p_pool"], st["pm_k"], st["pm_v"],
            st["sa_k"], st["sa_v"], st["sa_i"], st["sb_k"], st["sb_v"], st["s_pool"])
```

```python
import functools
import math

import jax
import jax.numpy as jnp
from jax import lax
from jax.experimental import pallas as pl
from jax.experimental.pallas import tpu as pltpu

F32 = jnp.float32
BF16 = jnp.bfloat16
I32 = jnp.int32

D_MODEL = 2048
DEPTH = 4
PAGE_SIZE = 128
HEAD_DIM = 128
A_HEADS = 6
A_KV_HEADS = 2
A_GROUP = A_HEADS // A_KV_HEADS
IDX_HEADS = 8
IDX_DIM = 64
DSA_TOPK = 256
B_HEADS = 6
B_KV_HEADS = 2
B_GROUP = B_HEADS // B_KV_HEADS
MOBA_BLOCK = 256
MOBA_TOPK = 3
POOL_WINDOWS = (2, 4, 8, 16)
POOL_GROUP_DIM = 128
POOL_DIM = len(POOL_WINDOWS) * POOL_GROUP_DIM
POOL_STATE = max(POOL_WINDOWS) - 1
A_DIM = A_HEADS * HEAD_DIM
A_KV_DIM = A_KV_HEADS * HEAD_DIM
B_DIM = B_HEADS * HEAD_DIM
B_KV_DIM = B_KV_HEADS * HEAD_DIM
MIX_DIM = A_DIM + B_DIM + POOL_DIM
MEM_TOKENS = 256
MEM_HEADS = 4
MEM_HEAD_DIM = 128
MEM_DIM = MEM_HEADS * MEM_HEAD_DIM
FFN_HIDDEN = -(-(8 * D_MODEL) // (3 * 256)) * 256
ALPHA = (2.0 * DEPTH) ** 0.25
LN_EPS = 1e-5
NEG_INF = -1e30
INT_MIN = -(2 ** 31)
SOFTMAX_C = (HEAD_DIM ** -0.5) * math.log2(math.e)

LANES = 128
SUBLANES = 8
VMEM_LIMIT_BYTES = 56 * 1024 * 1024

_SRC = dict(aq=(0, 768), ak=(768, 256), av=(1024, 256), iq=(1280, 512), ik=(1792, 64), iw=(1856, 8),
            bq=(1864, 768), bk=(2632, 256), bv=(2888, 256), pu=(3144, 512))
H_AQ, H_BQ, H_IQ, H_PU, H_AK, H_AV, H_BK, H_BV, H_IKW = 0, 768, 1536, 2048, 2560, 2816, 3072, 3328, 3584
H_DIM = 3840

DSA_TQ = 128
DSA_CK = 256
DSA_CA = 512
MOBA_TQ = 128
POOL_TT = 512
MEM_TR = 512
PAGES_PER_STEP = 16


def _cparams(sem):
    return pltpu.CompilerParams(dimension_semantics=sem, vmem_limit_bytes=VMEM_LIMIT_BYTES)


def _layer_norm_rows(y, g, b):
    mu = jnp.mean(y, axis=-1, keepdims=True)
    yc = y - mu
    var = jnp.mean(yc * yc, axis=-1, keepdims=True)
    return yc * lax.rsqrt(var + LN_EPS) * g + b


def _mm_kernel(x_ref, w_ref, o_ref, xb_ref):
    @pl.when(pl.program_id(1) == 0)
    def _():
        xb_ref[...] = x_ref[...].astype(BF16)

    o_ref[...] = jnp.dot(xb_ref[...], w_ref[...], preferred_element_type=F32)


def matmul(x, w, layer, tm, tn):
    m, k = x.shape
    n = w.shape[2]
    return pl.pallas_call(
        _mm_kernel,
        grid=(m // tm, n // tn),
        in_specs=[pl.BlockSpec((tm, k), lambda i, j: (i, 0)),
                  pl.BlockSpec((None, k, tn), lambda i, j: (layer, 0, j))],
        out_specs=pl.BlockSpec((tm, tn), lambda i, j: (i, j)),
        out_shape=jax.ShapeDtypeStruct((m, n), F32),
        scratch_shapes=[pltpu.VMEM((tm, k), BF16)],
        compiler_params=_cparams(("parallel", "arbitrary")),
        name="matmul",
    )(x, w)


def _mm_res_ln_kernel(a_ref, w_ref, x_ref, g_ref, b_ref, o_ref, acc_ref, *, nk):
    kk = pl.program_id(1)

    @pl.when(kk == 0)
    def _():
        acc_ref[...] = jnp.zeros_like(acc_ref)

    acc_ref[...] += jnp.dot(a_ref[...].astype(BF16), w_ref[...], preferred_element_type=F32)

    @pl.when(kk == nk - 1)
    def _():
        y = ALPHA * x_ref[...] + acc_ref[...]
        o_ref[...] = _layer_norm_rows(y, g_ref[...], b_ref[...])


def matmul_residual_ln(a, w, x, g, b, layer, tm, tk):
    m, k = a.shape
    n = w.shape[2]
    nk = k // tk
    return pl.pallas_call(
        functools.partial(_mm_res_ln_kernel, nk=nk),
        grid=(m // tm, nk),
        in_specs=[pl.BlockSpec((tm, tk), lambda i, j: (i, j)),
                  pl.BlockSpec((None, tk, n), lambda i, j: (layer, j, 0)),
                  pl.BlockSpec((tm, n), lambda i, j: (i, 0)),
                  pl.BlockSpec((None, 1, n), lambda i, j: (layer, 0, 0)),
                  pl.BlockSpec((None, 1, n), lambda i, j: (layer, 0, 0))],
        out_specs=pl.BlockSpec((tm, n), lambda i, j: (i, 0)),
        out_shape=jax.ShapeDtypeStruct((m, n), F32),
        scratch_shapes=[pltpu.VMEM((tm, n), F32)],
        compiler_params=_cparams(("parallel", "arbitrary")),
        name="matmul_residual_ln",
    )(a, w, x, g, b)


def _ffn_kernel(x_ref, w1_ref, w3_ref, w2_ref, g_ref, b_ref, o_ref, xb_ref, acc_ref, *, nf):
    j = pl.program_id(1)

    @pl.when(j == 0)
    def _():
        xb_ref[...] = x_ref[...].astype(BF16)
        acc_ref[...] = jnp.zeros_like(acc_ref)

    xb = xb_ref[...]
    h1 = jnp.dot(xb, w1_ref[...], preferred_element_type=F32)
    h3 = jnp.dot(xb, w3_ref[...], preferred_element_type=F32)
    h = (h1 / (1.0 + jnp.exp(-h1))) * h3
    acc_ref[...] += jnp.dot(h.astype(BF16), w2_ref[...], preferred_element_type=F32)

    @pl.when(j == nf - 1)
    def _():
        y = ALPHA * x_ref[...] + acc_ref[...]
        o_ref[...] = _layer_norm_rows(y, g_ref[...], b_ref[...])


def ffn_ln(x, w1, w3, w2, g, b, layer, tm, tf):
    m, d = x.shape
    f = w1.shape[2]
    nf = f // tf
    return pl.pallas_call(
        functools.partial(_ffn_kernel, nf=nf),
        grid=(m // tm, nf),
        in_specs=[pl.BlockSpec((tm, d), lambda i, j: (i, 0)),
                  pl.BlockSpec((None, d, tf), lambda i, j: (layer, 0, j)),
                  pl.BlockSpec((None, d, tf), lambda i, j: (layer, 0, j)),
                  pl.BlockSpec((None, tf, d), lambda i, j: (layer, j, 0)),
                  pl.BlockSpec((None, 1, d), lambda i, j: (layer, 0, 0)),
                  pl.BlockSpec((None, 1, d), lambda i, j: (layer, 0, 0))],
        out_specs=pl.BlockSpec((tm, d), lambda i, j: (i, 0)),
        out_shape=jax.ShapeDtypeStruct((m, d), F32),
        scratch_shapes=[pltpu.VMEM((tm, d), BF16), pltpu.VMEM((tm, d), F32)],
        compiler_params=_cparams(("parallel", "arbitrary")),
        name="ffn_ln",
    )(x, w1, w3, w2, g, b)


def _sortable_key(score):
    score = jnp.where(score == 0.0, 0.0, score)
    bits = pltpu.bitcast(score, I32)
    return jnp.where(bits < 0, bits ^ 0x7FFFFFFF, bits)


def _kth_largest(count_ge, shape, k_sel):
    c0 = count_ge(jnp.zeros(shape, I32))
    ok0 = c0 >= k_sel
    base = jnp.where(ok0, 0, INT_MIN).astype(I32)
    cnt = jnp.where(ok0, c0, jnp.iinfo(jnp.int32).max).astype(I32)

    def bit_body(i, carry):
        base, cnt = carry
        cand = base | jnp.left_shift(jnp.int32(1), 30 - i)
        c = count_ge(cand)
        ok = c >= k_sel
        return jnp.where(ok, cand, base), jnp.where(ok, c, cnt)

    return lax.fori_loop(0, 31, bit_body, (base, cnt))


def _flash_step_t(kc, vt, qt, bias, m, l, acc):
    lg = jnp.dot(kc, qt, preferred_element_type=F32) + bias
    m_new = jnp.maximum(m, jnp.max(lg, axis=0, keepdims=True))
    alpha = jnp.exp2(m - m_new)
    p = jnp.exp2(lg - m_new)
    l_new = alpha * l + jnp.sum(p, axis=0, keepdims=True)
    acc_new = alpha * acc + jnp.dot(vt, p.astype(BF16), preferred_element_type=F32)
    return m_new, l_new, acc_new


def _prep_kv_t(k_ref, v_ref, k_s, vt_s, rows, n_heads):
    kk = k_ref[rows, :]
    vv = v_ref[rows, :]
    for c in range(n_heads):
        cols = slice(c * HEAD_DIM, (c + 1) * HEAD_DIM)
        k_s[c, rows, :] = kk[:, cols].astype(BF16)
        vt_s[c, :, rows] = vv[:, cols].T.astype(BF16)
    return kk


def _stage_queries_t(q, qt_s, n_kv, group):
    q = q * SOFTMAX_C
    for c in range(n_kv):
        qt_s[c] = jnp.concatenate(
            [q[:, (c * group + g) * HEAD_DIM:(c * group + g + 1) * HEAD_DIM].T for g in range(group)],
            axis=1).astype(BF16)


def _head_q(qt_s, hh, tq):
    c, g = divmod(hh, A_GROUP)
    return qt_s[c, :, g * tq:(g + 1) * tq]


def _write_heads(acc_s, ls, o_ref, n_heads):
    for hh in range(n_heads):
        o_ref[:, hh * HEAD_DIM:(hh + 1) * HEAD_DIM] = (acc_s[hh] / ls[hh]).T


def _dsa_prompt_kernel(aq_ref, iq_ref, ikw_ref, ak_ref, av_ref, o_ref,
                       ik_s, k_s, vt_s, iqt_s, iwt_s, qt_s, keys_s, bias_s, acc_s, *, seq, k_sel):
    tq, ck = DSA_TQ, DSA_CK
    qi = pl.program_id(1)
    q0 = qi * tq
    prep_rows = min(512, seq)

    @pl.when(qi == 0)
    def _prep():
        lane = lax.broadcasted_iota(I32, (prep_rows, LANES), 1)

        def body(r, _):
            rows = pl.ds(pl.multiple_of(r * prep_rows, prep_rows), prep_rows)
            ik_s[rows, :] = jnp.where(lane < IDX_DIM, ikw_ref[rows, :], 0.0).astype(BF16)
            _prep_kv_t(ak_ref, av_ref, k_s, vt_s, rows, A_KV_HEADS)
            return 0

        lax.fori_loop(0, seq // prep_rows, body, 0)

    n_ck = (q0 + tq + ck - 1) // ck
    ikw_q = ikw_ref[pl.ds(pl.multiple_of(q0, tq), tq), :]
    iwt_s[...] = ikw_q.T[IDX_DIM:IDX_DIM + IDX_HEADS] * (IDX_HEADS ** -0.5)
    iqt = (iq_ref[...] * (IDX_DIM ** -0.5)).T
    zpad = jnp.zeros((LANES - IDX_DIM, 2 * tq), F32)
    for hp in range(IDX_HEADS // 2):
        pair = jnp.concatenate([iqt[(2 * hp) * IDX_DIM:(2 * hp + 1) * IDX_DIM],
                                iqt[(2 * hp + 1) * IDX_DIM:(2 * hp + 2) * IDX_DIM]], axis=1)
        iqt_s[hp] = jnp.concatenate([pair, zpad], axis=0).astype(BF16)
    _stage_queries_t(aq_ref[...], qt_s, A_KV_HEADS, A_GROUP)

    qpos = q0 + lax.broadcasted_iota(I32, (ck, tq), 1)
    krow = lax.broadcasted_iota(I32, (ck, tq), 0)

    def score_body(c, _):
        off = pl.multiple_of(c * ck, ck)
        ikc = ik_s[pl.ds(off, ck), :]
        acc = jnp.zeros((ck, tq), F32)
        for hp in range(IDX_HEADS // 2):
            s = jnp.dot(ikc, iqt_s[hp], preferred_element_type=F32)
            acc = acc + (jnp.maximum(s[:, :tq], 0.0) * iwt_s[2 * hp:2 * hp + 1, :]
                         + jnp.maximum(s[:, tq:], 0.0) * iwt_s[2 * hp + 1:2 * hp + 2, :])
        keys_s[pl.ds(off, ck), :] = jnp.where(off + krow <= qpos, _sortable_key(acc), INT_MIN)
        return 0

    lax.fori_loop(0, n_ck, score_body, 0)

    part = ck // 4

    def count_ge(cand):
        def body(c, acc):
            ind = jnp.where(keys_s[pl.ds(pl.multiple_of(c * ck, ck), ck), :] >= cand, 1, 0).astype(I32)
            return acc + ((ind[0:part] + ind[part:2 * part]) + (ind[2 * part:3 * part] + ind[3 * part:]))

        acc = lax.fori_loop(0, n_ck, body, jnp.zeros((part, tq), I32))
        return jnp.sum(acc, axis=0, keepdims=True)

    thr, cnt_thr = _kth_largest(count_ge, (1, tq), k_sel)
    has_tie = jnp.logical_and(cnt_thr > k_sel, thr > INT_MIN)
    any_tie = jnp.max(has_tie.astype(I32)) > 0
    thr_c = jnp.maximum(thr, INT_MIN + 1)

    @pl.when(jnp.logical_not(any_tie))
    def _fast():
        def body(c, _):
            rows = pl.ds(pl.multiple_of(c * ck, ck), ck)
            bias_s[rows, :] = jnp.where(keys_s[rows, :] >= thr_c, 0.0, NEG_INF)
            return 0

        lax.fori_loop(0, n_ck, body, 0)

    @pl.when(any_tie)
    def _ties():
        need = (k_sel - count_ge(thr_c + 1)).astype(F32)
        tri = jnp.where(lax.broadcasted_iota(I32, (LANES, LANES), 0)
                        >= lax.broadcasted_iota(I32, (LANES, LANES), 1), 1.0, 0.0).astype(BF16)

        def body(c, run):
            rows = pl.ds(pl.multiple_of(c * LANES, LANES), LANES)
            kchunk = keys_s[rows, :]
            eq = kchunk == thr_c
            eqf = jnp.where(eq, 1.0, 0.0)
            pre = jnp.dot(tri, eqf.astype(BF16), preferred_element_type=F32) + run
            sel = jnp.logical_or(kchunk > thr_c, jnp.logical_and(eq, pre <= need))
            bias_s[rows, :] = jnp.where(sel, 0.0, NEG_INF)
            return run + jnp.sum(eqf, axis=0, keepdims=True)

        lax.fori_loop(0, n_ck * (ck // LANES), body, jnp.zeros((1, tq), F32))

    ca = DSA_CA
    n_ca = (q0 + tq + ca - 1) // ca
    for extra in range(1, ca // ck):
        @pl.when(n_ck + extra <= n_ca * (ca // ck))
        def _mask_tail(extra=extra):
            bias_s[pl.ds(pl.multiple_of((n_ck + extra - 1) * ck, ck), ck), :] = jnp.full((ck, tq), NEG_INF, F32)

    acc_s[...] = jnp.zeros_like(acc_s)

    def att_body(j, carry):
        ms, ls = carry
        rows = pl.ds(pl.multiple_of(j * ca, ca), ca)
        bias = bias_s[rows, :]
        new_m, new_l = [], []
        for hh in range(A_HEADS):
            c = hh // A_GROUP
            m, l, acc = _flash_step_t(k_s[c, rows, :], vt_s[c, :, rows], _head_q(qt_s, hh, tq), bias,
                                      ms[hh], ls[hh], acc_s[hh])
            acc_s[hh] = acc
            new_m.append(m)
            new_l.append(l)
        return tuple(new_m), tuple(new_l)

    init = (tuple(jnp.full((1, tq), NEG_INF, F32) for _ in range(A_HEADS)),
            tuple(jnp.zeros((1, tq), F32) for _ in range(A_HEADS)))
    _, ls = lax.fori_loop(0, n_ca, att_body, init)
    _write_heads(acc_s, ls, o_ref, A_HEADS)


def dsa_prompt(h, batch, seq, n_rows):
    tq = DSA_TQ
    nq = seq // tq
    k_sel = min(DSA_TOPK, seq // 4)
    kern = functools.partial(_dsa_prompt_kernel, seq=seq, k_sel=k_sel)
    return pl.pallas_call(
        kern,
        grid=(batch, nq),
        in_specs=[pl.BlockSpec((tq, A_DIM), lambda b, i: (b * nq + i, H_AQ // A_DIM)),
                  pl.BlockSpec((tq, IDX_HEADS * IDX_DIM), lambda b, i: (b * nq + i, H_IQ // (IDX_HEADS * IDX_DIM))),
                  pl.BlockSpec((seq, LANES), lambda b, i: (b, H_IKW // LANES)),
                  pl.BlockSpec((seq, A_KV_DIM), lambda b, i: (b, H_AK // A_KV_DIM)),
                  pl.BlockSpec((seq, A_KV_DIM), lambda b, i: (b, H_AV // A_KV_DIM))],
        out_specs=pl.BlockSpec((tq, A_DIM), lambda b, i: (b * nq + i, 0)),
        out_shape=jax.ShapeDtypeStruct((n_rows, MIX_DIM), F32),
        scratch_shapes=[pltpu.VMEM((seq, LANES), BF16),
                        pltpu.VMEM((A_KV_HEADS, seq, HEAD_DIM), BF16),
                        pltpu.VMEM((A_KV_HEADS, HEAD_DIM, seq), BF16),
                        pltpu.VMEM((IDX_HEADS // 2, LANES, 2 * tq), BF16),
                        pltpu.VMEM((IDX_HEADS, tq), F32),
                        pltpu.VMEM((A_KV_HEADS, HEAD_DIM, A_GROUP * tq), BF16),
                        pltpu.VMEM((seq, tq), I32),
                        pltpu.VMEM((seq, tq), F32),
                        pltpu.VMEM((A_HEADS, HEAD_DIM, tq), F32)],
        compiler_params=_cparams(("arbitrary", "arbitrary")),
        name="dsa_prompt",
    )(h, h, h, h, h)


def _rank_select(gate, n_past_blocks, axis):
    idx = lax.broadcasted_iota(I32, gate.shape, axis)

    def body(m, rank):
        gm = jnp.sum(jnp.where(idx == m, gate, 0.0), axis=axis, keepdims=True)
        beats = jnp.logical_or(gm > gate, jnp.logical_and(gm == gate, m < idx))
        return rank + jnp.where(beats, 1, 0).astype(I32)

    rank = lax.fori_loop(0, n_past_blocks, body, jnp.zeros(gate.shape, I32))
    sel = jnp.logical_and(idx < n_past_blocks, rank < MOBA_TOPK)
    return jnp.where(sel, 0.0, NEG_INF)


def _moba_prompt_kernel(bq_ref, bk_ref, bv_ref, mix_ref, o_ref, k_s, vt_s, km_s, kmb_s, qt_s, sb_s, acc_s,
                        *, seq, nbp):
    del mix_ref
    tq, blk = MOBA_TQ, MOBA_BLOCK
    qi = pl.program_id(1)
    q0 = qi * tq
    nb = seq // blk

    @pl.when(qi == 0)
    def _prep():
        km_s[...] = jnp.zeros_like(km_s)

        def body(n, _):
            rows = pl.ds(pl.multiple_of(n * blk, blk), blk)
            kk = _prep_kv_t(bk_ref, bv_ref, k_s, vt_s, rows, B_KV_HEADS)
            km_s[pl.ds(n, 1), :] = jnp.mean(kk, axis=0, keepdims=True)
            return 0

        lax.fori_loop(0, nb, body, 0)
        for c in range(B_KV_HEADS):
            kmb_s[c] = km_s[:, c * HEAD_DIM:(c + 1) * HEAD_DIM].astype(BF16)

    own = q0 // blk
    own_rows = pl.ds(pl.multiple_of(own * blk, blk), blk)
    bq = bq_ref[...]
    _stage_queries_t(bq, qt_s, B_KV_HEADS, B_GROUP)
    gates = []
    for c in range(B_KV_HEADS):
        qg = jnp.concatenate(
            [bq[:, (c * B_GROUP + g) * HEAD_DIM:(c * B_GROUP + g + 1) * HEAD_DIM].T for g in range(B_GROUP)],
            axis=1).astype(BF16)
        gates.append(jnp.dot(kmb_s[c], qg, preferred_element_type=F32)[:nbp])
    sel_bias = _rank_select(jnp.concatenate(gates, axis=1), own, axis=0)
    for hh in range(B_HEADS):
        sb_s[hh] = sel_bias[:, hh * tq:(hh + 1) * tq]

    qpos = q0 + lax.broadcasted_iota(I32, (blk, tq), 1)
    kpos = own * blk + lax.broadcasted_iota(I32, (blk, tq), 0)
    causal_bias = jnp.where(kpos <= qpos, 0.0, NEG_INF)
    ms, ls = [], []
    for hh in range(B_HEADS):
        c = hh // B_GROUP
        m, l, acc = _flash_step_t(k_s[c, own_rows, :], vt_s[c, :, own_rows], _head_q(qt_s, hh, tq), causal_bias,
                                  jnp.full((1, tq), NEG_INF, F32), jnp.zeros((1, tq), F32),
                                  jnp.zeros((HEAD_DIM, tq), F32))
        acc_s[hh] = acc
        ms.append(m)
        ls.append(l)

    odd = own % 2

    def pair_body(p, carry):
        ms, ls = carry
        b0 = jnp.maximum(2 * p - odd, 0)
        rows = pl.ds(pl.multiple_of(b0 * blk, blk), 2 * blk)
        skip_second = jnp.where(jnp.logical_and(p == 0, odd == 1), NEG_INF, 0.0)
        new_m, new_l = [], []
        for hh in range(B_HEADS):
            c = hh // B_GROUP
            bias = jnp.concatenate(
                [jnp.broadcast_to(sb_s[hh, pl.ds(b0, 1), :], (blk, tq)),
                 jnp.broadcast_to(sb_s[hh, pl.ds(b0 + 1, 1), :] + skip_second, (blk, tq))], axis=0)
            m, l, acc = _flash_step_t(k_s[c, rows, :], vt_s[c, :, rows], _head_q(qt_s, hh, tq), bias,
                                      ms[hh], ls[hh], acc_s[hh])
            acc_s[hh] = acc
            new_m.append(m)
            new_l.append(l)
        return tuple(new_m), tuple(new_l)

    _, ls = lax.fori_loop(0, (own + 1) // 2, pair_body, (tuple(ms), tuple(ls)))
    _write_heads(acc_s, ls, o_ref, B_HEADS)


def moba_prompt(h, mix, batch, seq):
    tq = MOBA_TQ
    nq = seq // tq
    nbp = -(-(seq // MOBA_BLOCK) // SUBLANES) * SUBLANES
    return pl.pallas_call(
        functools.partial(_moba_prompt_kernel, seq=seq, nbp=nbp),
        grid=(batch, nq),
        in_specs=[pl.BlockSpec((tq, B_DIM), lambda b, i: (b * nq + i, H_BQ // B_DIM)),
                  pl.BlockSpec((seq, B_KV_DIM), lambda b, i: (b, H_BK // B_KV_DIM)),
                  pl.BlockSpec((seq, B_KV_DIM), lambda b, i: (b, H_BV // B_KV_DIM)),
                  pl.BlockSpec(memory_space=pl.ANY)],
        out_specs=pl.BlockSpec((tq, B_DIM), lambda b, i: (b * nq + i, A_DIM // B_DIM)),
        out_shape=jax.ShapeDtypeStruct(mix.shape, mix.dtype),
        input_output_aliases={3: 0},
        scratch_shapes=[pltpu.VMEM((B_KV_HEADS, seq, HEAD_DIM), BF16),
                        pltpu.VMEM((B_KV_HEADS, HEAD_DIM, seq), BF16),
                        pltpu.VMEM((LANES, B_KV_DIM), F32),
                        pltpu.VMEM((B_KV_HEADS, LANES, HEAD_DIM), BF16),
                        pltpu.VMEM((B_KV_HEADS, HEAD_DIM, B_GROUP * tq), BF16),
                        pltpu.VMEM((B_HEADS, nbp, tq), F32),
                        pltpu.VMEM((B_HEADS, HEAD_DIM, tq), F32)],
        compiler_params=_cparams(("arbitrary", "arbitrary")),
        name="moba_prompt",
    )(h, h, h, mix)


POOL_HALO = 16


def _pool_groups(u, window_sum, cnt_of, w_ref, sc_ref, o_ref):
    for g, w in enumerate(POOL_WINDOWS):
        cols = slice(g * POOL_GROUP_DIM, (g + 1) * POOL_GROUP_DIM)
        ug = u[:, cols]
        d = window_sum(g, w, ug) / cnt_of(w) - ug
        y = jnp.dot(d.astype(BF16), w_ref[g], preferred_element_type=F32)
        o_ref[:, cols] = y * sc_ref[:, cols]


def _pool_prompt_kernel(u_ref, halo_ref, w_ref, sc_ref, mix_ref, o_ref, ext_s):
    del mix_ref
    tt = POOL_TT
    i = pl.program_id(1)
    u = u_ref[...]
    ext_s[0:POOL_HALO, :] = jnp.where(i == 0, 0.0, halo_ref[...])
    ext_s[POOL_HALO:POOL_HALO + tt, :] = u
    t = i * tt + lax.broadcasted_iota(I32, (tt, 1), 0)

    def window_sum(g, w, ug):
        s = ug
        for k in range(1, w):
            s = s + ext_s[pl.ds(POOL_HALO - k, tt), g * POOL_GROUP_DIM:(g + 1) * POOL_GROUP_DIM]
        return s

    _pool_groups(u, window_sum, lambda w: jnp.minimum(w, t + 1).astype(F32), w_ref, sc_ref, o_ref)


def pool_prompt(h, mix, pool_w, pool_scale, layer, batch, seq):
    tt = POOL_TT
    nt = seq // tt
    halo_blocks = tt // POOL_HALO
    return pl.pallas_call(
        _pool_prompt_kernel,
        grid=(batch, nt),
        in_specs=[pl.BlockSpec((tt, POOL_DIM), lambda b, i: (b * nt + i, H_PU // POOL_DIM)),
                  pl.BlockSpec((POOL_HALO, POOL_DIM),
                               lambda b, i: (jnp.maximum((b * nt + i) * halo_blocks - 1, 0), H_PU // POOL_DIM)),
                  pl.BlockSpec((None, len(POOL_WINDOWS), POOL_GROUP_DIM, POOL_GROUP_DIM),
                               lambda b, i: (layer, 0, 0, 0)),
                  pl.BlockSpec((None, 1, POOL_DIM), lambda b, i: (layer, 0, 0)),
                  pl.BlockSpec(memory_space=pl.ANY)],
        out_specs=pl.BlockSpec((tt, POOL_DIM), lambda b, i: (b * nt + i, (A_DIM + B_DIM) // POOL_DIM)),
        out_shape=jax.ShapeDtypeStruct(mix.shape, mix.dtype),
        input_output_aliases={4: 0},
        scratch_shapes=[pltpu.VMEM((POOL_HALO + tt, POOL_DIM), F32)],
        compiler_params=_cparams(("arbitrary", "arbitrary")),
        name="pool_prompt",
    )(h, h, pool_w, pool_scale, mix)


def _pool_sample_kernel(u_ref, st_ref, w_ref, sc_ref, mix_ref, o_ref, ext_s, *, db, ts, n_past):
    del mix_ref
    u = u_ref[...]
    for b in range(db):
        ext_s[b, POOL_HALO - POOL_STATE:POOL_HALO, :] = st_ref[b]
        ext_s[b, POOL_HALO:POOL_HALO + ts, :] = u[b * ts:(b + 1) * ts]
    t = n_past + lax.broadcasted_iota(I32, (db * ts, 1), 0) % ts

    def window_sum(g, w, ug):
        parts = []
        for b in range(db):
            s = ug[b * ts:(b + 1) * ts]
            for k in range(1, w):
                s = s + ext_s[b, pl.ds(POOL_HALO - k, ts), g * POOL_GROUP_DIM:(g + 1) * POOL_GROUP_DIM]
            parts.append(s)
        return jnp.concatenate(parts, axis=0)

    _pool_groups(u, window_sum, lambda w: jnp.minimum(w, t + 1).astype(F32), w_ref, sc_ref, o_ref)


def pool_sample(h, mix, state_pool, pool_w, pool_scale, layer, n_prompt, db, ts, n_past):
    rows = db * ts
    return pl.pallas_call(
        functools.partial(_pool_sample_kernel, db=db, ts=ts, n_past=n_past),
        grid=(1,),
        in_specs=[pl.BlockSpec((rows, POOL_DIM), lambda i: (n_prompt // rows, H_PU // POOL_DIM)),
                  pl.BlockSpec((None, db, POOL_STATE, POOL_DIM), lambda i: (layer, 0, 0, 0)),
                  pl.BlockSpec((None, len(POOL_WINDOWS), POOL_GROUP_DIM, POOL_GROUP_DIM),
                               lambda i: (layer, 0, 0, 0)),
                  pl.BlockSpec((None, 1, POOL_DIM), lambda i: (layer, 0, 0)),
                  pl.BlockSpec(memory_space=pl.ANY)],
        out_specs=pl.BlockSpec((rows, POOL_DIM), lambda i: (n_prompt // rows, (A_DIM + B_DIM) // POOL_DIM)),
        out_shape=jax.ShapeDtypeStruct(mix.shape, mix.dtype),
        input_output_aliases={4: 0},
        scratch_shapes=[pltpu.VMEM((db, POOL_HALO + ts, POOL_DIM), F32)],
        compiler_params=_cparams(("arbitrary",)),
        name="pool_sample",
    )(h, state_pool, pool_w, pool_scale, mix)


def _memattn_heads(q_ref, head_k, head_v, o_ref):
    scale = MEM_HEAD_DIM ** -0.5
    for hh in range(MEM_HEADS):
        cols = slice(hh * MEM_HEAD_DIM, (hh + 1) * MEM_HEAD_DIM)
        qh = q_ref[:, cols].astype(BF16)
        lg = lax.dot_general(qh, head_k(hh).astype(BF16), (((1,), (1,)), ((), ())),
                             preferred_element_type=F32) * scale
        m = jnp.max(lg, axis=1, keepdims=True)
        p = jnp.exp(lg - m)
        l = jnp.sum(p, axis=1, keepdims=True)
        o_ref[:, cols] = jnp.dot(p.astype(BF16), head_v(hh).astype(BF16), preferred_element_type=F32) / l


def _memattn_prompt_kernel(q_ref, mk_ref, mv_ref, o_ref):
    _memattn_heads(q_ref, lambda hh: mk_ref[:, hh * MEM_HEAD_DIM:(hh + 1) * MEM_HEAD_DIM],
                   lambda hh: mv_ref[:, hh * MEM_HEAD_DIM:(hh + 1) * MEM_HEAD_DIM], o_ref)


def _memattn_sample_kernel(q_ref, mk_ref, mv_ref, o_in_ref, o_ref):
    del o_in_ref
    _memattn_heads(q_ref, lambda hh: mk_ref[:, hh, :], lambda hh: mv_ref[:, hh, :], o_ref)


def memattn_prompt(q, mkv, batch, seq):
    tr = min(MEM_TR, seq)
    nt = seq // tr
    return pl.pallas_call(
        _memattn_prompt_kernel,
        grid=(batch, nt),
        in_specs=[pl.BlockSpec((tr, MEM_DIM), lambda b, i: (b * nt + i, 0)),
                  pl.BlockSpec((MEM_TOKENS, MEM_DIM), lambda b, i: (b, 0)),
                  pl.BlockSpec((MEM_TOKENS, MEM_DIM), lambda b, i: (b, 1))],
        out_specs=pl.BlockSpec((tr, MEM_DIM), lambda b, i: (b * nt + i, 0)),
        out_shape=jax.ShapeDtypeStruct(q.shape, F32),
        compiler_params=_cparams(("arbitrary", "arbitrary")),
        name="memattn_prompt",
    )(q, mkv, mkv)


def memattn_sample(q, cache_k, cache_v, o, layer, n_prompt, db, ts):
    cache_spec = pl.BlockSpec((None, None, MEM_TOKENS, MEM_HEADS, MEM_HEAD_DIM), lambda b: (layer, b, 0, 0, 0))
    return pl.pallas_call(
        _memattn_sample_kernel,
        grid=(db,),
        in_specs=[pl.BlockSpec((ts, MEM_DIM), lambda b: (n_prompt // ts + b, 0)),
                  cache_spec, cache_spec,
                  pl.BlockSpec(memory_space=pl.ANY)],
        out_specs=pl.BlockSpec((ts, MEM_DIM), lambda b: (n_prompt // ts + b, 0)),
        out_shape=jax.ShapeDtypeStruct(o.shape, o.dtype),
        input_output_aliases={3: 0},
        compiler_params=_cparams(("arbitrary",)),
        name="memattn_sample",
    )(q, cache_k, cache_v, o)


def _page_specs(npg, page_shape, layer, step_of):
    zeros = (0,) * len(page_shape)

    def spec(i):
        return pl.BlockSpec((None, None) + page_shape,
                            lambda b, j, pt: (layer, pt[b, step_of(j) * npg + i]) + zeros)
    return [spec(i) for i in range(npg)]


def _pad_rows(x, rows):
    return jnp.concatenate([x, jnp.zeros((rows - x.shape[0], x.shape[1]), x.dtype)], axis=0)


def _gather_head(pages, c):
    return jnp.concatenate([p[:, c, :] for p in pages], axis=0)


def _dsa_sample_select_kernel(pt_ref, iq_ref, ikw_ref, *refs, npg, n_steps, n_past, ts, k_sel):
    del pt_ref
    pages = refs[:npg]
    bias_ref, keys_s = refs[npg], refs[npg + 1]
    j = pl.program_id(1)
    ck = npg * PAGE_SIZE
    lp = n_past + LANES
    iq = iq_ref[...] * (IDX_DIM ** -0.5)
    iq_st = jnp.concatenate([iq[:, h * IDX_DIM:(h + 1) * IDX_DIM] for h in range(IDX_HEADS)],
                            axis=0).astype(BF16)
    ikw = ikw_ref[...]
    iw = ikw[:, IDX_DIM:IDX_DIM + IDX_HEADS] * (IDX_HEADS ** -0.5)

    def scores(ik_t):
        s = jnp.dot(iq_st, ik_t.astype(BF16), preferred_element_type=F32)
        acc = jnp.zeros((ts, ik_t.shape[1]), F32)
        for h in range(IDX_HEADS):
            acc = acc + jnp.maximum(s[h * ts:(h + 1) * ts], 0.0) * iw[:, h:h + 1]
        return acc

    ik_past = jnp.concatenate([pages[i][...] for i in range(npg)], axis=1)
    keys_s[:, pl.ds(pl.multiple_of(j * ck, ck), ck)] = _sortable_key(scores(ik_past))

    @pl.when(j == n_steps - 1)
    def _select():
        sc_new = scores(_pad_rows(ikw, LANES).T[:IDX_DIM])
        causal = lax.broadcasted_iota(I32, (ts, LANES), 1) <= lax.broadcasted_iota(I32, (ts, LANES), 0)
        keys_s[:, n_past:lp] = jnp.where(causal, _sortable_key(sc_new), INT_MIN)

        def count_ge(cand):
            return jnp.sum(jnp.where(keys_s[...] >= cand, 1, 0).astype(I32), axis=1, keepdims=True)

        thr, cnt_thr = _kth_largest(count_ge, (ts, 1), k_sel)
        has_tie = jnp.logical_and(cnt_thr > k_sel, thr > INT_MIN)
        any_tie = jnp.max(has_tie.astype(I32)) > 0
        thr_c = jnp.maximum(thr, INT_MIN + 1)

        @pl.when(jnp.logical_not(any_tie))
        def _fast():
            bias_ref[...] = jnp.where(keys_s[...] >= thr_c, 0.0, NEG_INF)

        @pl.when(any_tie)
        def _ties():
            need = (k_sel - count_ge(thr_c + 1)).astype(F32)
            tri = jnp.where(lax.broadcasted_iota(I32, (LANES, LANES), 0)
                            <= lax.broadcasted_iota(I32, (LANES, LANES), 1), 1.0, 0.0).astype(BF16)

            def body(c, run):
                cols = pl.ds(pl.multiple_of(c * LANES, LANES), LANES)
                kchunk = keys_s[:, cols]
                eq = kchunk == thr_c
                eqf = jnp.where(eq, 1.0, 0.0)
                pre = jnp.dot(eqf.astype(BF16), tri, preferred_element_type=F32) + run
                sel = jnp.logical_or(kchunk > thr_c, jnp.logical_and(eq, pre <= need))
                bias_ref[:, cols] = jnp.where(sel, 0.0, NEG_INF)
                return run + jnp.sum(eqf, axis=1, keepdims=True)

            lax.fori_loop(0, lp // LANES, body, jnp.zeros((ts, 1), F32))


def dsa_sample_select(page_table, h, cache_idx, layer, n_prompt, db, ts):
    n_pages = page_table.shape[1]
    n_past = n_pages * PAGE_SIZE
    npg = min(PAGES_PER_STEP, n_pages)
    n_steps = n_pages // npg
    k_sel = min(DSA_TOPK, (n_past + ts) // 4)
    lp = n_past + LANES
    kern = functools.partial(_dsa_sample_select_kernel, npg=npg, n_steps=n_steps, n_past=n_past, ts=ts, k_sel=k_sel)
    row_blk = n_prompt // ts
    grid_spec = pltpu.PrefetchScalarGridSpec(
        num_scalar_prefetch=1,
        grid=(db, n_steps),
        in_specs=[pl.BlockSpec((ts, IDX_HEADS * IDX_DIM), lambda b, j, pt: (row_blk + b, H_IQ // (IDX_HEADS * IDX_DIM))),
                  pl.BlockSpec((ts, LANES), lambda b, j, pt: (row_blk + b, H_IKW // LANES))]
        + _page_specs(npg, (IDX_DIM, PAGE_SIZE), layer, lambda j: j),
        out_specs=pl.BlockSpec((None, ts, lp), lambda b, j, pt: (b, 0, 0)),
        scratch_shapes=[pltpu.VMEM((ts, lp), I32)],
    )
    return pl.pallas_call(
        kern,
        grid_spec=grid_spec,
        out_shape=jax.ShapeDtypeStruct((db, ts, lp), F32),
        compiler_params=_cparams(("arbitrary", "arbitrary")),
        name="dsa_sample_select",
    )(page_table, h, h, *([cache_idx] * npg))


def _tile_bias_t(bs, group, ts):
    full = jnp.concatenate([bs] * group + [jnp.zeros((LANES - group * ts, bs.shape[1]), bs.dtype)], axis=0)
    return full.T


def _stack_heads_t(q, c, group, ts):
    qc = jnp.concatenate(
        [q[:, (c * group + g) * HEAD_DIM:(c * group + g + 1) * HEAD_DIM] for g in range(group)]
        + [jnp.zeros((LANES - group * ts, HEAD_DIM), q.dtype)], axis=0)
    return qc.T.astype(BF16)


def _dsa_sample_attend_kernel(pt_ref, q_ref, kn_ref, vn_ref, bias_ref, *refs, npg, n_steps, n_past, ts):
    del pt_ref
    kp, vp = refs[:npg], refs[npg:2 * npg]
    o_ref = refs[2 * npg + 1]
    qT_s, m_s, l_s, accT_s = refs[2 * npg + 2:]
    j = pl.program_id(1)
    ck = npg * PAGE_SIZE
    scale = HEAD_DIM ** -0.5

    @pl.when(j == 0)
    def _init():
        q = q_ref[...]
        for c in range(A_KV_HEADS):
            qT_s[c] = _stack_heads_t(q, c, A_GROUP, ts)
        m_s[...] = jnp.full(m_s.shape, NEG_INF, F32)
        l_s[...] = jnp.zeros_like(l_s)
        accT_s[...] = jnp.zeros_like(accT_s)

    def update(c, kc, vc, bias_t):
        lg = jnp.dot(kc.astype(BF16), qT_s[c], preferred_element_type=F32) * scale + bias_t
        m_old = m_s[c]
        m_new = jnp.maximum(m_old, jnp.max(lg, axis=0, keepdims=True))
        alpha = jnp.exp(m_old - m_new)
        p = jnp.exp(lg - m_new)
        l_s[c] = alpha * l_s[c] + jnp.sum(p, axis=0, keepdims=True)
        accT_s[c] = alpha * accT_s[c] + jnp.dot(vc.T.astype(BF16), p.astype(BF16), preferred_element_type=F32)
        m_s[c] = m_new

    bias_t = _tile_bias_t(bias_ref[:, pl.ds(pl.multiple_of(j * ck, ck), ck)], A_GROUP, ts)
    for c in range(A_KV_HEADS):
        update(c, _gather_head(kp, c), _gather_head(vp, c), bias_t)

    @pl.when(j == n_steps - 1)
    def _finish():
        bias_n = _tile_bias_t(bias_ref[:, n_past:n_past + LANES], A_GROUP, ts)
        for c in range(A_KV_HEADS):
            cols = slice(c * HEAD_DIM, (c + 1) * HEAD_DIM)
            update(c, _pad_rows(kn_ref[:, cols], LANES), _pad_rows(vn_ref[:, cols], LANES), bias_n)
            out = (accT_s[c] / l_s[c]).T
            for g in range(A_GROUP):
                hcol = (c * A_GROUP + g) * HEAD_DIM
                o_ref[:, hcol:hcol + HEAD_DIM] = out[g * ts:(g + 1) * ts]


def dsa_sample_attend(page_table, h, bias, cache_k, cache_v, mix, layer, n_prompt, db, ts):
    n_pages = page_table.shape[1]
    n_past = n_pages * PAGE_SIZE
    npg = min(PAGES_PER_STEP, n_pages)
    n_steps = n_pages // npg
    lp = n_past + LANES
    kern = functools.partial(_dsa_sample_attend_kernel, npg=npg, n_steps=n_steps, n_past=n_past, ts=ts)
    row_blk = n_prompt // ts
    kv_tail = (PAGE_SIZE, A_KV_HEADS, HEAD_DIM)
    grid_spec = pltpu.PrefetchScalarGridSpec(
        num_scalar_prefetch=1,
        grid=(db, n_steps),
        in_specs=[pl.BlockSpec((ts, A_DIM), lambda b, j, pt: (row_blk + b, H_AQ // A_DIM)),
                  pl.BlockSpec((ts, A_KV_DIM), lambda b, j, pt: (row_blk + b, H_AK // A_KV_DIM)),
                  pl.BlockSpec((ts, A_KV_DIM), lambda b, j, pt: (row_blk + b, H_AV // A_KV_DIM)),
                  pl.BlockSpec((None, ts, lp), lambda b, j, pt: (b, 0, 0))]
        + _page_specs(npg, kv_tail, layer, lambda j: j)
        + _page_specs(npg, kv_tail, layer, lambda j: j)
        + [pl.BlockSpec(memory_space=pl.ANY)],
        out_specs=pl.BlockSpec((ts, A_DIM), lambda b, j, pt: (row_blk + b, 0)),
        scratch_shapes=[pltpu.VMEM((A_KV_HEADS, HEAD_DIM, LANES), BF16),
                        pltpu.VMEM((A_KV_HEADS, 1, LANES), F32),
                        pltpu.VMEM((A_KV_HEADS, 1, LANES), F32),
                        pltpu.VMEM((A_KV_HEADS, HEAD_DIM, LANES), F32)],
    )
    return pl.pallas_call(
        kern,
        grid_spec=grid_spec,
        out_shape=jax.ShapeDtypeStruct(mix.shape, mix.dtype),
        input_output_aliases={5 + 2 * npg: 0},
        compiler_params=_cparams(("arbitrary", "arbitrary")),
        name="dsa_sample_attend",
    )(page_table, h, h, h, bias, *([cache_k] * npg), *([cache_v] * npg), mix)


def _moba_sample_kernel(pt_ref, q_ref, kn_ref, vn_ref, *refs, npg, n_kv_steps, ts):
    del pt_ref
    kp, vp = refs[:npg], refs[npg:2 * npg]
    o_ref = refs[2 * npg + 1]
    qT_s, p_s, km_s, sb_s, pn_s, l_s, accT_s = refs[2 * npg + 2:]
    j = pl.program_id(1)
    ck = npg * PAGE_SIZE
    blk = MOBA_BLOCK
    bps = ck // blk
    nbk = n_kv_steps * bps
    scale = HEAD_DIM ** -0.5

    @pl.when(j == 0)
    def _init():
        q = q_ref[...]
        for c in range(B_KV_HEADS):
            qT_s[c] = _stack_heads_t(q, c, B_GROUP, ts)
        km_s[...] = jnp.zeros_like(km_s)
        accT_s[...] = jnp.zeros_like(accT_s)

    @pl.when(j < n_kv_steps)
    def _k_phase():
        for c in range(B_KV_HEADS):
            kc = _gather_head(kp, c)
            means = jnp.concatenate(
                [jnp.mean(kc[n * blk:(n + 1) * blk], axis=0, keepdims=True) for n in range(bps)], axis=0)
            km_s[c, pl.ds(pl.multiple_of(j * bps, bps), bps), :] = means
            p_s[c, pl.ds(pl.multiple_of(j * ck, ck), ck), :] = (
                jnp.dot(kc.astype(BF16), qT_s[c], preferred_element_type=F32) * scale)

    @pl.when(j == n_kv_steps - 1)
    def _select_softmax():
        key_i = lax.broadcasted_iota(I32, (LANES, LANES), 0)
        q_of_lane = lax.broadcasted_iota(I32, (LANES, LANES), 1) % ts
        new_ok = jnp.logical_and(key_i <= q_of_lane, key_i < ts)
        for c in range(B_KV_HEADS):
            cols = slice(c * HEAD_DIM, (c + 1) * HEAD_DIM)
            gate_t = jnp.dot(km_s[c].astype(BF16), qT_s[c], preferred_element_type=F32)
            sb_s[c] = _rank_select(gate_t, nbk, axis=0)
            lg_new = jnp.dot(_pad_rows(kn_ref[:, cols], LANES).astype(BF16), qT_s[c],
                             preferred_element_type=F32) * scale
            lg_new = jnp.where(new_ok, lg_new, NEG_INF)

            def blk_logits(n, c=c):
                rows = pl.ds(pl.multiple_of(n * blk, blk), blk)
                return rows, p_s[c, rows, :] + sb_s[c, pl.ds(n, 1), :]

            def max_body(n, m):
                _, lg = blk_logits(n)
                return jnp.maximum(m, jnp.max(lg, axis=0, keepdims=True))

            m = lax.fori_loop(0, nbk, max_body, jnp.max(lg_new, axis=0, keepdims=True))

            def exp_body(n, l, c=c, m=m):
                rows, lg = blk_logits(n)
                p = jnp.exp(lg - m)
                p_s[c, rows, :] = p
                return l + jnp.sum(p, axis=0, keepdims=True)

            p_new = jnp.exp(lg_new - m)
            pn_s[c] = p_new
            l_s[c] = lax.fori_loop(0, nbk, exp_body, jnp.sum(p_new, axis=0, keepdims=True))

    @pl.when(j >= n_kv_steps)
    def _v_phase():
        jj = j - n_kv_steps
        for c in range(B_KV_HEADS):
            vc = _gather_head(vp, c)
            p = p_s[c, pl.ds(pl.multiple_of(jj * ck, ck), ck), :]
            accT_s[c] += jnp.dot(vc.T.astype(BF16), p.astype(BF16), preferred_element_type=F32)

    @pl.when(j == 2 * n_kv_steps - 1)
    def _finish():
        for c in range(B_KV_HEADS):
            cols = slice(c * HEAD_DIM, (c + 1) * HEAD_DIM)
            vn = _pad_rows(vn_ref[:, cols], LANES)
            acc = accT_s[c] + jnp.dot(vn.T.astype(BF16), pn_s[c].astype(BF16), preferred_element_type=F32)
            out = (acc / l_s[c]).T
            for g in range(B_GROUP):
                hcol = (c * B_GROUP + g) * HEAD_DIM
                o_ref[:, hcol:hcol + HEAD_DIM] = out[g * ts:(g + 1) * ts]


def moba_sample(page_table, h, cache_k, cache_v, mix, layer, n_prompt, db, ts):
    n_pages = page_table.shape[1]
    n_past = n_pages * PAGE_SIZE
    npg = min(PAGES_PER_STEP, n_pages)
    n_kv = n_pages // npg
    kern = functools.partial(_moba_sample_kernel, npg=npg, n_kv_steps=n_kv, ts=ts)
    row_blk = n_prompt // ts
    kv_tail = (PAGE_SIZE, B_KV_HEADS, HEAD_DIM)
    grid_spec = pltpu.PrefetchScalarGridSpec(
        num_scalar_prefetch=1,
        grid=(db, 2 * n_kv),
        in_specs=[pl.BlockSpec((ts, B_DIM), lambda b, j, pt: (row_blk + b, H_BQ // B_DIM)),
                  pl.BlockSpec((ts, B_KV_DIM), lambda b, j, pt: (row_blk + b, H_BK // B_KV_DIM)),
                  pl.BlockSpec((ts, B_KV_DIM), lambda b, j, pt: (row_blk + b, H_BV // B_KV_DIM))]
        + _page_specs(npg, kv_tail, layer, lambda j: jnp.minimum(j, n_kv - 1))
        + _page_specs(npg, kv_tail, layer, lambda j: jnp.maximum(j - n_kv, 0))
        + [pl.BlockSpec(memory_space=pl.ANY)],
        out_specs=pl.BlockSpec((ts, B_DIM), lambda b, j, pt: (row_blk + b, A_DIM // B_DIM)),
        scratch_shapes=[pltpu.VMEM((B_KV_HEADS, HEAD_DIM, LANES), BF16),
                        pltpu.VMEM((B_KV_HEADS, n_past, LANES), F32),
                        pltpu.VMEM((B_KV_HEADS, LANES, HEAD_DIM), F32),
                        pltpu.VMEM((B_KV_HEADS, LANES, LANES), F32),
                        pltpu.VMEM((B_KV_HEADS, LANES, LANES), F32),
                        pltpu.VMEM((B_KV_HEADS, 1, LANES), F32),
                        pltpu.VMEM((B_KV_HEADS, HEAD_DIM, LANES), F32)],
    )
    return pl.pallas_call(
        kern,
        grid_spec=grid_spec,
        out_shape=jax.ShapeDtypeStruct(mix.shape, mix.dtype),
        input_output_aliases={4 + 2 * npg: 0},
        compiler_params=_cparams(("arbitrary", "arbitrary")),
        name="moba_sample",
    )(page_table, h, h, h, *([cache_k] * npg), *([cache_v] * npg), mix)


def _row_tile(n_rows, target):
    best = None
    for t in range(16, target + 1, 16):
        if n_rows % t == 0:
            best = t
    assert best is not None, n_rows
    return best


def _pack_w_in(w_in):
    order = ("aq", "bq", "iq", "pu", "ak", "av", "bk", "bv", "ik", "iw")
    parts = [w_in[..., _SRC[n][0]:_SRC[n][0] + _SRC[n][1]] for n in order]
    used = sum(_SRC[n][1] for n in order)
    parts.append(jnp.zeros(w_in.shape[:-1] + (H_DIM - used,), w_in.dtype))
    return jnp.concatenate(parts, axis=-1).astype(BF16)


def kernel(x_prompt, x_sample, cache_a_k, cache_a_v, cache_a_idx, cache_b_k, cache_b_v, state_pool,
           cache_mem_k, cache_mem_v, page_table, mem_prompt, w_in, w_out, pool_w, pool_scale, ln1_g, ln1_b,
           w_mem_q, w_mem_k, w_mem_v, w_mem_o, ln2_g, ln2_b, w_ffn_1, w_ffn_3, w_ffn_2, ln3_g, ln3_b):
    batch, seq, d = x_prompt.shape
    db, ts, _ = x_sample.shape
    depth = w_in.shape[0]
    n_past = page_table.shape[1] * PAGE_SIZE
    n_prompt = batch * seq
    n_rows = n_prompt + db * ts
    tm = _row_tile(n_rows, 700)

    w_in_p = _pack_w_in(w_in)
    w_out_b = w_out.astype(BF16)
    w_mq_b = w_mem_q.astype(BF16)
    w_mkv_b = jnp.concatenate([w_mem_k, w_mem_v], axis=-1).astype(BF16)
    w_mo_b = w_mem_o.astype(BF16)
    w1_b, w3_b, w2_b = w_ffn_1.astype(BF16), w_ffn_3.astype(BF16), w_ffn_2.astype(BF16)
    pool_w_b = pool_w.astype(BF16)
    pool_scale3 = pool_scale.reshape(depth, 1, POOL_DIM)
    ln = [a.reshape(depth, 1, d) for a in (ln1_g, ln1_b, ln2_g, ln2_b, ln3_g, ln3_b)]
    mem = mem_prompt.reshape(batch * MEM_TOKENS, d)
    idx_t = jnp.swapaxes(cache_a_idx, 2, 3)

    x = jnp.concatenate([x_prompt.reshape(n_prompt, d), x_sample.reshape(db * ts, d)], axis=0)
    outs = {k: [] for k in ("pa_k", "pa_v", "pa_i", "pb_k", "pb_v", "p_pool", "pm_k", "pm_v",
                            "sa_k", "sa_v", "sa_i", "sb_k", "sb_v", "s_pool")}
    for l in range(depth):
        h = matmul(x, w_in_p, l, tm, 768)
        mkv = matmul(mem, w_mkv_b, l, _row_tile(mem.shape[0], 512), MEM_DIM)

        mix = dsa_prompt(h, batch, seq, n_rows)
        mix = moba_prompt(h, mix, batch, seq)
        mix = pool_prompt(h, mix, pool_w_b, pool_scale3, l, batch, seq)
        bias = dsa_sample_select(page_table, h, idx_t, l, n_prompt, db, ts)
        mix = dsa_sample_attend(page_table, h, bias, cache_a_k, cache_a_v, mix, l, n_prompt, db, ts)
        mix = moba_sample(page_table, h, cache_b_k, cache_b_v, mix, l, n_prompt, db, ts)
        mix = pool_sample(h, mix, state_pool, pool_w_b, pool_scale3, l, n_prompt, db, ts, n_past)

        x1 = matmul_residual_ln(mix, w_out_b, x, ln[0], ln[1], l, tm, 512)
        q = matmul(x1, w_mq_b, l, tm, MEM_DIM)
        o = memattn_prompt(q, mkv, batch, seq)
        o = memattn_sample(q, cache_mem_k, cache_mem_v, o, l, n_prompt, db, ts)
        x2 = matmul_residual_ln(o, w_mo_b, x1, ln[2], ln[3], l, tm, MEM_DIM)
        x = ffn_ln(x2, w1_b, w3_b, w2_b, ln[4], ln[5], l, tm, 512)

        hp, hs = h[:n_prompt], h[n_prompt:]

        def cols(a, off, width, lead):
            return a[:, off:off + width].reshape(lead)

        kv_p, kv_s = (batch, seq, A_KV_HEADS, HEAD_DIM), (db, ts, A_KV_HEADS, HEAD_DIM)
        outs["pa_k"].append(cols(hp, H_AK, A_KV_DIM, kv_p))
        outs["pa_v"].append(cols(hp, H_AV, A_KV_DIM, kv_p))
        outs["pa_i"].append(cols(hp, H_IKW, IDX_DIM, (batch, seq, IDX_DIM)))
        outs["pb_k"].append(cols(hp, H_BK, B_KV_DIM, kv_p))
        outs["pb_v"].append(cols(hp, H_BV, B_KV_DIM, kv_p))
        pu_p = cols(hp, H_PU, POOL_DIM, (batch, seq, POOL_DIM))
        outs["p_pool"].append(pu_p[:, seq - POOL_STATE:])
        outs["pm_k"].append(mkv[:, :MEM_DIM].reshape(batch, MEM_TOKENS, MEM_HEADS, MEM_HEAD_DIM))
        outs["pm_v"].append(mkv[:, MEM_DIM:].reshape(batch, MEM_TOKENS, MEM_HEADS, MEM_HEAD_DIM))
        outs["sa_k"].append(cols(hs, H_AK, A_KV_DIM, kv_s))
        outs["sa_v"].append(cols(hs, H_AV, A_KV_DIM, kv_s))
        outs["sa_i"].append(cols(hs, H_IKW, IDX_DIM, (db, ts, IDX_DIM)))
        outs["sb_k"].append(cols(hs, H_BK, B_KV_DIM, kv_s))
        outs["sb_v"].append(cols(hs, H_BV, B_KV_DIM, kv_s))
        pu_s = cols(hs, H_PU, POOL_DIM, (db, ts, POOL_DIM))
        outs["s_pool"].append(jnp.concatenate([state_pool[l], pu_s], axis=1)[:, -POOL_STATE:])

    st = {k: jnp.stack(v) for k, v in outs.items()}
    return (x[:n_prompt].reshape(batch, seq, d), x[n_prompt:].reshape(db, ts, d),
            st["pa_k"], st["pa_v"], st["pa_i"], st["pb_k"], st["pb_v"], st["p_pool"], st["pm_k"], st["pm_v"],
            st["sa_k"], st["sa_v"], st["sa_i"], st["sb_k"], st["sb_v"], st["s_pool"])
```

```python
import functools
import math

import jax
import jax.numpy as jnp
from jax import lax
from jax.experimental import pallas as pl
from jax.experimental.pallas import tpu as pltpu

F32 = jnp.float32
BF16 = jnp.bfloat16
I32 = jnp.int32

D_MODEL = 2048
DEPTH = 4
PAGE_SIZE = 128
HEAD_DIM = 128
A_HEADS = 6
A_KV_HEADS = 2
A_GROUP = A_HEADS // A_KV_HEADS
IDX_HEADS = 8
IDX_DIM = 64
DSA_TOPK = 256
B_HEADS = 6
B_KV_HEADS = 2
B_GROUP = B_HEADS // B_KV_HEADS
MOBA_BLOCK = 256
MOBA_TOPK = 3
POOL_WINDOWS = (2, 4, 8, 16)
POOL_GROUP_DIM = 128
POOL_DIM = len(POOL_WINDOWS) * POOL_GROUP_DIM
POOL_STATE = max(POOL_WINDOWS) - 1
A_DIM = A_HEADS * HEAD_DIM
A_KV_DIM = A_KV_HEADS * HEAD_DIM
B_DIM = B_HEADS * HEAD_DIM
B_KV_DIM = B_KV_HEADS * HEAD_DIM
MIX_DIM = A_DIM + B_DIM + POOL_DIM
MEM_TOKENS = 256
MEM_HEADS = 4
MEM_HEAD_DIM = 128
MEM_DIM = MEM_HEADS * MEM_HEAD_DIM
FFN_HIDDEN = -(-(8 * D_MODEL) // (3 * 256)) * 256
ALPHA = (2.0 * DEPTH) ** 0.25
LN_EPS = 1e-5
NEG_INF = -1e30
INT_MIN = -(2 ** 31)
SOFTMAX_C = (HEAD_DIM ** -0.5) * math.log2(math.e)

LANES = 128
SUBLANES = 8
VMEM_LIMIT_BYTES = 56 * 1024 * 1024

_SRC = dict(aq=(0, 768), ak=(768, 256), av=(1024, 256), iq=(1280, 512), ik=(1792, 64), iw=(1856, 8),
            bq=(1864, 768), bk=(2632, 256), bv=(2888, 256), pu=(3144, 512))
H_AQ, H_BQ, H_IQ, H_PU, H_AK, H_AV, H_BK, H_BV, H_IKW = 0, 768, 1536, 2048, 2560, 2816, 3072, 3328, 3584
H_DIM = 3840

DSA_TQ = 128
DSA_CK = 256
DSA_CA = 512
MOBA_TQ = 128
POOL_TT = 512
MEM_TR = 512
IDX_PAGES_PER_STEP = 64
KV_PAGES_PER_STEP = 32


def _cparams(sem):
    return pltpu.CompilerParams(dimension_semantics=sem, vmem_limit_bytes=VMEM_LIMIT_BYTES)


def _layer_norm_rows(y, g, b):
    mu = jnp.mean(y, axis=-1, keepdims=True)
    yc = y - mu
    var = jnp.mean(yc * yc, axis=-1, keepdims=True)
    return yc * lax.rsqrt(var + LN_EPS) * g + b


def _mm_kernel(x_ref, w_ref, o_ref, xb_ref):
    @pl.when(pl.program_id(1) == 0)
    def _():
        xb_ref[...] = x_ref[...].astype(BF16)

    o_ref[...] = jnp.dot(xb_ref[...], w_ref[...], preferred_element_type=F32)


def matmul(x, w, layer, tm, tn):
    m, k = x.shape
    n = w.shape[2]
    return pl.pallas_call(
        _mm_kernel,
        grid=(m // tm, n // tn),
        in_specs=[pl.BlockSpec((tm, k), lambda i, j: (i, 0)),
                  pl.BlockSpec((None, k, tn), lambda i, j: (layer, 0, j))],
        out_specs=pl.BlockSpec((tm, tn), lambda i, j: (i, j)),
        out_shape=jax.ShapeDtypeStruct((m, n), F32),
        scratch_shapes=[pltpu.VMEM((tm, k), BF16)],
        compiler_params=_cparams(("parallel", "arbitrary")),
        name="matmul",
    )(x, w)


def _mm_res_ln_kernel(a_ref, w_ref, x_ref, g_ref, b_ref, o_ref, acc_ref, *, nk):
    kk = pl.program_id(1)

    @pl.when(kk == 0)
    def _():
        acc_ref[...] = jnp.zeros_like(acc_ref)

    acc_ref[...] += jnp.dot(a_ref[...].astype(BF16), w_ref[...], preferred_element_type=F32)

    @pl.when(kk == nk - 1)
    def _():
        y = ALPHA * x_ref[...] + acc_ref[...]
        o_ref[...] = _layer_norm_rows(y, g_ref[...], b_ref[...])


def matmul_residual_ln(a, w, x, g, b, layer, tm, tk):
    m, k = a.shape
    n = w.shape[2]
    nk = k // tk
    return pl.pallas_call(
        functools.partial(_mm_res_ln_kernel, nk=nk),
        grid=(m // tm, nk),
        in_specs=[pl.BlockSpec((tm, tk), lambda i, j: (i, j)),
                  pl.BlockSpec((None, tk, n), lambda i, j: (layer, j, 0)),
                  pl.BlockSpec((tm, n), lambda i, j: (i, 0)),
                  pl.BlockSpec((None, 1, n), lambda i, j: (layer, 0, 0)),
                  pl.BlockSpec((None, 1, n), lambda i, j: (layer, 0, 0))],
        out_specs=pl.BlockSpec((tm, n), lambda i, j: (i, 0)),
        out_shape=jax.ShapeDtypeStruct((m, n), F32),
        scratch_shapes=[pltpu.VMEM((tm, n), F32)],
        compiler_params=_cparams(("parallel", "arbitrary")),
        name="matmul_residual_ln",
    )(a, w, x, g, b)


def _ffn_kernel(x_ref, w1_ref, w3_ref, w2_ref, g_ref, b_ref, o_ref, xb_ref, acc_ref, *, nf):
    j = pl.program_id(1)

    @pl.when(j == 0)
    def _():
        xb_ref[...] = x_ref[...].astype(BF16)
        acc_ref[...] = jnp.zeros_like(acc_ref)

    xb = xb_ref[...]
    h1 = jnp.dot(xb, w1_ref[...], preferred_element_type=F32)
    h3 = jnp.dot(xb, w3_ref[...], preferred_element_type=F32)
    h = (h1 / (1.0 + jnp.exp(-h1))) * h3
    acc_ref[...] += jnp.dot(h.astype(BF16), w2_ref[...], preferred_element_type=F32)

    @pl.when(j == nf - 1)
    def _():
        y = ALPHA * x_ref[...] + acc_ref[...]
        o_ref[...] = _layer_norm_rows(y, g_ref[...], b_ref[...])


def ffn_ln(x, w1, w3, w2, g, b, layer, tm, tf):
    m, d = x.shape
    f = w1.shape[2]
    nf = f // tf
    return pl.pallas_call(
        functools.partial(_ffn_kernel, nf=nf),
        grid=(m // tm, nf),
        in_specs=[pl.BlockSpec((tm, d), lambda i, j: (i, 0)),
                  pl.BlockSpec((None, d, tf), lambda i, j: (layer, 0, j)),
                  pl.BlockSpec((None, d, tf), lambda i, j: (layer, 0, j)),
                  pl.BlockSpec((None, tf, d), lambda i, j: (layer, j, 0)),
                  pl.BlockSpec((None, 1, d), lambda i, j: (layer, 0, 0)),
                  pl.BlockSpec((None, 1, d), lambda i, j: (layer, 0, 0))],
        out_specs=pl.BlockSpec((tm, d), lambda i, j: (i, 0)),
        out_shape=jax.ShapeDtypeStruct((m, d), F32),
        scratch_shapes=[pltpu.VMEM((tm, d), BF16), pltpu.VMEM((tm, d), F32)],
        compiler_params=_cparams(("parallel", "arbitrary")),
        name="ffn_ln",
    )(x, w1, w3, w2, g, b)


def _sortable_key(score):
    score = jnp.where(score == 0.0, 0.0, score)
    bits = pltpu.bitcast(score, I32)
    return jnp.where(bits < 0, bits ^ 0x7FFFFFFF, bits)


def _kth_largest(count_ge, shape, k_sel):
    c0 = count_ge(jnp.zeros(shape, I32))
    ok0 = c0 >= k_sel
    base = jnp.where(ok0, 0, INT_MIN).astype(I32)
    cnt = jnp.where(ok0, c0, jnp.iinfo(jnp.int32).max).astype(I32)

    def bit_body(i, carry):
        base, cnt = carry
        cand = base | jnp.left_shift(jnp.int32(1), 30 - i)
        c = count_ge(cand)
        ok = c >= k_sel
        return jnp.where(ok, cand, base), jnp.where(ok, c, cnt)

    return lax.fori_loop(0, 31, bit_body, (base, cnt))


def _flash_step_t(kc, vt, qt, bias, m, l, acc):
    lg = jnp.dot(kc, qt, preferred_element_type=F32) + bias
    m_new = jnp.maximum(m, jnp.max(lg, axis=0, keepdims=True))
    alpha = jnp.exp2(m - m_new)
    p = jnp.exp2(lg - m_new)
    l_new = alpha * l + jnp.sum(p, axis=0, keepdims=True)
    acc_new = alpha * acc + jnp.dot(vt, p.astype(BF16), preferred_element_type=F32)
    return m_new, l_new, acc_new


def _prep_kv_t(k_ref, v_ref, k_s, vt_s, rows, n_heads):
    kk = k_ref[rows, :]
    vv = v_ref[rows, :]
    for c in range(n_heads):
        cols = slice(c * HEAD_DIM, (c + 1) * HEAD_DIM)
        k_s[c, rows, :] = kk[:, cols].astype(BF16)
        vt_s[c, :, rows] = vv[:, cols].T.astype(BF16)
    return kk


def _stage_queries_t(q, qt_s, n_kv, group):
    q = q * SOFTMAX_C
    for c in range(n_kv):
        qt_s[c] = jnp.concatenate(
            [q[:, (c * group + g) * HEAD_DIM:(c * group + g + 1) * HEAD_DIM].T for g in range(group)],
            axis=1).astype(BF16)


def _head_q(qt_s, hh, tq):
    c, g = divmod(hh, A_GROUP)
    return qt_s[c, :, g * tq:(g + 1) * tq]


def _write_heads(acc_s, ls, o_ref, n_heads):
    for hh in range(n_heads):
        o_ref[:, hh * HEAD_DIM:(hh + 1) * HEAD_DIM] = (acc_s[hh] / ls[hh]).T


def _dsa_prompt_kernel(aq_ref, iq_ref, ikw_ref, ak_ref, av_ref, o_ref,
                       ik_s, k_s, vt_s, iqt_s, iwt_s, qt_s, keys_s, bias_s, acc_s, *, seq, k_sel):
    tq, ck = DSA_TQ, DSA_CK
    qi = pl.program_id(1)
    q0 = qi * tq
    prep_rows = min(512, seq)

    @pl.when(qi == 0)
    def _prep():
        lane = lax.broadcasted_iota(I32, (prep_rows, LANES), 1)

        def body(r, _):
            rows = pl.ds(pl.multiple_of(r * prep_rows, prep_rows), prep_rows)
            ik_s[rows, :] = jnp.where(lane < IDX_DIM, ikw_ref[rows, :], 0.0).astype(BF16)
            _prep_kv_t(ak_ref, av_ref, k_s, vt_s, rows, A_KV_HEADS)
            return 0

        lax.fori_loop(0, seq // prep_rows, body, 0)

    n_ck = (q0 + tq + ck - 1) // ck
    ikw_q = ikw_ref[pl.ds(pl.multiple_of(q0, tq), tq), :]
    iwt_s[...] = ikw_q.T[IDX_DIM:IDX_DIM + IDX_HEADS] * (IDX_HEADS ** -0.5)
    iqt = (iq_ref[...] * (IDX_DIM ** -0.5)).T
    zpad = jnp.zeros((LANES - IDX_DIM, 2 * tq), F32)
    for hp in range(IDX_HEADS // 2):
        pair = jnp.concatenate([iqt[(2 * hp) * IDX_DIM:(2 * hp + 1) * IDX_DIM],
                                iqt[(2 * hp + 1) * IDX_DIM:(2 * hp + 2) * IDX_DIM]], axis=1)
        iqt_s[hp] = jnp.concatenate([pair, zpad], axis=0).astype(BF16)
    _stage_queries_t(aq_ref[...], qt_s, A_KV_HEADS, A_GROUP)

    qpos = q0 + lax.broadcasted_iota(I32, (ck, tq), 1)
    krow = lax.broadcasted_iota(I32, (ck, tq), 0)

    def score_chunk(off):
        ikc = ik_s[pl.ds(off, ck), :]
        acc = jnp.zeros((ck, tq), F32)
        for hp in range(IDX_HEADS // 2):
            s = jnp.dot(ikc, iqt_s[hp], preferred_element_type=F32)
            acc = acc + (jnp.maximum(s[:, :tq], 0.0) * iwt_s[2 * hp:2 * hp + 1, :]
                         + jnp.maximum(s[:, tq:], 0.0) * iwt_s[2 * hp + 1:2 * hp + 2, :])
        keys_s[pl.ds(off, ck), :] = jnp.where(off + krow <= qpos, _sortable_key(acc), INT_MIN)

    def score_body(c, _):
        off = pl.multiple_of(c * (2 * ck), 2 * ck)
        score_chunk(off)
        score_chunk(off + ck)
        return 0

    lax.fori_loop(0, (n_ck + 1) // 2, score_body, 0)

    part = ck // 4

    def count_ge(cand):
        def body(c, acc):
            ind = jnp.where(keys_s[pl.ds(pl.multiple_of(c * ck, ck), ck), :] >= cand, 1, 0).astype(I32)
            return acc + ((ind[0:part] + ind[part:2 * part]) + (ind[2 * part:3 * part] + ind[3 * part:]))

        acc = lax.fori_loop(0, n_ck, body, jnp.zeros((part, tq), I32))
        return jnp.sum(acc, axis=0, keepdims=True)

    thr, cnt_thr = _kth_largest(count_ge, (1, tq), k_sel)
    has_tie = jnp.logical_and(cnt_thr > k_sel, thr > INT_MIN)
    any_tie = jnp.max(has_tie.astype(I32)) > 0
    thr_c = jnp.maximum(thr, INT_MIN + 1)

    @pl.when(jnp.logical_not(any_tie))
    def _fast():
        def body(c, _):
            rows = pl.ds(pl.multiple_of(c * ck, ck), ck)
            bias_s[rows, :] = jnp.where(keys_s[rows, :] >= thr_c, 0.0, NEG_INF)
            return 0

        lax.fori_loop(0, n_ck, body, 0)

    @pl.when(any_tie)
    def _ties():
        need = (k_sel - count_ge(thr_c + 1)).astype(F32)
        tri = jnp.where(lax.broadcasted_iota(I32, (LANES, LANES), 0)
                        >= lax.broadcasted_iota(I32, (LANES, LANES), 1), 1.0, 0.0).astype(BF16)

        def body(c, run):
            rows = pl.ds(pl.multiple_of(c * LANES, LANES), LANES)
            kchunk = keys_s[rows, :]
            eq = kchunk == thr_c
            eqf = jnp.where(eq, 1.0, 0.0)
            pre = jnp.dot(tri, eqf.astype(BF16), preferred_element_type=F32) + run
            sel = jnp.logical_or(kchunk > thr_c, jnp.logical_and(eq, pre <= need))
            bias_s[rows, :] = jnp.where(sel, 0.0, NEG_INF)
            return run + jnp.sum(eqf, axis=0, keepdims=True)

        lax.fori_loop(0, n_ck * (ck // LANES), body, jnp.zeros((1, tq), F32))

    ca = DSA_CA
    n_ca = (q0 + tq + ca - 1) // ca
    for extra in range(1, ca // ck):
        @pl.when(n_ck + extra <= n_ca * (ca // ck))
        def _mask_tail(extra=extra):
            bias_s[pl.ds(pl.multiple_of((n_ck + extra - 1) * ck, ck), ck), :] = jnp.full((ck, tq), NEG_INF, F32)

    acc_s[...] = jnp.zeros_like(acc_s)

    def att_body(j, carry):
        ms, ls = carry
        rows = pl.ds(pl.multiple_of(j * ca, ca), ca)
        bias = bias_s[rows, :]
        new_m, new_l = [], []
        for hh in range(A_HEADS):
            c = hh // A_GROUP
            m, l, acc = _flash_step_t(k_s[c, rows, :], vt_s[c, :, rows], _head_q(qt_s, hh, tq), bias,
                                      ms[hh], ls[hh], acc_s[hh])
            acc_s[hh] = acc
            new_m.append(m)
            new_l.append(l)
        return tuple(new_m), tuple(new_l)

    init = (tuple(jnp.full((1, tq), NEG_INF, F32) for _ in range(A_HEADS)),
            tuple(jnp.zeros((1, tq), F32) for _ in range(A_HEADS)))
    _, ls = lax.fori_loop(0, n_ca, att_body, init)
    _write_heads(acc_s, ls, o_ref, A_HEADS)


def dsa_prompt(h, batch, seq, n_rows):
    tq = DSA_TQ
    nq = seq // tq
    k_sel = min(DSA_TOPK, seq // 4)
    kern = functools.partial(_dsa_prompt_kernel, seq=seq, k_sel=k_sel)
    return pl.pallas_call(
        kern,
        grid=(batch, nq),
        in_specs=[pl.BlockSpec((tq, A_DIM), lambda b, i: (b * nq + i, H_AQ // A_DIM)),
                  pl.BlockSpec((tq, IDX_HEADS * IDX_DIM), lambda b, i: (b * nq + i, H_IQ // (IDX_HEADS * IDX_DIM))),
                  pl.BlockSpec((seq, LANES), lambda b, i: (b, H_IKW // LANES)),
                  pl.BlockSpec((seq, A_KV_DIM), lambda b, i: (b, H_AK // A_KV_DIM)),
                  pl.BlockSpec((seq, A_KV_DIM), lambda b, i: (b, H_AV // A_KV_DIM))],
        out_specs=pl.BlockSpec((tq, A_DIM), lambda b, i: (b * nq + i, 0)),
        out_shape=jax.ShapeDtypeStruct((n_rows, MIX_DIM), F32),
        scratch_shapes=[pltpu.VMEM((seq, LANES), BF16),
                        pltpu.VMEM((A_KV_HEADS, seq, HEAD_DIM), BF16),
                        pltpu.VMEM((A_KV_HEADS, HEAD_DIM, seq), BF16),
                        pltpu.VMEM((IDX_HEADS // 2, LANES, 2 * tq), BF16),
                        pltpu.VMEM((IDX_HEADS, tq), F32),
                        pltpu.VMEM((A_KV_HEADS, HEAD_DIM, A_GROUP * tq), BF16),
                        pltpu.VMEM((seq, tq), I32),
                        pltpu.VMEM((seq, tq), F32),
                        pltpu.VMEM((A_HEADS, HEAD_DIM, tq), F32)],
        compiler_params=_cparams(("arbitrary", "arbitrary")),
        name="dsa_prompt",
    )(h, h, h, h, h)


def _rank_select(gate, n_past_blocks, axis):
    idx = lax.broadcasted_iota(I32, gate.shape, axis)

    def body(m, rank):
        gm = jnp.sum(jnp.where(idx == m, gate, 0.0), axis=axis, keepdims=True)
        beats = jnp.logical_or(gm > gate, jnp.logical_and(gm == gate, m < idx))
        return rank + jnp.where(beats, 1, 0).astype(I32)

    rank = lax.fori_loop(0, n_past_blocks, body, jnp.zeros(gate.shape, I32))
    sel = jnp.logical_and(idx < n_past_blocks, rank < MOBA_TOPK)
    return jnp.where(sel, 0.0, NEG_INF)


def _moba_prompt_kernel(bq_ref, bk_ref, bv_ref, mix_ref, o_ref, k_s, vt_s, km_s, kmb_s, qt_s, sb_s, acc_s,
                        *, seq, nbp):
    del mix_ref
    tq, blk = MOBA_TQ, MOBA_BLOCK
    qi = pl.program_id(1)
    q0 = qi * tq
    nb = seq // blk

    @pl.when(qi == 0)
    def _prep():
        km_s[...] = jnp.zeros_like(km_s)

        def body(n, _):
            rows = pl.ds(pl.multiple_of(n * blk, blk), blk)
            kk = _prep_kv_t(bk_ref, bv_ref, k_s, vt_s, rows, B_KV_HEADS)
            km_s[pl.ds(n, 1), :] = jnp.mean(kk, axis=0, keepdims=True)
            return 0

        lax.fori_loop(0, nb, body, 0)
        for c in range(B_KV_HEADS):
            kmb_s[c] = km_s[:, c * HEAD_DIM:(c + 1) * HEAD_DIM].astype(BF16)

    own = q0 // blk
    own_rows = pl.ds(pl.multiple_of(own * blk, blk), blk)
    bq = bq_ref[...]
    _stage_queries_t(bq, qt_s, B_KV_HEADS, B_GROUP)
    gates = []
    for c in range(B_KV_HEADS):
        qg = jnp.concatenate(
            [bq[:, (c * B_GROUP + g) * HEAD_DIM:(c * B_GROUP + g + 1) * HEAD_DIM].T for g in range(B_GROUP)],
            axis=1).astype(BF16)
        gates.append(jnp.dot(kmb_s[c], qg, preferred_element_type=F32)[:nbp])
    sel_bias = _rank_select(jnp.concatenate(gates, axis=1), own, axis=0)
    for hh in range(B_HEADS):
        sb_s[hh] = sel_bias[:, hh * tq:(hh + 1) * tq]

    qpos = q0 + lax.broadcasted_iota(I32, (blk, tq), 1)
    kpos = own * blk + lax.broadcasted_iota(I32, (blk, tq), 0)
    causal_bias = jnp.where(kpos <= qpos, 0.0, NEG_INF)
    ms, ls = [], []
    for hh in range(B_HEADS):
        c = hh // B_GROUP
        m, l, acc = _flash_step_t(k_s[c, own_rows, :], vt_s[c, :, own_rows], _head_q(qt_s, hh, tq), causal_bias,
                                  jnp.full((1, tq), NEG_INF, F32), jnp.zeros((1, tq), F32),
                                  jnp.zeros((HEAD_DIM, tq), F32))
        acc_s[hh] = acc
        ms.append(m)
        ls.append(l)

    odd = own % 2

    def pair_body(p, carry):
        ms, ls = carry
        b0 = jnp.maximum(2 * p - odd, 0)
        rows = pl.ds(pl.multiple_of(b0 * blk, blk), 2 * blk)
        skip_second = jnp.where(jnp.logical_and(p == 0, odd == 1), NEG_INF, 0.0)
        new_m, new_l = [], []
        for hh in range(B_HEADS):
            c = hh // B_GROUP
            bias = jnp.concatenate(
                [jnp.broadcast_to(sb_s[hh, pl.ds(b0, 1), :], (blk, tq)),
                 jnp.broadcast_to(sb_s[hh, pl.ds(b0 + 1, 1), :] + skip_second, (blk, tq))], axis=0)
            m, l, acc = _flash_step_t(k_s[c, rows, :], vt_s[c, :, rows], _head_q(qt_s, hh, tq), bias,
                                      ms[hh], ls[hh], acc_s[hh])
            acc_s[hh] = acc
            new_m.append(m)
            new_l.append(l)
        return tuple(new_m), tuple(new_l)

    _, ls = lax.fori_loop(0, (own + 1) // 2, pair_body, (tuple(ms), tuple(ls)))
    _write_heads(acc_s, ls, o_ref, B_HEADS)


def moba_prompt(h, mix, batch, seq):
    tq = MOBA_TQ
    nq = seq // tq
    nbp = -(-(seq // MOBA_BLOCK) // SUBLANES) * SUBLANES
    return pl.pallas_call(
        functools.partial(_moba_prompt_kernel, seq=seq, nbp=nbp),
        grid=(batch, nq),
        in_specs=[pl.BlockSpec((tq, B_DIM), lambda b, i: (b * nq + i, H_BQ // B_DIM)),
                  pl.BlockSpec((seq, B_KV_DIM), lambda b, i: (b, H_BK // B_KV_DIM)),
                  pl.BlockSpec((seq, B_KV_DIM), lambda b, i: (b, H_BV // B_KV_DIM)),
                  pl.BlockSpec(memory_space=pl.ANY)],
        out_specs=pl.BlockSpec((tq, B_DIM), lambda b, i: (b * nq + i, A_DIM // B_DIM)),
        out_shape=jax.ShapeDtypeStruct(mix.shape, mix.dtype),
        input_output_aliases={3: 0},
        scratch_shapes=[pltpu.VMEM((B_KV_HEADS, seq, HEAD_DIM), BF16),
                        pltpu.VMEM((B_KV_HEADS, HEAD_DIM, seq), BF16),
                        pltpu.VMEM((LANES, B_KV_DIM), F32),
                        pltpu.VMEM((B_KV_HEADS, LANES, HEAD_DIM), BF16),
                        pltpu.VMEM((B_KV_HEADS, HEAD_DIM, B_GROUP * tq), BF16),
                        pltpu.VMEM((B_HEADS, nbp, tq), F32),
                        pltpu.VMEM((B_HEADS, HEAD_DIM, tq), F32)],
        compiler_params=_cparams(("arbitrary", "arbitrary")),
        name="moba_prompt",
    )(h, h, h, mix)


POOL_HALO = 16


def _pool_groups(u, window_sum, cnt_of, w_ref, sc_ref, o_ref):
    for g, w in enumerate(POOL_WINDOWS):
        cols = slice(g * POOL_GROUP_DIM, (g + 1) * POOL_GROUP_DIM)
        ug = u[:, cols]
        d = window_sum(g, w, ug) / cnt_of(w) - ug
        y = jnp.dot(d.astype(BF16), w_ref[g], preferred_element_type=F32)
        o_ref[:, cols] = y * sc_ref[:, cols]


def _pool_prompt_kernel(u_ref, halo_ref, w_ref, sc_ref, mix_ref, o_ref, ext_s):
    del mix_ref
    tt = POOL_TT
    i = pl.program_id(1)
    u = u_ref[...]
    ext_s[0:POOL_HALO, :] = jnp.where(i == 0, 0.0, halo_ref[...])
    ext_s[POOL_HALO:POOL_HALO + tt, :] = u
    t = i * tt + lax.broadcasted_iota(I32, (tt, 1), 0)

    def window_sum(g, w, ug):
        s = ug
        for k in range(1, w):
            s = s + ext_s[pl.ds(POOL_HALO - k, tt), g * POOL_GROUP_DIM:(g + 1) * POOL_GROUP_DIM]
        return s

    _pool_groups(u, window_sum, lambda w: jnp.minimum(w, t + 1).astype(F32), w_ref, sc_ref, o_ref)


def pool_prompt(h, mix, pool_w, pool_scale, layer, batch, seq):
    tt = POOL_TT
    nt = seq // tt
    halo_blocks = tt // POOL_HALO
    return pl.pallas_call(
        _pool_prompt_kernel,
        grid=(batch, nt),
        in_specs=[pl.BlockSpec((tt, POOL_DIM), lambda b, i: (b * nt + i, H_PU // POOL_DIM)),
                  pl.BlockSpec((POOL_HALO, POOL_DIM),
                               lambda b, i: (jnp.maximum((b * nt + i) * halo_blocks - 1, 0), H_PU // POOL_DIM)),
                  pl.BlockSpec((None, len(POOL_WINDOWS), POOL_GROUP_DIM, POOL_GROUP_DIM),
                               lambda b, i: (layer, 0, 0, 0)),
                  pl.BlockSpec((None, 1, POOL_DIM), lambda b, i: (layer, 0, 0)),
                  pl.BlockSpec(memory_space=pl.ANY)],
        out_specs=pl.BlockSpec((tt, POOL_DIM), lambda b, i: (b * nt + i, (A_DIM + B_DIM) // POOL_DIM)),
        out_shape=jax.ShapeDtypeStruct(mix.shape, mix.dtype),
        input_output_aliases={4: 0},
        scratch_shapes=[pltpu.VMEM((POOL_HALO + tt, POOL_DIM), F32)],
        compiler_params=_cparams(("arbitrary", "arbitrary")),
        name="pool_prompt",
    )(h, h, pool_w, pool_scale, mix)


def _pool_sample_kernel(u_ref, st_ref, w_ref, sc_ref, mix_ref, o_ref, ext_s, *, db, ts, n_past):
    del mix_ref
    u = u_ref[...]
    for b in range(db):
        ext_s[b, POOL_HALO - POOL_STATE:POOL_HALO, :] = st_ref[b]
        ext_s[b, POOL_HALO:POOL_HALO + ts, :] = u[b * ts:(b + 1) * ts]
    t = n_past + lax.broadcasted_iota(I32, (db * ts, 1), 0) % ts

    def window_sum(g, w, ug):
        parts = []
        for b in range(db):
            s = ug[b * ts:(b + 1) * ts]
            for k in range(1, w):
                s = s + ext_s[b, pl.ds(POOL_HALO - k, ts), g * POOL_GROUP_DIM:(g + 1) * POOL_GROUP_DIM]
            parts.append(s)
        return jnp.concatenate(parts, axis=0)

    _pool_groups(u, window_sum, lambda w: jnp.minimum(w, t + 1).astype(F32), w_ref, sc_ref, o_ref)


def pool_sample(h, mix, state_pool, pool_w, pool_scale, layer, n_prompt, db, ts, n_past):
    rows = db * ts
    return pl.pallas_call(
        functools.partial(_pool_sample_kernel, db=db, ts=ts, n_past=n_past),
        grid=(1,),
        in_specs=[pl.BlockSpec((rows, POOL_DIM), lambda i: (n_prompt // rows, H_PU // POOL_DIM)),
                  pl.BlockSpec((None, db, POOL_STATE, POOL_DIM), lambda i: (layer, 0, 0, 0)),
                  pl.BlockSpec((None, len(POOL_WINDOWS), POOL_GROUP_DIM, POOL_GROUP_DIM),
                               lambda i: (layer, 0, 0, 0)),
                  pl.BlockSpec((None, 1, POOL_DIM), lambda i: (layer, 0, 0)),
                  pl.BlockSpec(memory_space=pl.ANY)],
        out_specs=pl.BlockSpec((rows, POOL_DIM), lambda i: (n_prompt // rows, (A_DIM + B_DIM) // POOL_DIM)),
        out_shape=jax.ShapeDtypeStruct(mix.shape, mix.dtype),
        input_output_aliases={4: 0},
        scratch_shapes=[pltpu.VMEM((db, POOL_HALO + ts, POOL_DIM), F32)],
        compiler_params=_cparams(("arbitrary",)),
        name="pool_sample",
    )(h, state_pool, pool_w, pool_scale, mix)


def _memattn_heads(q_ref, head_k, head_v, o_ref):
    scale = MEM_HEAD_DIM ** -0.5
    for hh in range(MEM_HEADS):
        cols = slice(hh * MEM_HEAD_DIM, (hh + 1) * MEM_HEAD_DIM)
        qh = q_ref[:, cols].astype(BF16)
        lg = lax.dot_general(qh, head_k(hh).astype(BF16), (((1,), (1,)), ((), ())),
                             preferred_element_type=F32) * scale
        m = jnp.max(lg, axis=1, keepdims=True)
        p = jnp.exp(lg - m)
        l = jnp.sum(p, axis=1, keepdims=True)
        o_ref[:, cols] = jnp.dot(p.astype(BF16), head_v(hh).astype(BF16), preferred_element_type=F32) / l


def _memattn_prompt_kernel(q_ref, mk_ref, mv_ref, o_ref):
    _memattn_heads(q_ref, lambda hh: mk_ref[:, hh * MEM_HEAD_DIM:(hh + 1) * MEM_HEAD_DIM],
                   lambda hh: mv_ref[:, hh * MEM_HEAD_DIM:(hh + 1) * MEM_HEAD_DIM], o_ref)


def _memattn_sample_kernel(q_ref, mk_ref, mv_ref, o_in_ref, o_ref):
    del o_in_ref
    _memattn_heads(q_ref, lambda hh: _head_rows(mk_ref, hh, MEM_TOKENS, MEM_HEADS),
                   lambda hh: _head_rows(mv_ref, hh, MEM_TOKENS, MEM_HEADS), o_ref)


def memattn_prompt(q, mkv, batch, seq):
    tr = min(MEM_TR, seq)
    nt = seq // tr
    return pl.pallas_call(
        _memattn_prompt_kernel,
        grid=(batch, nt),
        in_specs=[pl.BlockSpec((tr, MEM_DIM), lambda b, i: (b * nt + i, 0)),
                  pl.BlockSpec((MEM_TOKENS, MEM_DIM), lambda b, i: (b, 0)),
                  pl.BlockSpec((MEM_TOKENS, MEM_DIM), lambda b, i: (b, 1))],
        out_specs=pl.BlockSpec((tr, MEM_DIM), lambda b, i: (b * nt + i, 0)),
        out_shape=jax.ShapeDtypeStruct(q.shape, F32),
        compiler_params=_cparams(("arbitrary", "arbitrary")),
        name="memattn_prompt",
    )(q, mkv, mkv)


def memattn_sample(q, cache_k, cache_v, o, layer, n_prompt, db, ts):
    cache_spec = pl.BlockSpec((None, None, MEM_TOKENS, MEM_HEADS, MEM_HEAD_DIM), lambda b: (layer, b, 0, 0, 0))
    return pl.pallas_call(
        _memattn_sample_kernel,
        grid=(db,),
        in_specs=[pl.BlockSpec((ts, MEM_DIM), lambda b: (n_prompt // ts + b, 0)),
                  cache_spec, cache_spec,
                  pl.BlockSpec(memory_space=pl.ANY)],
        out_specs=pl.BlockSpec((ts, MEM_DIM), lambda b: (n_prompt // ts + b, 0)),
        out_shape=jax.ShapeDtypeStruct(o.shape, o.dtype),
        input_output_aliases={3: 0},
        compiler_params=_cparams(("arbitrary",)),
        name="memattn_sample",
    )(q, cache_k, cache_v, o)


def _page_specs(npg, page_shape, layer, step_of):
    zeros = (0,) * len(page_shape)

    def spec(i):
        return pl.BlockSpec((None, None) + page_shape,
                            lambda b, j, pt: (layer, pt[b, step_of(j) * npg + i]) + zeros)
    return [spec(i) for i in range(npg)]


def _pad_rows(x, rows):
    return jnp.concatenate([x, jnp.zeros((rows - x.shape[0], x.shape[1]), x.dtype)], axis=0)


def _head_rows(ref, c, rows, n_heads):
    return ref.reshape(rows * n_heads, HEAD_DIM)[pl.ds(c, rows, stride=n_heads), :]


def _gather_head(pages, c, n_heads):
    return jnp.concatenate([_head_rows(p, c, PAGE_SIZE, n_heads) for p in pages], axis=0)


def _dsa_sample_select_kernel(pt_ref, iq_ref, ikw_ref, *refs, npg, n_steps, n_past, ts, k_sel):
    del pt_ref
    pages = refs[:npg]
    bias_ref, keys_s = refs[npg], refs[npg + 1]
    j = pl.program_id(1)
    ck = npg * PAGE_SIZE
    lp = n_past + LANES
    iq = iq_ref[...] * (IDX_DIM ** -0.5)
    iq_st = jnp.concatenate([iq[:, h * IDX_DIM:(h + 1) * IDX_DIM] for h in range(IDX_HEADS)],
                            axis=0).astype(BF16)
    ikw = ikw_ref[...]
    iw = ikw[:, IDX_DIM:IDX_DIM + IDX_HEADS] * (IDX_HEADS ** -0.5)

    def scores(ik_t):
        s = jnp.dot(iq_st, ik_t.astype(BF16), preferred_element_type=F32)
        acc = jnp.zeros((ts, ik_t.shape[1]), F32)
        for h in range(IDX_HEADS):
            acc = acc + jnp.maximum(s[h * ts:(h + 1) * ts], 0.0) * iw[:, h:h + 1]
        return acc

    ik_past = jnp.concatenate([pages[i][...] for i in range(npg)], axis=1)
    keys_s[:, pl.ds(pl.multiple_of(j * ck, ck), ck)] = _sortable_key(scores(ik_past))

    @pl.when(j == n_steps - 1)
    def _select():
        sc_new = scores(_pad_rows(ikw, LANES).T[:IDX_DIM])
        causal = lax.broadcasted_iota(I32, (ts, LANES), 1) <= lax.broadcasted_iota(I32, (ts, LANES), 0)
        keys_s[:, n_past:lp] = jnp.where(causal, _sortable_key(sc_new), INT_MIN)

        def count_ge(cand):
            return jnp.sum(jnp.where(keys_s[...] >= cand, 1, 0).astype(I32), axis=1, keepdims=True)

        thr, cnt_thr = _kth_largest(count_ge, (ts, 1), k_sel)
        has_tie = jnp.logical_and(cnt_thr > k_sel, thr > INT_MIN)
        any_tie = jnp.max(has_tie.astype(I32)) > 0
        thr_c = jnp.maximum(thr, INT_MIN + 1)

        @pl.when(jnp.logical_not(any_tie))
        def _fast():
            bias_ref[...] = jnp.where(keys_s[...] >= thr_c, 0.0, NEG_INF)

        @pl.when(any_tie)
        def _ties():
            need = (k_sel - count_ge(thr_c + 1)).astype(F32)
            tri = jnp.where(lax.broadcasted_iota(I32, (LANES, LANES), 0)
                            <= lax.broadcasted_iota(I32, (LANES, LANES), 1), 1.0, 0.0).astype(BF16)

            def body(c, run):
                cols = pl.ds(pl.multiple_of(c * LANES, LANES), LANES)
                kchunk = keys_s[:, cols]
                eq = kchunk == thr_c
                eqf = jnp.where(eq, 1.0, 0.0)
                pre = jnp.dot(eqf.astype(BF16), tri, preferred_element_type=F32) + run
                sel = jnp.logical_or(kchunk > thr_c, jnp.logical_and(eq, pre <= need))
                bias_ref[:, cols] = jnp.where(sel, 0.0, NEG_INF)
                return run + jnp.sum(eqf, axis=1, keepdims=True)

            lax.fori_loop(0, lp // LANES, body, jnp.zeros((ts, 1), F32))


def dsa_sample_select(page_table, h, cache_idx, layer, n_prompt, db, ts):
    n_pages = page_table.shape[1]
    n_past = n_pages * PAGE_SIZE
    npg = min(IDX_PAGES_PER_STEP, n_pages)
    n_steps = n_pages // npg
    k_sel = min(DSA_TOPK, (n_past + ts) // 4)
    lp = n_past + LANES
    kern = functools.partial(_dsa_sample_select_kernel, npg=npg, n_steps=n_steps, n_past=n_past, ts=ts, k_sel=k_sel)
    row_blk = n_prompt // ts
    grid_spec = pltpu.PrefetchScalarGridSpec(
        num_scalar_prefetch=1,
        grid=(db, n_steps),
        in_specs=[pl.BlockSpec((ts, IDX_HEADS * IDX_DIM), lambda b, j, pt: (row_blk + b, H_IQ // (IDX_HEADS * IDX_DIM))),
                  pl.BlockSpec((ts, LANES), lambda b, j, pt: (row_blk + b, H_IKW // LANES))]
        + _page_specs(npg, (IDX_DIM, PAGE_SIZE), layer, lambda j: j),
        out_specs=pl.BlockSpec((None, ts, lp), lambda b, j, pt: (b, 0, 0)),
        scratch_shapes=[pltpu.VMEM((ts, lp), I32)],
    )
    return pl.pallas_call(
        kern,
        grid_spec=grid_spec,
        out_shape=jax.ShapeDtypeStruct((db, ts, lp), F32),
        compiler_params=_cparams(("arbitrary", "arbitrary")),
        name="dsa_sample_select",
    )(page_table, h, h, *([cache_idx] * npg))


def _tile_bias_t(bs, group, ts):
    full = jnp.concatenate([bs] * group + [jnp.zeros((LANES - group * ts, bs.shape[1]), bs.dtype)], axis=0)
    return full.T


def _stack_heads_t(q, c, group, ts):
    qc = jnp.concatenate(
        [q[:, (c * group + g) * HEAD_DIM:(c * group + g + 1) * HEAD_DIM] for g in range(group)]
        + [jnp.zeros((LANES - group * ts, HEAD_DIM), q.dtype)], axis=0)
    return qc.T.astype(BF16)


def _dsa_sample_attend_kernel(pt_ref, q_ref, kn_ref, vn_ref, bias_ref, *refs, npg, n_steps, n_past, ts):
    del pt_ref
    kp, vp = refs[:npg], refs[npg:2 * npg]
    o_ref = refs[2 * npg + 1]
    qT_s, m_s, l_s, accT_s = refs[2 * npg + 2:]
    j = pl.program_id(1)
    ck = npg * PAGE_SIZE
    scale = HEAD_DIM ** -0.5

    @pl.when(j == 0)
    def _init():
        q = q_ref[...]
        for c in range(A_KV_HEADS):
            qT_s[c] = _stack_heads_t(q, c, A_GROUP, ts)
        m_s[...] = jnp.full(m_s.shape, NEG_INF, F32)
        l_s[...] = jnp.zeros_like(l_s)
        accT_s[...] = jnp.zeros_like(accT_s)

    def update(c, kc, vc, bias_t):
        lg = jnp.dot(kc.astype(BF16), qT_s[c], preferred_element_type=F32) * scale + bias_t
        m_old = m_s[c]
        m_new = jnp.maximum(m_old, jnp.max(lg, axis=0, keepdims=True))
        alpha = jnp.exp(m_old - m_new)
        p = jnp.exp(lg - m_new)
        l_s[c] = alpha * l_s[c] + jnp.sum(p, axis=0, keepdims=True)
        accT_s[c] = alpha * accT_s[c] + jnp.dot(vc.T.astype(BF16), p.astype(BF16), preferred_element_type=F32)
        m_s[c] = m_new

    bias_t = _tile_bias_t(bias_ref[:, pl.ds(pl.multiple_of(j * ck, ck), ck)], A_GROUP, ts)
    for c in range(A_KV_HEADS):
        update(c, _gather_head(kp, c, A_KV_HEADS), _gather_head(vp, c, A_KV_HEADS), bias_t)

    @pl.when(j == n_steps - 1)
    def _finish():
        bias_n = _tile_bias_t(bias_ref[:, n_past:n_past + LANES], A_GROUP, ts)
        for c in range(A_KV_HEADS):
            cols = slice(c * HEAD_DIM, (c + 1) * HEAD_DIM)
            update(c, _pad_rows(kn_ref[:, cols], LANES), _pad_rows(vn_ref[:, cols], LANES), bias_n)
            out = (accT_s[c] / l_s[c]).T
            for g in range(A_GROUP):
                hcol = (c * A_GROUP + g) * HEAD_DIM
                o_ref[:, hcol:hcol + HEAD_DIM] = out[g * ts:(g + 1) * ts]


def dsa_sample_attend(page_table, h, bias, cache_k, cache_v, mix, layer, n_prompt, db, ts):
    n_pages = page_table.shape[1]
    n_past = n_pages * PAGE_SIZE
    npg = min(KV_PAGES_PER_STEP, n_pages)
    n_steps = n_pages // npg
    lp = n_past + LANES
    kern = functools.partial(_dsa_sample_attend_kernel, npg=npg, n_steps=n_steps, n_past=n_past, ts=ts)
    row_blk = n_prompt // ts
    kv_tail = (PAGE_SIZE, A_KV_HEADS, HEAD_DIM)
    grid_spec = pltpu.PrefetchScalarGridSpec(
        num_scalar_prefetch=1,
        grid=(db, n_steps),
        in_specs=[pl.BlockSpec((ts, A_DIM), lambda b, j, pt: (row_blk + b, H_AQ // A_DIM)),
                  pl.BlockSpec((ts, A_KV_DIM), lambda b, j, pt: (row_blk + b, H_AK // A_KV_DIM)),
                  pl.BlockSpec((ts, A_KV_DIM), lambda b, j, pt: (row_blk + b, H_AV // A_KV_DIM)),
                  pl.BlockSpec((None, ts, lp), lambda b, j, pt: (b, 0, 0))]
        + _page_specs(npg, kv_tail, layer, lambda j: j)
        + _page_specs(npg, kv_tail, layer, lambda j: j)
        + [pl.BlockSpec(memory_space=pl.ANY)],
        out_specs=pl.BlockSpec((ts, A_DIM), lambda b, j, pt: (row_blk + b, 0)),
        scratch_shapes=[pltpu.VMEM((A_KV_HEADS, HEAD_DIM, LANES), BF16),
                        pltpu.VMEM((A_KV_HEADS, 1, LANES), F32),
                        pltpu.VMEM((A_KV_HEADS, 1, LANES), F32),
                        pltpu.VMEM((A_KV_HEADS, HEAD_DIM, LANES), F32)],
    )
    return pl.pallas_call(
        kern,
        grid_spec=grid_spec,
        out_shape=jax.ShapeDtypeStruct(mix.shape, mix.dtype),
        input_output_aliases={5 + 2 * npg: 0},
        compiler_params=_cparams(("arbitrary", "arbitrary")),
        name="dsa_sample_attend",
    )(page_table, h, h, h, bias, *([cache_k] * npg), *([cache_v] * npg), mix)


def _moba_sample_kernel(pt_ref, q_ref, kn_ref, vn_ref, *refs, npg, n_kv_steps, ts):
    del pt_ref
    kp, vp = refs[:npg], refs[npg:2 * npg]
    o_ref = refs[2 * npg + 1]
    qT_s, p_s, km_s, sb_s, pn_s, l_s, accT_s = refs[2 * npg + 2:]
    j = pl.program_id(1)
    ck = npg * PAGE_SIZE
    blk = MOBA_BLOCK
    bps = ck // blk
    nbk = n_kv_steps * bps
    scale = HEAD_DIM ** -0.5

    @pl.when(j == 0)
    def _init():
        q = q_ref[...]
        for c in range(B_KV_HEADS):
            qT_s[c] = _stack_heads_t(q, c, B_GROUP, ts)
        km_s[...] = jnp.zeros_like(km_s)
        accT_s[...] = jnp.zeros_like(accT_s)

    @pl.when(j < n_kv_steps)
    def _k_phase():
        for c in range(B_KV_HEADS):
            kc = _gather_head(kp, c, B_KV_HEADS)
            means = jnp.concatenate(
                [jnp.mean(kc[n * blk:(n + 1) * blk], axis=0, keepdims=True) for n in range(bps)], axis=0)
            km_s[c, pl.ds(pl.multiple_of(j * bps, bps), bps), :] = means
            p_s[c, pl.ds(pl.multiple_of(j * ck, ck), ck), :] = (
                jnp.dot(kc.astype(BF16), qT_s[c], preferred_element_type=F32) * scale)

    @pl.when(j == n_kv_steps - 1)
    def _select_softmax():
        key_i = lax.broadcasted_iota(I32, (LANES, LANES), 0)
        q_of_lane = lax.broadcasted_iota(I32, (LANES, LANES), 1) % ts
        new_ok = jnp.logical_and(key_i <= q_of_lane, key_i < ts)
        for c in range(B_KV_HEADS):
            cols = slice(c * HEAD_DIM, (c + 1) * HEAD_DIM)
            gate_t = jnp.dot(km_s[c].astype(BF16), qT_s[c], preferred_element_type=F32)
            sb_s[c] = _rank_select(gate_t, nbk, axis=0)
            lg_new = jnp.dot(_pad_rows(kn_ref[:, cols], LANES).astype(BF16), qT_s[c],
                             preferred_element_type=F32) * scale
            lg_new = jnp.where(new_ok, lg_new, NEG_INF)

            def blk_logits(n, c=c):
                rows = pl.ds(pl.multiple_of(n * blk, blk), blk)
                return rows, p_s[c, rows, :] + sb_s[c, pl.ds(n, 1), :]

            def max_body(n, m):
                _, lg = blk_logits(n)
                return jnp.maximum(m, jnp.max(lg, axis=0, keepdims=True))

            m = lax.fori_loop(0, nbk, max_body, jnp.max(lg_new, axis=0, keepdims=True))

            def exp_body(n, l, c=c, m=m):
                rows, lg = blk_logits(n)
                p = jnp.exp(lg - m)
                p_s[c, rows, :] = p
                return l + jnp.sum(p, axis=0, keepdims=True)

            p_new = jnp.exp(lg_new - m)
            pn_s[c] = p_new
            l_s[c] = lax.fori_loop(0, nbk, exp_body, jnp.sum(p_new, axis=0, keepdims=True))

    @pl.when(j >= n_kv_steps)
    def _v_phase():
        jj = j - n_kv_steps
        for c in range(B_KV_HEADS):
            vc = _gather_head(vp, c, B_KV_HEADS)
            p = p_s[c, pl.ds(pl.multiple_of(jj * ck, ck), ck), :]
            accT_s[c] += jnp.dot(vc.T.astype(BF16), p.astype(BF16), preferred_element_type=F32)

    @pl.when(j == 2 * n_kv_steps - 1)
    def _finish():
        for c in range(B_KV_HEADS):
            cols = slice(c * HEAD_DIM, (c + 1) * HEAD_DIM)
            vn = _pad_rows(vn_ref[:, cols], LANES)
            acc = accT_s[c] + jnp.dot(vn.T.astype(BF16), pn_s[c].astype(BF16), preferred_element_type=F32)
            out = (acc / l_s[c]).T
            for g in range(B_GROUP):
                hcol = (c * B_GROUP + g) * HEAD_DIM
                o_ref[:, hcol:hcol + HEAD_DIM] = out[g * ts:(g + 1) * ts]


def moba_sample(page_table, h, cache_k, cache_v, mix, layer, n_prompt, db, ts):
    n_pages = page_table.shape[1]
    n_past = n_pages * PAGE_SIZE
    npg = min(KV_PAGES_PER_STEP, n_pages)
    n_kv = n_pages // npg
    kern = functools.partial(_moba_sample_kernel, npg=npg, n_kv_steps=n_kv, ts=ts)
    row_blk = n_prompt // ts
    kv_tail = (PAGE_SIZE, B_KV_HEADS, HEAD_DIM)
    grid_spec = pltpu.PrefetchScalarGridSpec(
        num_scalar_prefetch=1,
        grid=(db, 2 * n_kv),
        in_specs=[pl.BlockSpec((ts, B_DIM), lambda b, j, pt: (row_blk + b, H_BQ // B_DIM)),
                  pl.BlockSpec((ts, B_KV_DIM), lambda b, j, pt: (row_blk + b, H_BK // B_KV_DIM)),
                  pl.BlockSpec((ts, B_KV_DIM), lambda b, j, pt: (row_blk + b, H_BV // B_KV_DIM))]
        + _page_specs(npg, kv_tail, layer, lambda j: jnp.minimum(j, n_kv - 1))
        + _page_specs(npg, kv_tail, layer, lambda j: jnp.maximum(j - n_kv, 0))
        + [pl.BlockSpec(memory_space=pl.ANY)],
        out_specs=pl.BlockSpec((ts, B_DIM), lambda b, j, pt: (row_blk + b, A_DIM // B_DIM)),
        scratch_shapes=[pltpu.VMEM((B_KV_HEADS, HEAD_DIM, LANES), BF16),
                        pltpu.VMEM((B_KV_HEADS, n_past, LANES), F32),
                        pltpu.VMEM((B_KV_HEADS, LANES, HEAD_DIM), F32),
                        pltpu.VMEM((B_KV_HEADS, LANES, LANES), F32),
                        pltpu.VMEM((B_KV_HEADS, LANES, LANES), F32),
                        pltpu.VMEM((B_KV_HEADS, 1, LANES), F32),
                        pltpu.VMEM((B_KV_HEADS, HEAD_DIM, LANES), F32)],
    )
    return pl.pallas_call(
        kern,
        grid_spec=grid_spec,
        out_shape=jax.ShapeDtypeStruct(mix.shape, mix.dtype),
        input_output_aliases={4 + 2 * npg: 0},
        compiler_params=_cparams(("arbitrary", "arbitrary")),
        name="moba_sample",
    )(page_table, h, h, h, *([cache_k] * npg), *([cache_v] * npg), mix)


def _row_tile(n_rows, target):
    best = None
    for t in range(16, target + 1, 16):
        if n_rows % t == 0:
            best = t
    assert best is not None, n_rows
    return best


def _pack_w_in(w_in):
    order = ("aq", "bq", "iq", "pu", "ak", "av", "bk", "bv", "ik", "iw")
    parts = [w_in[..., _SRC[n][0]:_SRC[n][0] + _SRC[n][1]] for n in order]
    used = sum(_SRC[n][1] for n in order)
    parts.append(jnp.zeros(w_in.shape[:-1] + (H_DIM - used,), w_in.dtype))
    return jnp.concatenate(parts, axis=-1).astype(BF16)


def kernel(x_prompt, x_sample, cache_a_k, cache_a_v, cache_a_idx, cache_b_k, cache_b_v, state_pool,
           cache_mem_k, cache_mem_v, page_table, mem_prompt, w_in, w_out, pool_w, pool_scale, ln1_g, ln1_b,
           w_mem_q, w_mem_k, w_mem_v, w_mem_o, ln2_g, ln2_b, w_ffn_1, w_ffn_3, w_ffn_2, ln3_g, ln3_b):
    batch, seq, d = x_prompt.shape
    db, ts, _ = x_sample.shape
    depth = w_in.shape[0]
    n_past = page_table.shape[1] * PAGE_SIZE
    n_prompt = batch * seq
    n_rows = n_prompt + db * ts
    tm = _row_tile(n_rows, 700)

    w_in_p = _pack_w_in(w_in)
    w_out_b = w_out.astype(BF16)
    w_mq_b = w_mem_q.astype(BF16)
    w_mkv_b = jnp.concatenate([w_mem_k, w_mem_v], axis=-1).astype(BF16)
    w_mo_b = w_mem_o.astype(BF16)
    w1_b, w3_b, w2_b = w_ffn_1.astype(BF16), w_ffn_3.astype(BF16), w_ffn_2.astype(BF16)
    pool_w_b = pool_w.astype(BF16)
    pool_scale3 = pool_scale.reshape(depth, 1, POOL_DIM)
    ln = [a.reshape(depth, 1, d) for a in (ln1_g, ln1_b, ln2_g, ln2_b, ln3_g, ln3_b)]
    mem = mem_prompt.reshape(batch * MEM_TOKENS, d)
    idx_t = jnp.swapaxes(cache_a_idx, 2, 3)

    x = jnp.concatenate([x_prompt.reshape(n_prompt, d), x_sample.reshape(db * ts, d)], axis=0)
    outs = {k: [] for k in ("pa_k", "pa_v", "pa_i", "pb_k", "pb_v", "p_pool", "pm_k", "pm_v",
                            "sa_k", "sa_v", "sa_i", "sb_k", "sb_v", "s_pool")}
    for l in range(depth):
        h = matmul(x, w_in_p, l, tm, 768)
        mkv = matmul(mem, w_mkv_b, l, _row_tile(mem.shape[0], 512), MEM_DIM)

        mix = dsa_prompt(h, batch, seq, n_rows)
        mix = moba_prompt(h, mix, batch, seq)
        mix = pool_prompt(h, mix, pool_w_b, pool_scale3, l, batch, seq)
        bias = dsa_sample_select(page_table, h, idx_t, l, n_prompt, db, ts)
        mix = dsa_sample_attend(page_table, h, bias, cache_a_k, cache_a_v, mix, l, n_prompt, db, ts)
        mix = moba_sample(page_table, h, cache_b_k, cache_b_v, mix, l, n_prompt, db, ts)
        mix = pool_sample(h, mix, state_pool, pool_w_b, pool_scale3, l, n_prompt, db, ts, n_past)

        x1 = matmul_residual_ln(mix, w_out_b, x, ln[0], ln[1], l, tm, 512)
        q = matmul(x1, w_mq_b, l, tm, MEM_DIM)
        o = memattn_prompt(q, mkv, batch, seq)
        o = memattn_sample(q, cache_mem_k, cache_mem_v, o, l, n_prompt, db, ts)
        x2 = matmul_residual_ln(o, w_mo_b, x1, ln[2], ln[3], l, tm, MEM_DIM)
        x = ffn_ln(x2, w1_b, w3_b, w2_b, ln[4], ln[5], l, tm, 512)

        hp, hs = h[:n_prompt], h[n_prompt:]

        def cols(a, off, width, lead):
            return a[:, off:off + width].reshape(lead)

        kv_p, kv_s = (batch, seq, A_KV_HEADS, HEAD_DIM), (db, ts, A_KV_HEADS, HEAD_DIM)
        outs["pa_k"].append(cols(hp, H_AK, A_KV_DIM, kv_p))
        outs["pa_v"].append(cols(hp, H_AV, A_KV_DIM, kv_p))
        outs["pa_i"].append(cols(hp, H_IKW, IDX_DIM, (batch, seq, IDX_DIM)))
        outs["pb_k"].append(cols(hp, H_BK, B_KV_DIM, kv_p))
        outs["pb_v"].append(cols(hp, H_BV, B_KV_DIM, kv_p))
        pu_p = cols(hp, H_PU, POOL_DIM, (batch, seq, POOL_DIM))
        outs["p_pool"].append(pu_p[:, seq - POOL_STATE:])
        outs["pm_k"].append(mkv[:, :MEM_DIM].reshape(batch, MEM_TOKENS, MEM_HEADS, MEM_HEAD_DIM))
        outs["pm_v"].append(mkv[:, MEM_DIM:].reshape(batch, MEM_TOKENS, MEM_HEADS, MEM_HEAD_DIM))
        outs["sa_k"].append(cols(hs, H_AK, A_KV_DIM, kv_s))
        outs["sa_v"].append(cols(hs, H_AV, A_KV_DIM, kv_s))
        outs["sa_i"].append(cols(hs, H_IKW, IDX_DIM, (db, ts, IDX_DIM)))
        outs["sb_k"].append(cols(hs, H_BK, B_KV_DIM, kv_s))
        outs["sb_v"].append(cols(hs, H_BV, B_KV_DIM, kv_s))
        pu_s = cols(hs, H_PU, POOL_DIM, (db, ts, POOL_DIM))
        outs["s_pool"].append(jnp.concatenate([state_pool[l], pu_s], axis=1)[:, -POOL_STATE:])

    st = {k: jnp.stack(v) for k, v in outs.items()}
    return (x[:n_prompt].reshape(batch, seq, d), x[n_prompt:].reshape(db, ts, d),
            st["pa_k"], st["pa_v"], st["pa_i"], st["pb_k"], st["pb_v"], st["p_pool"], st["pm_k"], st["pm_v"],
            st["sa_k"], st["sa_v"], st["sa_i"], st["sb_k"], st["sb_v"], st["s_pool"])
```

```python
import functools
import math

import jax
import jax.numpy as jnp
from jax import lax
from jax.experimental import pallas as pl
from jax.experimental.pallas import tpu as pltpu

F32 = jnp.float32
BF16 = jnp.bfloat16
I32 = jnp.int32

D_MODEL = 2048
DEPTH = 4
PAGE_SIZE = 128
HEAD_DIM = 128
A_HEADS = 6
A_KV_HEADS = 2
A_GROUP = A_HEADS // A_KV_HEADS
IDX_HEADS = 8
IDX_DIM = 64
DSA_TOPK = 256
B_HEADS = 6
B_KV_HEADS = 2
B_GROUP = B_HEADS // B_KV_HEADS
MOBA_BLOCK = 256
MOBA_TOPK = 3
POOL_WINDOWS = (2, 4, 8, 16)
POOL_GROUP_DIM = 128
POOL_DIM = len(POOL_WINDOWS) * POOL_GROUP_DIM
POOL_STATE = max(POOL_WINDOWS) - 1
A_DIM = A_HEADS * HEAD_DIM
A_KV_DIM = A_KV_HEADS * HEAD_DIM
B_DIM = B_HEADS * HEAD_DIM
B_KV_DIM = B_KV_HEADS * HEAD_DIM
MIX_DIM = A_DIM + B_DIM + POOL_DIM
MEM_TOKENS = 256
MEM_HEADS = 4
MEM_HEAD_DIM = 128
MEM_DIM = MEM_HEADS * MEM_HEAD_DIM
FFN_HIDDEN = -(-(8 * D_MODEL) // (3 * 256)) * 256
ALPHA = (2.0 * DEPTH) ** 0.25
LN_EPS = 1e-5
NEG_INF = -1e30
INT_MIN = -(2 ** 31)
SOFTMAX_C = (HEAD_DIM ** -0.5) * math.log2(math.e)

LANES = 128
SUBLANES = 8
VMEM_LIMIT_BYTES = 56 * 1024 * 1024

_SRC = dict(aq=(0, 768), ak=(768, 256), av=(1024, 256), iq=(1280, 512), ik=(1792, 64), iw=(1856, 8),
            bq=(1864, 768), bk=(2632, 256), bv=(2888, 256), pu=(3144, 512))
H_AQ, H_BQ, H_IQ, H_PU, H_AK, H_AV, H_BK, H_BV, H_IKW = 0, 768, 1536, 2048, 2560, 2816, 3072, 3328, 3584
H_DIM = 3840

DSA_TQ = 128
DSA_CK = 256
DSA_CA = 512
MOBA_TQ = 128
POOL_TT = 512
MEM_TR = 512
IDX_PAGES_PER_STEP = 64
KV_PAGES_PER_STEP = 32


def _cparams(sem):
    return pltpu.CompilerParams(dimension_semantics=sem, vmem_limit_bytes=VMEM_LIMIT_BYTES)


def _layer_norm_rows(y, g, b):
    mu = jnp.mean(y, axis=-1, keepdims=True)
    yc = y - mu
    var = jnp.mean(yc * yc, axis=-1, keepdims=True)
    return yc * lax.rsqrt(var + LN_EPS) * g + b


def _mm_kernel(x_ref, w_ref, o_ref, xb_ref):
    @pl.when(pl.program_id(1) == 0)
    def _():
        xb_ref[...] = x_ref[...].astype(BF16)

    o_ref[...] = jnp.dot(xb_ref[...], w_ref[...], preferred_element_type=F32)


def matmul(x, w, layer, tm, tn):
    m, k = x.shape
    n = w.shape[2]
    return pl.pallas_call(
        _mm_kernel,
        grid=(m // tm, n // tn),
        in_specs=[pl.BlockSpec((tm, k), lambda i, j: (i, 0)),
                  pl.BlockSpec((None, k, tn), lambda i, j: (layer, 0, j))],
        out_specs=pl.BlockSpec((tm, tn), lambda i, j: (i, j)),
        out_shape=jax.ShapeDtypeStruct((m, n), F32),
        scratch_shapes=[pltpu.VMEM((tm, k), BF16)],
        compiler_params=_cparams(("parallel", "arbitrary")),
        name="matmul",
    )(x, w)


def _mm_rows_kernel(x_ref, w_ref, o_in_ref, o_ref):
    del o_in_ref
    o_ref[...] = jnp.dot(x_ref[...].astype(BF16), w_ref[...], preferred_element_type=F32)


def project_rows(x, w, layer, out, row_block, rows, tn):
    k = x.shape[1]
    n = w.shape[2]
    return pl.pallas_call(
        _mm_rows_kernel,
        grid=(n // tn,),
        in_specs=[pl.BlockSpec((rows, k), lambda j: (row_block, 0)),
                  pl.BlockSpec((None, k, tn), lambda j: (layer, 0, j)),
                  pl.BlockSpec(memory_space=pl.ANY)],
        out_specs=pl.BlockSpec((rows, tn), lambda j: (row_block, j)),
        out_shape=jax.ShapeDtypeStruct(out.shape, out.dtype),
        input_output_aliases={2: 0},
        compiler_params=_cparams(("arbitrary",)),
        name="project_rows",
    )(x, w, out)


H_TN = 768
_KV_PIECES = (H_AK, H_AV, H_BK, H_BV)


def _store_head_rows(ref, c, rows, n_heads, value):
    ref.reshape(rows * n_heads, HEAD_DIM)[pl.ds(c, rows, stride=n_heads), :] = value


def _project_in_kernel(x_ref, w_ref, *refs, tm):
    o_ref = refs[-6]
    kv_refs = refs[-5:-1]
    xb_ref = refs[-1]
    j = pl.program_id(1)

    @pl.when(j == 0)
    def _():
        xb_ref[...] = x_ref[...].astype(BF16)

    o_ref[...] = jnp.dot(xb_ref[...], w_ref[...], preferred_element_type=F32)
    for piece, kv_ref in zip(_KV_PIECES, kv_refs):
        step, off = divmod(piece, H_TN)

        @pl.when(j == step)
        def _(kv_ref=kv_ref, off=off):
            for c in range(A_KV_HEADS):
                lo = off + c * HEAD_DIM
                _store_head_rows(kv_ref, c, tm, A_KV_HEADS, o_ref[:, lo:lo + HEAD_DIM])


def project_in(x, w, layer, tm, n_prompt, kv_prev):
    m, k = x.shape
    depth, _, n = w.shape
    kv_shape = jax.ShapeDtypeStruct((depth, n_prompt, A_KV_HEADS, HEAD_DIM), F32)
    n_prev = 0 if kv_prev is None else len(kv_prev)
    kv_spec = pl.BlockSpec((None, tm, A_KV_HEADS, HEAD_DIM), lambda i, j: (layer, i, 0, 0))
    outs = pl.pallas_call(
        functools.partial(_project_in_kernel, tm=tm),
        grid=(n_prompt // tm, n // H_TN),
        in_specs=[pl.BlockSpec((tm, k), lambda i, j: (i, 0)),
                  pl.BlockSpec((None, k, H_TN), lambda i, j: (layer, 0, j))]
        + [pl.BlockSpec(memory_space=pl.ANY)] * n_prev,
        out_specs=[pl.BlockSpec((tm, H_TN), lambda i, j: (i, j))] + [kv_spec] * len(_KV_PIECES),
        out_shape=[jax.ShapeDtypeStruct((m, n), F32)] + [kv_shape] * len(_KV_PIECES),
        input_output_aliases={2 + i: 1 + i for i in range(n_prev)},
        scratch_shapes=[pltpu.VMEM((tm, k), BF16)],
        compiler_params=_cparams(("parallel", "arbitrary")),
        name="project_in",
    )(x, w, *(kv_prev or ()))
    return outs[0], tuple(outs[1:])


def _mm_res_ln_kernel(a_ref, w_ref, x_ref, g_ref, b_ref, o_ref, acc_ref, *, nk):
    kk = pl.program_id(1)

    @pl.when(kk == 0)
    def _():
        acc_ref[...] = jnp.zeros_like(acc_ref)

    acc_ref[...] += jnp.dot(a_ref[...].astype(BF16), w_ref[...], preferred_element_type=F32)

    @pl.when(kk == nk - 1)
    def _():
        y = ALPHA * x_ref[...] + acc_ref[...]
        o_ref[...] = _layer_norm_rows(y, g_ref[...], b_ref[...])


def matmul_residual_ln(a, w, x, g, b, layer, tm, tk):
    m, k = a.shape
    n = w.shape[2]
    nk = k // tk
    return pl.pallas_call(
        functools.partial(_mm_res_ln_kernel, nk=nk),
        grid=(m // tm, nk),
        in_specs=[pl.BlockSpec((tm, tk), lambda i, j: (i, j)),
                  pl.BlockSpec((None, tk, n), lambda i, j: (layer, j, 0)),
                  pl.BlockSpec((tm, n), lambda i, j: (i, 0)),
                  pl.BlockSpec((None, 1, n), lambda i, j: (layer, 0, 0)),
                  pl.BlockSpec((None, 1, n), lambda i, j: (layer, 0, 0))],
        out_specs=pl.BlockSpec((tm, n), lambda i, j: (i, 0)),
        out_shape=jax.ShapeDtypeStruct((m, n), F32),
        scratch_shapes=[pltpu.VMEM((tm, n), F32)],
        compiler_params=_cparams(("parallel", "arbitrary")),
        name="matmul_residual_ln",
    )(a, w, x, g, b)


def _ffn_kernel(x_ref, w1_ref, w3_ref, w2_ref, g_ref, b_ref, o_ref, xb_ref, acc_ref, *, nf):
    j = pl.program_id(1)

    @pl.when(j == 0)
    def _():
        xb_ref[...] = x_ref[...].astype(BF16)
        acc_ref[...] = jnp.zeros_like(acc_ref)

    xb = xb_ref[...]
    h1 = jnp.dot(xb, w1_ref[...], preferred_element_type=F32)
    h3 = jnp.dot(xb, w3_ref[...], preferred_element_type=F32)
    h = (h1 / (1.0 + jnp.exp(-h1))) * h3
    acc_ref[...] += jnp.dot(h.astype(BF16), w2_ref[...], preferred_element_type=F32)

    @pl.when(j == nf - 1)
    def _():
        y = ALPHA * x_ref[...] + acc_ref[...]
        o_ref[...] = _layer_norm_rows(y, g_ref[...], b_ref[...])


def ffn_ln(x, w1, w3, w2, g, b, layer, tm, tf):
    m, d = x.shape
    f = w1.shape[2]
    nf = f // tf
    return pl.pallas_call(
        functools.partial(_ffn_kernel, nf=nf),
        grid=(m // tm, nf),
        in_specs=[pl.BlockSpec((tm, d), lambda i, j: (i, 0)),
                  pl.BlockSpec((None, d, tf), lambda i, j: (layer, 0, j)),
                  pl.BlockSpec((None, d, tf), lambda i, j: (layer, 0, j)),
                  pl.BlockSpec((None, tf, d), lambda i, j: (layer, j, 0)),
                  pl.BlockSpec((None, 1, d), lambda i, j: (layer, 0, 0)),
                  pl.BlockSpec((None, 1, d), lambda i, j: (layer, 0, 0))],
        out_specs=pl.BlockSpec((tm, d), lambda i, j: (i, 0)),
        out_shape=jax.ShapeDtypeStruct((m, d), F32),
        scratch_shapes=[pltpu.VMEM((tm, d), BF16), pltpu.VMEM((tm, d), F32)],
        compiler_params=_cparams(("parallel", "arbitrary")),
        name="ffn_ln",
    )(x, w1, w3, w2, g, b)


def _sortable_key(score):
    score = jnp.where(score == 0.0, 0.0, score)
    bits = pltpu.bitcast(score, I32)
    return jnp.where(bits < 0, bits ^ 0x7FFFFFFF, bits)


def _kth_largest(count_ge, shape, k_sel):
    c0 = count_ge(jnp.zeros(shape, I32))
    ok0 = c0 >= k_sel
    base = jnp.where(ok0, 0, INT_MIN).astype(I32)
    cnt = jnp.where(ok0, c0, jnp.iinfo(jnp.int32).max).astype(I32)

    def bit_body(i, carry):
        base, cnt = carry
        cand = base | jnp.left_shift(jnp.int32(1), 30 - i)
        c = count_ge(cand)
        ok = c >= k_sel
        return jnp.where(ok, cand, base), jnp.where(ok, c, cnt)

    return lax.fori_loop(0, 31, bit_body, (base, cnt))


def _flash_step_t(kc, vt, qt, bias, m, l, acc):
    lg = jnp.dot(kc, qt, preferred_element_type=F32) + bias
    m_new = jnp.maximum(m, jnp.max(lg, axis=0, keepdims=True))
    alpha = jnp.exp2(m - m_new)
    p = jnp.exp2(lg - m_new)
    l_new = alpha * l + jnp.sum(p, axis=0, keepdims=True)
    acc_new = alpha * acc + jnp.dot(vt, p.astype(BF16), preferred_element_type=F32)
    return m_new, l_new, acc_new


def _prep_kv_t(k_ref, v_ref, k_s, vt_s, rows, n_heads):
    kk = k_ref[rows, :]
    vv = v_ref[rows, :]
    for c in range(n_heads):
        cols = slice(c * HEAD_DIM, (c + 1) * HEAD_DIM)
        k_s[c, rows, :] = kk[:, cols].astype(BF16)
        vt_s[c, :, rows] = vv[:, cols].T.astype(BF16)
    return kk


def _stage_queries_t(q, qt_s, n_kv, group):
    q = q * SOFTMAX_C
    for c in range(n_kv):
        qt_s[c] = jnp.concatenate(
            [q[:, (c * group + g) * HEAD_DIM:(c * group + g + 1) * HEAD_DIM].T for g in range(group)],
            axis=1).astype(BF16)


def _head_q(qt_s, hh, tq):
    c, g = divmod(hh, A_GROUP)
    return qt_s[c, :, g * tq:(g + 1) * tq]


def _write_heads(acc_s, ls, o_ref, n_heads):
    for hh in range(n_heads):
        o_ref[:, hh * HEAD_DIM:(hh + 1) * HEAD_DIM] = (acc_s[hh] / ls[hh]).T


def _dsa_prompt_kernel(aq_ref, iq_ref, ikw_ref, ak_ref, av_ref, o_ref,
                       ik_s, k_s, vt_s, iqt_s, iwt_s, qt_s, keys_s, bias_s, acc_s, *, seq, k_sel):
    tq, ck = DSA_TQ, DSA_CK
    qi = pl.program_id(1)
    q0 = qi * tq
    prep_rows = min(512, seq)

    @pl.when(qi == 0)
    def _prep():
        lane = lax.broadcasted_iota(I32, (prep_rows, LANES), 1)

        def body(r, _):
            rows = pl.ds(pl.multiple_of(r * prep_rows, prep_rows), prep_rows)
            ik_s[rows, :] = jnp.where(lane < IDX_DIM, ikw_ref[rows, :], 0.0).astype(BF16)
            _prep_kv_t(ak_ref, av_ref, k_s, vt_s, rows, A_KV_HEADS)
            return 0

        lax.fori_loop(0, seq // prep_rows, body, 0)

    n_ck = (q0 + tq + ck - 1) // ck
    ikw_q = ikw_ref[pl.ds(pl.multiple_of(q0, tq), tq), :]
    iwt_s[...] = ikw_q.T[IDX_DIM:IDX_DIM + IDX_HEADS] * (IDX_HEADS ** -0.5)
    iqt = (iq_ref[...] * (IDX_DIM ** -0.5)).T
    zpad = jnp.zeros((LANES - IDX_DIM, 2 * tq), F32)
    for hp in range(IDX_HEADS // 2):
        pair = jnp.concatenate([iqt[(2 * hp) * IDX_DIM:(2 * hp + 1) * IDX_DIM],
                                iqt[(2 * hp + 1) * IDX_DIM:(2 * hp + 2) * IDX_DIM]], axis=1)
        iqt_s[hp] = jnp.concatenate([pair, zpad], axis=0).astype(BF16)
    _stage_queries_t(aq_ref[...], qt_s, A_KV_HEADS, A_GROUP)

    qpos = q0 + lax.broadcasted_iota(I32, (ck, tq), 1)
    krow = lax.broadcasted_iota(I32, (ck, tq), 0)

    def score_chunk(off):
        ikc = ik_s[pl.ds(off, ck), :]
        acc = jnp.zeros((ck, tq), F32)
        for hp in range(IDX_HEADS // 2):
            s = jnp.dot(ikc, iqt_s[hp], preferred_element_type=F32)
            acc = acc + (jnp.maximum(s[:, :tq], 0.0) * iwt_s[2 * hp:2 * hp + 1, :]
                         + jnp.maximum(s[:, tq:], 0.0) * iwt_s[2 * hp + 1:2 * hp + 2, :])
        keys_s[pl.ds(off, ck), :] = jnp.where(off + krow <= qpos, _sortable_key(acc), INT_MIN)

    def score_body(c, _):
        off = pl.multiple_of(c * (2 * ck), 2 * ck)
        score_chunk(off)
        score_chunk(off + ck)
        return 0

    lax.fori_loop(0, (n_ck + 1) // 2, score_body, 0)

    part = ck // 4

    def count_ge(cand):
        def body(c, acc):
            ind = jnp.where(keys_s[pl.ds(pl.multiple_of(c * ck, ck), ck), :] >= cand, 1, 0).astype(I32)
            return acc + ((ind[0:part] + ind[part:2 * part]) + (ind[2 * part:3 * part] + ind[3 * part:]))

        acc = lax.fori_loop(0, n_ck, body, jnp.zeros((part, tq), I32))
        return jnp.sum(acc, axis=0, keepdims=True)

    thr, cnt_thr = _kth_largest(count_ge, (1, tq), k_sel)
    has_tie = jnp.logical_and(cnt_thr > k_sel, thr > INT_MIN)
    any_tie = jnp.max(has_tie.astype(I32)) > 0
    thr_c = jnp.maximum(thr, INT_MIN + 1)

    @pl.when(jnp.logical_not(any_tie))
    def _fast():
        def body(c, _):
            rows = pl.ds(pl.multiple_of(c * ck, ck), ck)
            bias_s[rows, :] = jnp.where(keys_s[rows, :] >= thr_c, 0.0, NEG_INF)
            return 0

        lax.fori_loop(0, n_ck, body, 0)

    @pl.when(any_tie)
    def _ties():
        need = (k_sel - count_ge(thr_c + 1)).astype(F32)
        tri = jnp.where(lax.broadcasted_iota(I32, (LANES, LANES), 0)
                        >= lax.broadcasted_iota(I32, (LANES, LANES), 1), 1.0, 0.0).astype(BF16)

        def body(c, run):
            rows = pl.ds(pl.multiple_of(c * LANES, LANES), LANES)
            kchunk = keys_s[rows, :]
            eq = kchunk == thr_c
            eqf = jnp.where(eq, 1.0, 0.0)
            pre = jnp.dot(tri, eqf.astype(BF16), preferred_element_type=F32) + run
            sel = jnp.logical_or(kchunk > thr_c, jnp.logical_and(eq, pre <= need))
            bias_s[rows, :] = jnp.where(sel, 0.0, NEG_INF)
            return run + jnp.sum(eqf, axis=0, keepdims=True)

        lax.fori_loop(0, n_ck * (ck // LANES), body, jnp.zeros((1, tq), F32))

    ca = DSA_CA
    n_ca = (q0 + tq + ca - 1) // ca
    for extra in range(1, ca // ck):
        @pl.when(n_ck + extra <= n_ca * (ca // ck))
        def _mask_tail(extra=extra):
            bias_s[pl.ds(pl.multiple_of((n_ck + extra - 1) * ck, ck), ck), :] = jnp.full((ck, tq), NEG_INF, F32)

    acc_s[...] = jnp.zeros_like(acc_s)

    def att_body(j, carry):
        ms, ls = carry
        rows = pl.ds(pl.multiple_of(j * ca, ca), ca)
        bias = bias_s[rows, :]
        new_m, new_l = [], []
        for hh in range(A_HEADS):
            c = hh // A_GROUP
            m, l, acc = _flash_step_t(k_s[c, rows, :], vt_s[c, :, rows], _head_q(qt_s, hh, tq), bias,
                                      ms[hh], ls[hh], acc_s[hh])
            acc_s[hh] = acc
            new_m.append(m)
            new_l.append(l)
        return tuple(new_m), tuple(new_l)

    init = (tuple(jnp.full((1, tq), NEG_INF, F32) for _ in range(A_HEADS)),
            tuple(jnp.zeros((1, tq), F32) for _ in range(A_HEADS)))
    _, ls = lax.fori_loop(0, n_ca, att_body, init)
    _write_heads(acc_s, ls, o_ref, A_HEADS)


def dsa_prompt(h, batch, seq, n_rows):
    tq = DSA_TQ
    nq = seq // tq
    k_sel = min(DSA_TOPK, seq // 4)
    kern = functools.partial(_dsa_prompt_kernel, seq=seq, k_sel=k_sel)
    return pl.pallas_call(
        kern,
        grid=(batch, nq),
        in_specs=[pl.BlockSpec((tq, A_DIM), lambda b, i: (b * nq + i, H_AQ // A_DIM)),
                  pl.BlockSpec((tq, IDX_HEADS * IDX_DIM), lambda b, i: (b * nq + i, H_IQ // (IDX_HEADS * IDX_DIM))),
                  pl.BlockSpec((seq, LANES), lambda b, i: (b, H_IKW // LANES)),
                  pl.BlockSpec((seq, A_KV_DIM), lambda b, i: (b, H_AK // A_KV_DIM)),
                  pl.BlockSpec((seq, A_KV_DIM), lambda b, i: (b, H_AV // A_KV_DIM))],
        out_specs=pl.BlockSpec((tq, A_DIM), lambda b, i: (b * nq + i, 0)),
        out_shape=jax.ShapeDtypeStruct((n_rows, MIX_DIM), F32),
        scratch_shapes=[pltpu.VMEM((seq, LANES), BF16),
                        pltpu.VMEM((A_KV_HEADS, seq, HEAD_DIM), BF16),
                        pltpu.VMEM((A_KV_HEADS, HEAD_DIM, seq), BF16),
                        pltpu.VMEM((IDX_HEADS // 2, LANES, 2 * tq), BF16),
                        pltpu.VMEM((IDX_HEADS, tq), F32),
                        pltpu.VMEM((A_KV_HEADS, HEAD_DIM, A_GROUP * tq), BF16),
                        pltpu.VMEM((seq, tq), I32),
                        pltpu.VMEM((seq, tq), F32),
                        pltpu.VMEM((A_HEADS, HEAD_DIM, tq), F32)],
        compiler_params=_cparams(("arbitrary", "arbitrary")),
        name="dsa_prompt",
    )(h, h, h, h, h)


def _rank_select(gate, n_past_blocks, axis):
    idx = lax.broadcasted_iota(I32, gate.shape, axis)

    def body(m, rank):
        gm = jnp.sum(jnp.where(idx == m, gate, 0.0), axis=axis, keepdims=True)
        beats = jnp.logical_or(gm > gate, jnp.logical_and(gm == gate, m < idx))
        return rank + jnp.where(beats, 1, 0).astype(I32)

    rank = lax.fori_loop(0, n_past_blocks, body, jnp.zeros(gate.shape, I32))
    sel = jnp.logical_and(idx < n_past_blocks, rank < MOBA_TOPK)
    return jnp.where(sel, 0.0, NEG_INF)


def _moba_prompt_kernel(bq_ref, bk_ref, bv_ref, mix_ref, o_ref, k_s, vt_s, km_s, kmb_s, qt_s, sb_s, acc_s,
                        *, seq, nbp):
    del mix_ref
    tq, blk = MOBA_TQ, MOBA_BLOCK
    qi = pl.program_id(1)
    q0 = qi * tq
    nb = seq // blk

    @pl.when(qi == 0)
    def _prep():
        km_s[...] = jnp.zeros_like(km_s)

        def body(n, _):
            rows = pl.ds(pl.multiple_of(n * blk, blk), blk)
            kk = _prep_kv_t(bk_ref, bv_ref, k_s, vt_s, rows, B_KV_HEADS)
            km_s[pl.ds(n, 1), :] = jnp.mean(kk, axis=0, keepdims=True)
            return 0

        lax.fori_loop(0, nb, body, 0)
        for c in range(B_KV_HEADS):
            kmb_s[c] = km_s[:, c * HEAD_DIM:(c + 1) * HEAD_DIM].astype(BF16)

    own = q0 // blk
    own_rows = pl.ds(pl.multiple_of(own * blk, blk), blk)
    bq = bq_ref[...]
    _stage_queries_t(bq, qt_s, B_KV_HEADS, B_GROUP)
    gates = []
    for c in range(B_KV_HEADS):
        qg = jnp.concatenate(
            [bq[:, (c * B_GROUP + g) * HEAD_DIM:(c * B_GROUP + g + 1) * HEAD_DIM].T for g in range(B_GROUP)],
            axis=1).astype(BF16)
        gates.append(jnp.dot(kmb_s[c], qg, preferred_element_type=F32)[:nbp])
    sel_bias = _rank_select(jnp.concatenate(gates, axis=1), own, axis=0)
    for hh in range(B_HEADS):
        sb_s[hh] = sel_bias[:, hh * tq:(hh + 1) * tq]

    qpos = q0 + lax.broadcasted_iota(I32, (blk, tq), 1)
    kpos = own * blk + lax.broadcasted_iota(I32, (blk, tq), 0)
    causal_bias = jnp.where(kpos <= qpos, 0.0, NEG_INF)
    ms, ls = [], []
    for hh in range(B_HEADS):
        c = hh // B_GROUP
        m, l, acc = _flash_step_t(k_s[c, own_rows, :], vt_s[c, :, own_rows], _head_q(qt_s, hh, tq), causal_bias,
                                  jnp.full((1, tq), NEG_INF, F32), jnp.zeros((1, tq), F32),
                                  jnp.zeros((HEAD_DIM, tq), F32))
        acc_s[hh] = acc
        ms.append(m)
        ls.append(l)

    odd = own % 2

    def pair_body(p, carry):
        ms, ls = carry
        b0 = jnp.maximum(2 * p - odd, 0)
        rows = pl.ds(pl.multiple_of(b0 * blk, blk), 2 * blk)
        skip_second = jnp.where(jnp.logical_and(p == 0, odd == 1), NEG_INF, 0.0)
        new_m, new_l = [], []
        for hh in range(B_HEADS):
            c = hh // B_GROUP
            bias = jnp.concatenate(
                [jnp.broadcast_to(sb_s[hh, pl.ds(b0, 1), :], (blk, tq)),
                 jnp.broadcast_to(sb_s[hh, pl.ds(b0 + 1, 1), :] + skip_second, (blk, tq))], axis=0)
            m, l, acc = _flash_step_t(k_s[c, rows, :], vt_s[c, :, rows], _head_q(qt_s, hh, tq), bias,
                                      ms[hh], ls[hh], acc_s[hh])
            acc_s[hh] = acc
            new_m.append(m)
            new_l.append(l)
        return tuple(new_m), tuple(new_l)

    _, ls = lax.fori_loop(0, (own + 1) // 2, pair_body, (tuple(ms), tuple(ls)))
    _write_heads(acc_s, ls, o_ref, B_HEADS)


def moba_prompt(h, mix, batch, seq):
    tq = MOBA_TQ
    nq = seq // tq
    nbp = -(-(seq // MOBA_BLOCK) // SUBLANES) * SUBLANES
    return pl.pallas_call(
        functools.partial(_moba_prompt_kernel, seq=seq, nbp=nbp),
        grid=(batch, nq),
        in_specs=[pl.BlockSpec((tq, B_DIM), lambda b, i: (b * nq + i, H_BQ // B_DIM)),
                  pl.BlockSpec((seq, B_KV_DIM), lambda b, i: (b, H_BK // B_KV_DIM)),
                  pl.BlockSpec((seq, B_KV_DIM), lambda b, i: (b, H_BV // B_KV_DIM)),
                  pl.BlockSpec(memory_space=pl.ANY)],
        out_specs=pl.BlockSpec((tq, B_DIM), lambda b, i: (b * nq + i, A_DIM // B_DIM)),
        out_shape=jax.ShapeDtypeStruct(mix.shape, mix.dtype),
        input_output_aliases={3: 0},
        scratch_shapes=[pltpu.VMEM((B_KV_HEADS, seq, HEAD_DIM), BF16),
                        pltpu.VMEM((B_KV_HEADS, HEAD_DIM, seq), BF16),
                        pltpu.VMEM((LANES, B_KV_DIM), F32),
                        pltpu.VMEM((B_KV_HEADS, LANES, HEAD_DIM), BF16),
                        pltpu.VMEM((B_KV_HEADS, HEAD_DIM, B_GROUP * tq), BF16),
                        pltpu.VMEM((B_HEADS, nbp, tq), F32),
                        pltpu.VMEM((B_HEADS, HEAD_DIM, tq), F32)],
        compiler_params=_cparams(("arbitrary", "arbitrary")),
        name="moba_prompt",
    )(h, h, h, mix)


POOL_HALO = 16


def _pool_groups(u, window_sum, cnt_of, w_ref, sc_ref, o_ref):
    for g, w in enumerate(POOL_WINDOWS):
        cols = slice(g * POOL_GROUP_DIM, (g + 1) * POOL_GROUP_DIM)
        ug = u[:, cols]
        d = window_sum(g, w, ug) / cnt_of(w) - ug
        y = jnp.dot(d.astype(BF16), w_ref[g], preferred_element_type=F32)
        o_ref[:, cols] = y * sc_ref[:, cols]


def _pool_prompt_kernel(u_ref, halo_ref, w_ref, sc_ref, mix_ref, o_ref, ext_s):
    del mix_ref
    tt = POOL_TT
    i = pl.program_id(1)
    u = u_ref[...]
    ext_s[0:POOL_HALO, :] = jnp.where(i == 0, 0.0, halo_ref[...])
    ext_s[POOL_HALO:POOL_HALO + tt, :] = u
    t = i * tt + lax.broadcasted_iota(I32, (tt, 1), 0)

    def window_sum(g, w, ug):
        s = ug
        for k in range(1, w):
            s = s + ext_s[pl.ds(POOL_HALO - k, tt), g * POOL_GROUP_DIM:(g + 1) * POOL_GROUP_DIM]
        return s

    _pool_groups(u, window_sum, lambda w: jnp.minimum(w, t + 1).astype(F32), w_ref, sc_ref, o_ref)


def pool_prompt(h, mix, pool_w, pool_scale, layer, batch, seq):
    tt = POOL_TT
    nt = seq // tt
    halo_blocks = tt // POOL_HALO
    return pl.pallas_call(
        _pool_prompt_kernel,
        grid=(batch, nt),
        in_specs=[pl.BlockSpec((tt, POOL_DIM), lambda b, i: (b * nt + i, H_PU // POOL_DIM)),
                  pl.BlockSpec((POOL_HALO, POOL_DIM),
                               lambda b, i: (jnp.maximum((b * nt + i) * halo_blocks - 1, 0), H_PU // POOL_DIM)),
                  pl.BlockSpec((None, len(POOL_WINDOWS), POOL_GROUP_DIM, POOL_GROUP_DIM),
                               lambda b, i: (layer, 0, 0, 0)),
                  pl.BlockSpec((None, 1, POOL_DIM), lambda b, i: (layer, 0, 0)),
                  pl.BlockSpec(memory_space=pl.ANY)],
        out_specs=pl.BlockSpec((tt, POOL_DIM), lambda b, i: (b * nt + i, (A_DIM + B_DIM) // POOL_DIM)),
        out_shape=jax.ShapeDtypeStruct(mix.shape, mix.dtype),
        input_output_aliases={4: 0},
        scratch_shapes=[pltpu.VMEM((POOL_HALO + tt, POOL_DIM), F32)],
        compiler_params=_cparams(("arbitrary", "arbitrary")),
        name="pool_prompt",
    )(h, h, pool_w, pool_scale, mix)


def _pool_sample_kernel(u_ref, st_ref, w_ref, sc_ref, mix_ref, o_ref, ext_s, *, db, ts, n_past):
    del mix_ref
    u = u_ref[...]
    for b in range(db):
        ext_s[b, POOL_HALO - POOL_STATE:POOL_HALO, :] = st_ref[b]
        ext_s[b, POOL_HALO:POOL_HALO + ts, :] = u[b * ts:(b + 1) * ts]
    t = n_past + lax.broadcasted_iota(I32, (db * ts, 1), 0) % ts

    def window_sum(g, w, ug):
        parts = []
        for b in range(db):
            s = ug[b * ts:(b + 1) * ts]
            for k in range(1, w):
                s = s + ext_s[b, pl.ds(POOL_HALO - k, ts), g * POOL_GROUP_DIM:(g + 1) * POOL_GROUP_DIM]
            parts.append(s)
        return jnp.concatenate(parts, axis=0)

    _pool_groups(u, window_sum, lambda w: jnp.minimum(w, t + 1).astype(F32), w_ref, sc_ref, o_ref)


def pool_sample(h, mix, state_pool, pool_w, pool_scale, layer, n_prompt, db, ts, n_past):
    rows = db * ts
    return pl.pallas_call(
        functools.partial(_pool_sample_kernel, db=db, ts=ts, n_past=n_past),
        grid=(1,),
        in_specs=[pl.BlockSpec((rows, POOL_DIM), lambda i: (n_prompt // rows, H_PU // POOL_DIM)),
                  pl.BlockSpec((None, db, POOL_STATE, POOL_DIM), lambda i: (layer, 0, 0, 0)),
                  pl.BlockSpec((None, len(POOL_WINDOWS), POOL_GROUP_DIM, POOL_GROUP_DIM),
                               lambda i: (layer, 0, 0, 0)),
                  pl.BlockSpec((None, 1, POOL_DIM), lambda i: (layer, 0, 0)),
                  pl.BlockSpec(memory_space=pl.ANY)],
        out_specs=pl.BlockSpec((rows, POOL_DIM), lambda i: (n_prompt // rows, (A_DIM + B_DIM) // POOL_DIM)),
        out_shape=jax.ShapeDtypeStruct(mix.shape, mix.dtype),
        input_output_aliases={4: 0},
        scratch_shapes=[pltpu.VMEM((db, POOL_HALO + ts, POOL_DIM), F32)],
        compiler_params=_cparams(("arbitrary",)),
        name="pool_sample",
    )(h, state_pool, pool_w, pool_scale, mix)


def _memattn_heads(q_ref, head_k, head_v, o_ref):
    scale = MEM_HEAD_DIM ** -0.5
    for hh in range(MEM_HEADS):
        cols = slice(hh * MEM_HEAD_DIM, (hh + 1) * MEM_HEAD_DIM)
        qh = q_ref[:, cols].astype(BF16)
        lg = lax.dot_general(qh, head_k(hh).astype(BF16), (((1,), (1,)), ((), ())),
                             preferred_element_type=F32) * scale
        m = jnp.max(lg, axis=1, keepdims=True)
        p = jnp.exp(lg - m)
        l = jnp.sum(p, axis=1, keepdims=True)
        o_ref[:, cols] = jnp.dot(p.astype(BF16), head_v(hh).astype(BF16), preferred_element_type=F32) / l


def _memattn_prompt_kernel(q_ref, mk_ref, mv_ref, o_ref):
    _memattn_heads(q_ref, lambda hh: mk_ref[:, hh * MEM_HEAD_DIM:(hh + 1) * MEM_HEAD_DIM],
                   lambda hh: mv_ref[:, hh * MEM_HEAD_DIM:(hh + 1) * MEM_HEAD_DIM], o_ref)


def _memattn_sample_kernel(q_ref, mk_ref, mv_ref, o_in_ref, o_ref):
    del o_in_ref
    _memattn_heads(q_ref, lambda hh: _head_rows(mk_ref, hh, MEM_TOKENS, MEM_HEADS),
                   lambda hh: _head_rows(mv_ref, hh, MEM_TOKENS, MEM_HEADS), o_ref)


def memattn_prompt(q, mkv, batch, seq):
    tr = min(MEM_TR, seq)
    nt = seq // tr
    return pl.pallas_call(
        _memattn_prompt_kernel,
        grid=(batch, nt),
        in_specs=[pl.BlockSpec((tr, MEM_DIM), lambda b, i: (b * nt + i, 0)),
                  pl.BlockSpec((MEM_TOKENS, MEM_DIM), lambda b, i: (b, 0)),
                  pl.BlockSpec((MEM_TOKENS, MEM_DIM), lambda b, i: (b, 1))],
        out_specs=pl.BlockSpec((tr, MEM_DIM), lambda b, i: (b * nt + i, 0)),
        out_shape=jax.ShapeDtypeStruct(q.shape, F32),
        compiler_params=_cparams(("arbitrary", "arbitrary")),
        name="memattn_prompt",
    )(q, mkv, mkv)


def memattn_sample(q, cache_k, cache_v, o, layer, n_prompt, db, ts):
    cache_spec = pl.BlockSpec((None, None, MEM_TOKENS, MEM_HEADS, MEM_HEAD_DIM), lambda b: (layer, b, 0, 0, 0))
    return pl.pallas_call(
        _memattn_sample_kernel,
        grid=(db,),
        in_specs=[pl.BlockSpec((ts, MEM_DIM), lambda b: (n_prompt // ts + b, 0)),
                  cache_spec, cache_spec,
                  pl.BlockSpec(memory_space=pl.ANY)],
        out_specs=pl.BlockSpec((ts, MEM_DIM), lambda b: (n_prompt // ts + b, 0)),
        out_shape=jax.ShapeDtypeStruct(o.shape, o.dtype),
        input_output_aliases={3: 0},
        compiler_params=_cparams(("arbitrary",)),
        name="memattn_sample",
    )(q, cache_k, cache_v, o)


def _page_specs(npg, page_shape, layer, step_of):
    zeros = (0,) * len(page_shape)

    def spec(i):
        return pl.BlockSpec((None, None) + page_shape,
                            lambda b, j, pt: (layer, pt[b, step_of(j) * npg + i]) + zeros)
    return [spec(i) for i in range(npg)]


def _pad_rows(x, rows):
    return jnp.concatenate([x, jnp.zeros((rows - x.shape[0], x.shape[1]), x.dtype)], axis=0)


def _head_rows(ref, c, rows, n_heads):
    return ref.reshape(rows * n_heads, HEAD_DIM)[pl.ds(c, rows, stride=n_heads), :]


def _gather_head(pages, c, n_heads):
    return jnp.concatenate([_head_rows(p, c, PAGE_SIZE, n_heads) for p in pages], axis=0)


def _dsa_sample_select_kernel(pt_ref, iq_ref, ikw_ref, *refs, npg, n_steps, n_past, ts, k_sel):
    del pt_ref
    pages = refs[:npg]
    bias_ref, keys_s = refs[npg], refs[npg + 1]
    j = pl.program_id(1)
    ck = npg * PAGE_SIZE
    lp = n_past + LANES
    iq = iq_ref[...] * (IDX_DIM ** -0.5)
    iq_st = jnp.concatenate([iq[:, h * IDX_DIM:(h + 1) * IDX_DIM] for h in range(IDX_HEADS)],
                            axis=0).astype(BF16)
    ikw = ikw_ref[...]
    iw = ikw[:, IDX_DIM:IDX_DIM + IDX_HEADS] * (IDX_HEADS ** -0.5)

    def scores(ik_t):
        s = jnp.dot(iq_st, ik_t.astype(BF16), preferred_element_type=F32)
        acc = jnp.zeros((ts, ik_t.shape[1]), F32)
        for h in range(IDX_HEADS):
            acc = acc + jnp.maximum(s[h * ts:(h + 1) * ts], 0.0) * iw[:, h:h + 1]
        return acc

    ik_past = jnp.concatenate([pages[i][...] for i in range(npg)], axis=1)
    keys_s[:, pl.ds(pl.multiple_of(j * ck, ck), ck)] = _sortable_key(scores(ik_past))

    @pl.when(j == n_steps - 1)
    def _select():
        sc_new = scores(_pad_rows(ikw, LANES).T[:IDX_DIM])
        causal = lax.broadcasted_iota(I32, (ts, LANES), 1) <= lax.broadcasted_iota(I32, (ts, LANES), 0)
        keys_s[:, n_past:lp] = jnp.where(causal, _sortable_key(sc_new), INT_MIN)

        def count_ge(cand):
            return jnp.sum(jnp.where(keys_s[...] >= cand, 1, 0).astype(I32), axis=1, keepdims=True)

        thr, cnt_thr = _kth_largest(count_ge, (ts, 1), k_sel)
        has_tie = jnp.logical_and(cnt_thr > k_sel, thr > INT_MIN)
        any_tie = jnp.max(has_tie.astype(I32)) > 0
        thr_c = jnp.maximum(thr, INT_MIN + 1)

        @pl.when(jnp.logical_not(any_tie))
        def _fast():
            bias_ref[...] = jnp.where(keys_s[...] >= thr_c, 0.0, NEG_INF)

        @pl.when(any_tie)
        def _ties():
            need = (k_sel - count_ge(thr_c + 1)).astype(F32)
            tri = jnp.where(lax.broadcasted_iota(I32, (LANES, LANES), 0)
                            <= lax.broadcasted_iota(I32, (LANES, LANES), 1), 1.0, 0.0).astype(BF16)

            def body(c, run):
                cols = pl.ds(pl.multiple_of(c * LANES, LANES), LANES)
                kchunk = keys_s[:, cols]
                eq = kchunk == thr_c
                eqf = jnp.where(eq, 1.0, 0.0)
                pre = jnp.dot(eqf.astype(BF16), tri, preferred_element_type=F32) + run
                sel = jnp.logical_or(kchunk > thr_c, jnp.logical_and(eq, pre <= need))
                bias_ref[:, cols] = jnp.where(sel, 0.0, NEG_INF)
                return run + jnp.sum(eqf, axis=1, keepdims=True)

            lax.fori_loop(0, lp // LANES, body, jnp.zeros((ts, 1), F32))


def dsa_sample_select(page_table, h, cache_idx, layer, n_prompt, db, ts):
    n_pages = page_table.shape[1]
    n_past = n_pages * PAGE_SIZE
    npg = min(IDX_PAGES_PER_STEP, n_pages)
    n_steps = n_pages // npg
    k_sel = min(DSA_TOPK, (n_past + ts) // 4)
    lp = n_past + LANES
    kern = functools.partial(_dsa_sample_select_kernel, npg=npg, n_steps=n_steps, n_past=n_past, ts=ts, k_sel=k_sel)
    row_blk = n_prompt // ts
    grid_spec = pltpu.PrefetchScalarGridSpec(
        num_scalar_prefetch=1,
        grid=(db, n_steps),
        in_specs=[pl.BlockSpec((ts, IDX_HEADS * IDX_DIM), lambda b, j, pt: (row_blk + b, H_IQ // (IDX_HEADS * IDX_DIM))),
                  pl.BlockSpec((ts, LANES), lambda b, j, pt: (row_blk + b, H_IKW // LANES))]
        + _page_specs(npg, (IDX_DIM, PAGE_SIZE), layer, lambda j: j),
        out_specs=pl.BlockSpec((None, ts, lp), lambda b, j, pt: (b, 0, 0)),
        scratch_shapes=[pltpu.VMEM((ts, lp), I32)],
    )
    return pl.pallas_call(
        kern,
        grid_spec=grid_spec,
        out_shape=jax.ShapeDtypeStruct((db, ts, lp), F32),
        compiler_params=_cparams(("arbitrary", "arbitrary")),
        name="dsa_sample_select",
    )(page_table, h, h, *([cache_idx] * npg))


def _tile_bias_t(bs, group, ts):
    full = jnp.concatenate([bs] * group + [jnp.zeros((LANES - group * ts, bs.shape[1]), bs.dtype)], axis=0)
    return full.T


def _stack_heads_t(q, c, group, ts):
    qc = jnp.concatenate(
        [q[:, (c * group + g) * HEAD_DIM:(c * group + g + 1) * HEAD_DIM] for g in range(group)]
        + [jnp.zeros((LANES - group * ts, HEAD_DIM), q.dtype)], axis=0)
    return qc.T.astype(BF16)


def _dsa_sample_attend_kernel(pt_ref, q_ref, kn_ref, vn_ref, bias_ref, *refs, npg, n_steps, n_past, ts):
    del pt_ref
    kp, vp = refs[:npg], refs[npg:2 * npg]
    o_ref = refs[2 * npg + 1]
    qT_s, m_s, l_s, accT_s = refs[2 * npg + 2:]
    j = pl.program_id(1)
    ck = npg * PAGE_SIZE
    scale = HEAD_DIM ** -0.5

    @pl.when(j == 0)
    def _init():
        q = q_ref[...]
        for c in range(A_KV_HEADS):
            qT_s[c] = _stack_heads_t(q, c, A_GROUP, ts)
        m_s[...] = jnp.full(m_s.shape, NEG_INF, F32)
        l_s[...] = jnp.zeros_like(l_s)
        accT_s[...] = jnp.zeros_like(accT_s)

    def update(c, kc, vc, bias_t):
        lg = jnp.dot(kc.astype(BF16), qT_s[c], preferred_element_type=F32) * scale + bias_t
        m_old = m_s[c]
        m_new = jnp.maximum(m_old, jnp.max(lg, axis=0, keepdims=True))
        alpha = jnp.exp(m_old - m_new)
        p = jnp.exp(lg - m_new)
        l_s[c] = alpha * l_s[c] + jnp.sum(p, axis=0, keepdims=True)
        accT_s[c] = alpha * accT_s[c] + jnp.dot(vc.T.astype(BF16), p.astype(BF16), preferred_element_type=F32)
        m_s[c] = m_new

    bias_t = _tile_bias_t(bias_ref[:, pl.ds(pl.multiple_of(j * ck, ck), ck)], A_GROUP, ts)
    for c in range(A_KV_HEADS):
        update(c, _gather_head(kp, c, A_KV_HEADS), _gather_head(vp, c, A_KV_HEADS), bias_t)

    @pl.when(j == n_steps - 1)
    def _finish():
        bias_n = _tile_bias_t(bias_ref[:, n_past:n_past + LANES], A_GROUP, ts)
        for c in range(A_KV_HEADS):
            cols = slice(c * HEAD_DIM, (c + 1) * HEAD_DIM)
            update(c, _pad_rows(kn_ref[:, cols], LANES), _pad_rows(vn_ref[:, cols], LANES), bias_n)
            out = (accT_s[c] / l_s[c]).T
            for g in range(A_GROUP):
                hcol = (c * A_GROUP + g) * HEAD_DIM
                o_ref[:, hcol:hcol + HEAD_DIM] = out[g * ts:(g + 1) * ts]


def dsa_sample_attend(page_table, h, bias, cache_k, cache_v, mix, layer, n_prompt, db, ts):
    n_pages = page_table.shape[1]
    n_past = n_pages * PAGE_SIZE
    npg = min(KV_PAGES_PER_STEP, n_pages)
    n_steps = n_pages // npg
    lp = n_past + LANES
    kern = functools.partial(_dsa_sample_attend_kernel, npg=npg, n_steps=n_steps, n_past=n_past, ts=ts)
    row_blk = n_prompt // ts
    kv_tail = (PAGE_SIZE, A_KV_HEADS, HEAD_DIM)
    grid_spec = pltpu.PrefetchScalarGridSpec(
        num_scalar_prefetch=1,
        grid=(db, n_steps),
        in_specs=[pl.BlockSpec((ts, A_DIM), lambda b, j, pt: (row_blk + b, H_AQ // A_DIM)),
                  pl.BlockSpec((ts, A_KV_DIM), lambda b, j, pt: (row_blk + b, H_AK // A_KV_DIM)),
                  pl.BlockSpec((ts, A_KV_DIM), lambda b, j, pt: (row_blk + b, H_AV // A_KV_DIM)),
                  pl.BlockSpec((None, ts, lp), lambda b, j, pt: (b, 0, 0))]
        + _page_specs(npg, kv_tail, layer, lambda j: j)
        + _page_specs(npg, kv_tail, layer, lambda j: j)
        + [pl.BlockSpec(memory_space=pl.ANY)],
        out_specs=pl.BlockSpec((ts, A_DIM), lambda b, j, pt: (row_blk + b, 0)),
        scratch_shapes=[pltpu.VMEM((A_KV_HEADS, HEAD_DIM, LANES), BF16),
                        pltpu.VMEM((A_KV_HEADS, 1, LANES), F32),
                        pltpu.VMEM((A_KV_HEADS, 1, LANES), F32),
                        pltpu.VMEM((A_KV_HEADS, HEAD_DIM, LANES), F32)],
    )
    return pl.pallas_call(
        kern,
        grid_spec=grid_spec,
        out_shape=jax.ShapeDtypeStruct(mix.shape, mix.dtype),
        input_output_aliases={5 + 2 * npg: 0},
        compiler_params=_cparams(("arbitrary", "arbitrary")),
        name="dsa_sample_attend",
    )(page_table, h, h, h, bias, *([cache_k] * npg), *([cache_v] * npg), mix)


def _moba_sample_k_kernel(pt_ref, q_ref, kn_ref, *refs, npg, n_steps, ts):
    del pt_ref
    kp = refs[:npg]
    p_ref, pn_ref, l_ref = refs[npg:npg + 3]
    qT_s, lg_s, km_s, sb_s = refs[npg + 3:]
    j = pl.program_id(1)
    ck = npg * PAGE_SIZE
    blk = MOBA_BLOCK
    bps = ck // blk
    nbk = n_steps * bps
    scale = HEAD_DIM ** -0.5

    @pl.when(j == 0)
    def _init():
        q = q_ref[...]
        for c in range(B_KV_HEADS):
            qT_s[c] = _stack_heads_t(q, c, B_GROUP, ts)
        km_s[...] = jnp.zeros_like(km_s)

    for c in range(B_KV_HEADS):
        kc = _gather_head(kp, c, B_KV_HEADS)
        means = jnp.concatenate(
            [jnp.mean(kc[n * blk:(n + 1) * blk], axis=0, keepdims=True) for n in range(bps)], axis=0)
        km_s[c, pl.ds(pl.multiple_of(j * bps, bps), bps), :] = means
        lg_s[c, pl.ds(pl.multiple_of(j * ck, ck), ck), :] = (
            jnp.dot(kc.astype(BF16), qT_s[c], preferred_element_type=F32) * scale)

    @pl.when(j == n_steps - 1)
    def _select_softmax():
        key_i = lax.broadcasted_iota(I32, (LANES, LANES), 0)
        q_of_lane = lax.broadcasted_iota(I32, (LANES, LANES), 1) % ts
        new_ok = jnp.logical_and(key_i <= q_of_lane, key_i < ts)
        for c in range(B_KV_HEADS):
            cols = slice(c * HEAD_DIM, (c + 1) * HEAD_DIM)
            gate_t = jnp.dot(km_s[c].astype(BF16), qT_s[c], preferred_element_type=F32)
            sb_s[c] = _rank_select(gate_t, nbk, axis=0)
            lg_new = jnp.dot(_pad_rows(kn_ref[:, cols], LANES).astype(BF16), qT_s[c],
                             preferred_element_type=F32) * scale
            lg_new = jnp.where(new_ok, lg_new, NEG_INF)

            def blk_logits(n, c=c):
                rows = pl.ds(pl.multiple_of(n * blk, blk), blk)
                return rows, lg_s[c, rows, :] + sb_s[c, pl.ds(n, 1), :]

            def max_body(n, m):
                _, lg = blk_logits(n)
                return jnp.maximum(m, jnp.max(lg, axis=0, keepdims=True))

            m = lax.fori_loop(0, nbk, max_body, jnp.max(lg_new, axis=0, keepdims=True))

            def exp_body(n, l, c=c, m=m):
                rows, lg = blk_logits(n)
                p = jnp.exp(lg - m)
                p_ref[c, rows, :] = p.astype(BF16)
                return l + jnp.sum(p, axis=0, keepdims=True)

            p_new = jnp.exp(lg_new - m)
            pn_ref[c] = p_new
            l_ref[c] = lax.fori_loop(0, nbk, exp_body, jnp.sum(p_new, axis=0, keepdims=True))


def _moba_sample_v_kernel(pt_ref, vn_ref, p_ref, pn_ref, l_ref, *refs, npg, n_steps, ts):
    del pt_ref
    vp = refs[:npg]
    o_ref = refs[npg + 1]
    accT_s = refs[npg + 2]
    j = pl.program_id(1)

    @pl.when(j == 0)
    def _init():
        accT_s[...] = jnp.zeros_like(accT_s)

    for c in range(B_KV_HEADS):
        vc = _gather_head(vp, c, B_KV_HEADS)
        accT_s[c] += jnp.dot(vc.T.astype(BF16), p_ref[c], preferred_element_type=F32)

    @pl.when(j == n_steps - 1)
    def _finish():
        for c in range(B_KV_HEADS):
            cols = slice(c * HEAD_DIM, (c + 1) * HEAD_DIM)
            vn = _pad_rows(vn_ref[:, cols], LANES)
            acc = accT_s[c] + jnp.dot(vn.T.astype(BF16), pn_ref[c].astype(BF16), preferred_element_type=F32)
            out = (acc / l_ref[c]).T
            for g in range(B_GROUP):
                hcol = (c * B_GROUP + g) * HEAD_DIM
                o_ref[:, hcol:hcol + HEAD_DIM] = out[g * ts:(g + 1) * ts]


def moba_sample(page_table, h, cache_k, cache_v, mix, layer, n_prompt, db, ts):
    n_pages = page_table.shape[1]
    n_past = n_pages * PAGE_SIZE
    npg = min(KV_PAGES_PER_STEP, n_pages)
    n_steps = n_pages // npg
    ck = npg * PAGE_SIZE
    row_blk = n_prompt // ts
    page_shape = (PAGE_SIZE, B_KV_HEADS, HEAD_DIM)
    k_spec = pltpu.PrefetchScalarGridSpec(
        num_scalar_prefetch=1,
        grid=(db, n_steps),
        in_specs=[pl.BlockSpec((ts, B_DIM), lambda b, j, pt: (row_blk + b, H_BQ // B_DIM)),
                  pl.BlockSpec((ts, B_KV_DIM), lambda b, j, pt: (row_blk + b, H_BK // B_KV_DIM))]
        + _page_specs(npg, page_shape, layer, lambda j: j),
        out_specs=[pl.BlockSpec((None, B_KV_HEADS, n_past, LANES), lambda b, j, pt: (b, 0, 0, 0)),
                   pl.BlockSpec((None, B_KV_HEADS, LANES, LANES), lambda b, j, pt: (b, 0, 0, 0)),
                   pl.BlockSpec((None, B_KV_HEADS, 1, LANES), lambda b, j, pt: (b, 0, 0, 0))],
        scratch_shapes=[pltpu.VMEM((B_KV_HEADS, HEAD_DIM, LANES), BF16),
                        pltpu.VMEM((B_KV_HEADS, n_past, LANES), F32),
                        pltpu.VMEM((B_KV_HEADS, LANES, HEAD_DIM), F32),
                        pltpu.VMEM((B_KV_HEADS, LANES, LANES), F32)],
    )
    p, pn, l = pl.pallas_call(
        functools.partial(_moba_sample_k_kernel, npg=npg, n_steps=n_steps, ts=ts),
        grid_spec=k_spec,
        out_shape=[jax.ShapeDtypeStruct((db, B_KV_HEADS, n_past, LANES), BF16),
                   jax.ShapeDtypeStruct((db, B_KV_HEADS, LANES, LANES), F32),
                   jax.ShapeDtypeStruct((db, B_KV_HEADS, 1, LANES), F32)],
        compiler_params=_cparams(("arbitrary", "arbitrary")),
        name="moba_sample_k",
    )(page_table, h, h, *([cache_k] * npg))
    v_spec = pltpu.PrefetchScalarGridSpec(
        num_scalar_prefetch=1,
        grid=(db, n_steps),
        in_specs=[pl.BlockSpec((ts, B_KV_DIM), lambda b, j, pt: (row_blk + b, H_BV // B_KV_DIM)),
                  pl.BlockSpec((None, B_KV_HEADS, ck, LANES), lambda b, j, pt: (b, 0, j, 0)),
                  pl.BlockSpec((None, B_KV_HEADS, LANES, LANES), lambda b, j, pt: (b, 0, 0, 0)),
                  pl.BlockSpec((None, B_KV_HEADS, 1, LANES), lambda b, j, pt: (b, 0, 0, 0))]
        + _page_specs(npg, page_shape, layer, lambda j: j)
        + [pl.BlockSpec(memory_space=pl.ANY)],
        out_specs=pl.BlockSpec((ts, B_DIM), lambda b, j, pt: (row_blk + b, A_DIM // B_DIM)),
        scratch_shapes=[pltpu.VMEM((B_KV_HEADS, HEAD_DIM, LANES), F32)],
    )
    return pl.pallas_call(
        functools.partial(_moba_sample_v_kernel, npg=npg, n_steps=n_steps, ts=ts),
        grid_spec=v_spec,
        out_shape=jax.ShapeDtypeStruct(mix.shape, mix.dtype),
        input_output_aliases={5 + npg: 0},
        compiler_params=_cparams(("arbitrary", "arbitrary")),
        name="moba_sample_v",
    )(page_table, h, p, pn, l, *([cache_v] * npg), mix)


def _row_tile(n_rows, target):
    best = None
    for t in range(16, target + 1, 16):
        if n_rows % t == 0:
            best = t
    assert best is not None, n_rows
    return best


def _pack_w_in(w_in):
    order = ("aq", "bq", "iq", "pu", "ak", "av", "bk", "bv", "ik", "iw")
    parts = [w_in[..., _SRC[n][0]:_SRC[n][0] + _SRC[n][1]] for n in order]
    used = sum(_SRC[n][1] for n in order)
    parts.append(jnp.zeros(w_in.shape[:-1] + (H_DIM - used,), w_in.dtype))
    return jnp.concatenate(parts, axis=-1).astype(BF16)


def kernel(x_prompt, x_sample, cache_a_k, cache_a_v, cache_a_idx, cache_b_k, cache_b_v, state_pool,
           cache_mem_k, cache_mem_v, page_table, mem_prompt, w_in, w_out, pool_w, pool_scale, ln1_g, ln1_b,
           w_mem_q, w_mem_k, w_mem_v, w_mem_o, ln2_g, ln2_b, w_ffn_1, w_ffn_3, w_ffn_2, ln3_g, ln3_b):
    batch, seq, d = x_prompt.shape
    db, ts, _ = x_sample.shape
    depth = w_in.shape[0]
    n_past = page_table.shape[1] * PAGE_SIZE
    n_prompt = batch * seq
    n_rows = n_prompt + db * ts
    tm = _row_tile(n_rows, 700)
    tm_p = _row_tile(n_prompt, 700)

    w_in_p = _pack_w_in(w_in)
    w_out_b = w_out.astype(BF16)
    w_mq_b = w_mem_q.astype(BF16)
    w_mkv_b = jnp.concatenate([w_mem_k, w_mem_v], axis=-1).astype(BF16)
    w_mo_b = w_mem_o.astype(BF16)
    w1_b, w3_b, w2_b = w_ffn_1.astype(BF16), w_ffn_3.astype(BF16), w_ffn_2.astype(BF16)
    pool_w_b = pool_w.astype(BF16)
    pool_scale3 = pool_scale.reshape(depth, 1, POOL_DIM)
    ln = [a.reshape(depth, 1, d) for a in (ln1_g, ln1_b, ln2_g, ln2_b, ln3_g, ln3_b)]
    mem = mem_prompt.reshape(batch * MEM_TOKENS, d)
    idx_t = jnp.swapaxes(cache_a_idx, 2, 3)

    x = jnp.concatenate([x_prompt.reshape(n_prompt, d), x_sample.reshape(db * ts, d)], axis=0)
    outs = {k: [] for k in ("pa_i", "p_pool", "pm_k", "pm_v", "sa_k", "sa_v", "sa_i", "sb_k", "sb_v", "s_pool")}
    kv_new = None
    for l in range(depth):
        h, kv_new = project_in(x, w_in_p, l, tm_p, n_prompt, kv_new)
        h = project_rows(x, w_in_p, l, h, n_prompt // (db * ts), db * ts, H_TN)
        mkv = matmul(mem, w_mkv_b, l, _row_tile(mem.shape[0], 512), MEM_DIM)

        mix = dsa_prompt(h, batch, seq, n_rows)
        mix = moba_prompt(h, mix, batch, seq)
        mix = pool_prompt(h, mix, pool_w_b, pool_scale3, l, batch, seq)
        bias = dsa_sample_select(page_table, h, idx_t, l, n_prompt, db, ts)
        mix = dsa_sample_attend(page_table, h, bias, cache_a_k, cache_a_v, mix, l, n_prompt, db, ts)
        mix = moba_sample(page_table, h, cache_b_k, cache_b_v, mix, l, n_prompt, db, ts)
        mix = pool_sample(h, mix, state_pool, pool_w_b, pool_scale3, l, n_prompt, db, ts, n_past)

        x1 = matmul_residual_ln(mix, w_out_b, x, ln[0], ln[1], l, tm, 1024)
        q = matmul(x1, w_mq_b, l, tm, MEM_DIM)
        o = memattn_prompt(q, mkv, batch, seq)
        o = memattn_sample(q, cache_mem_k, cache_mem_v, o, l, n_prompt, db, ts)
        x2 = matmul_residual_ln(o, w_mo_b, x1, ln[2], ln[3], l, tm, MEM_DIM)
        x = ffn_ln(x2, w1_b, w3_b, w2_b, ln[4], ln[5], l, tm, 512)

        hp, hs = h[:n_prompt], h[n_prompt:]

        def cols(a, off, width, lead):
            return a[:, off:off + width].reshape(lead)

        kv_s = (db, ts, A_KV_HEADS, HEAD_DIM)
        outs["pa_i"].append(cols(hp, H_IKW, IDX_DIM, (batch, seq, IDX_DIM)))
        outs["p_pool"].append(jnp.stack(
            [h[(b + 1) * seq - POOL_STATE:(b + 1) * seq, H_PU:H_PU + POOL_DIM] for b in range(batch)]))
        outs["pm_k"].append(mkv[:, :MEM_DIM].reshape(batch, MEM_TOKENS, MEM_HEADS, MEM_HEAD_DIM))
        outs["pm_v"].append(mkv[:, MEM_DIM:].reshape(batch, MEM_TOKENS, MEM_HEADS, MEM_HEAD_DIM))
        outs["sa_k"].append(cols(hs, H_AK, A_KV_DIM, kv_s))
        outs["sa_v"].append(cols(hs, H_AV, A_KV_DIM, kv_s))
        outs["sa_i"].append(cols(hs, H_IKW, IDX_DIM, (db, ts, IDX_DIM)))
        outs["sb_k"].append(cols(hs, H_BK, B_KV_DIM, kv_s))
        outs["sb_v"].append(cols(hs, H_BV, B_KV_DIM, kv_s))
        pu_s = cols(hs, H_PU, POOL_DIM, (db, ts, POOL_DIM))
        outs["s_pool"].append(jnp.concatenate([state_pool[l], pu_s], axis=1)[:, -POOL_STATE:])

    st = {k: jnp.stack(v) for k, v in outs.items()}
    for name, arr in zip(("pa_k", "pa_v", "pb_k", "pb_v"), kv_new):
        st[name] = arr.reshape(depth, batch, seq, A_KV_HEADS, HEAD_DIM)
    return (x[:n_prompt].reshape(batch, seq, d), x[n_prompt:].reshape(db, ts, d),
            st["pa_k"], st["pa_v"], st["pa_i"], st["pb_k"], st["pb_v"], st["p_pool"], st["pm_k"], st["pm_v"],
            st["sa_k"], st["sa_v"], st["sa_i"], st["sb_k"], st["sb_v"], st["s_pool"])
```

```python
import functools
import math

import jax
import jax.numpy as jnp
from jax import lax
from jax.experimental import pallas as pl
from jax.experimental.pallas import tpu as pltpu

F32 = jnp.float32
BF16 = jnp.bfloat16
I32 = jnp.int32

D_MODEL = 2048
DEPTH = 4
PAGE_SIZE = 128
HEAD_DIM = 128
A_HEADS = 6
A_KV_HEADS = 2
A_GROUP = A_HEADS // A_KV_HEADS
IDX_HEADS = 8
IDX_DIM = 64
DSA_TOPK = 256
B_HEADS = 6
B_KV_HEADS = 2
B_GROUP = B_HEADS // B_KV_HEADS
MOBA_BLOCK = 256
MOBA_TOPK = 3
POOL_WINDOWS = (2, 4, 8, 16)
POOL_GROUP_DIM = 128
POOL_DIM = len(POOL_WINDOWS) * POOL_GROUP_DIM
POOL_STATE = max(POOL_WINDOWS) - 1
A_DIM = A_HEADS * HEAD_DIM
A_KV_DIM = A_KV_HEADS * HEAD_DIM
B_DIM = B_HEADS * HEAD_DIM
B_KV_DIM = B_KV_HEADS * HEAD_DIM
MIX_DIM = A_DIM + B_DIM + POOL_DIM
MEM_TOKENS = 256
MEM_HEADS = 4
MEM_HEAD_DIM = 128
MEM_DIM = MEM_HEADS * MEM_HEAD_DIM
FFN_HIDDEN = -(-(8 * D_MODEL) // (3 * 256)) * 256
ALPHA = (2.0 * DEPTH) ** 0.25
LN_EPS = 1e-5
NEG_INF = -1e30
INT_MIN = -(2 ** 31)
SOFTMAX_C = (HEAD_DIM ** -0.5) * math.log2(math.e)

LANES = 128
SUBLANES = 8
VMEM_LIMIT_BYTES = 56 * 1024 * 1024

_SRC = dict(aq=(0, 768), ak=(768, 256), av=(1024, 256), iq=(1280, 512), ik=(1792, 64), iw=(1856, 8),
            bq=(1864, 768), bk=(2632, 256), bv=(2888, 256), pu=(3144, 512))
H_AQ, H_BQ, H_IQ, H_PU, H_AK, H_AV, H_BK, H_BV, H_IKW = 0, 768, 1536, 2048, 2560, 2816, 3072, 3328, 3584
H_DIM = 3840

DSA_TQ = 128
DSA_CK = 256
DSA_CA = 512
MOBA_TQ = 128
POOL_TT = 512
MEM_TR = 512
IDX_PAGES_PER_STEP = 64
KV_PAGES_PER_STEP = 32


def _cparams(sem):
    return pltpu.CompilerParams(dimension_semantics=sem, vmem_limit_bytes=VMEM_LIMIT_BYTES)


def _layer_norm_rows(y, g, b):
    mu = jnp.mean(y, axis=-1, keepdims=True)
    yc = y - mu
    var = jnp.mean(yc * yc, axis=-1, keepdims=True)
    return yc * lax.rsqrt(var + LN_EPS) * g + b


def _mm_kernel(x_ref, w_ref, o_ref, xb_ref):
    @pl.when(pl.program_id(1) == 0)
    def _():
        xb_ref[...] = x_ref[...].astype(BF16)

    o_ref[...] = jnp.dot(xb_ref[...], w_ref[...], preferred_element_type=F32)


def matmul(x, w, layer, tm, tn):
    m, k = x.shape
    n = w.shape[2]
    return pl.pallas_call(
        _mm_kernel,
        grid=(m // tm, n // tn),
        in_specs=[pl.BlockSpec((tm, k), lambda i, j: (i, 0)),
                  pl.BlockSpec((None, k, tn), lambda i, j: (layer, 0, j))],
        out_specs=pl.BlockSpec((tm, tn), lambda i, j: (i, j)),
        out_shape=jax.ShapeDtypeStruct((m, n), F32),
        scratch_shapes=[pltpu.VMEM((tm, k), BF16)],
        compiler_params=_cparams(("parallel", "arbitrary")),
        name="matmul",
    )(x, w)


def _mm_rows_kernel(x_ref, w_ref, o_in_ref, o_ref):
    del o_in_ref
    o_ref[...] = jnp.dot(x_ref[...].astype(BF16), w_ref[...], preferred_element_type=F32)


def project_rows(x, w, layer, out, row_block, rows, tn):
    k = x.shape[1]
    n = w.shape[2]
    return pl.pallas_call(
        _mm_rows_kernel,
        grid=(n // tn,),
        in_specs=[pl.BlockSpec((rows, k), lambda j: (row_block, 0)),
                  pl.BlockSpec((None, k, tn), lambda j: (layer, 0, j)),
                  pl.BlockSpec(memory_space=pl.ANY)],
        out_specs=pl.BlockSpec((rows, tn), lambda j: (row_block, j)),
        out_shape=jax.ShapeDtypeStruct(out.shape, out.dtype),
        input_output_aliases={2: 0},
        compiler_params=_cparams(("arbitrary",)),
        name="project_rows",
    )(x, w, out)


H_TN = 768
_KV_PIECES = (H_AK, H_AV, H_BK, H_BV)


def _store_head_rows(ref, c, rows, n_heads, value):
    ref.reshape(rows * n_heads, HEAD_DIM)[pl.ds(c, rows, stride=n_heads), :] = value


def _project_in_kernel(x_ref, w_ref, *refs, tm):
    o_ref = refs[-6]
    kv_refs = refs[-5:-1]
    xb_ref = refs[-1]
    j = pl.program_id(1)

    @pl.when(j == 0)
    def _():
        xb_ref[...] = x_ref[...].astype(BF16)

    o_ref[...] = jnp.dot(xb_ref[...], w_ref[...], preferred_element_type=F32)
    for piece, kv_ref in zip(_KV_PIECES, kv_refs):
        step, off = divmod(piece, H_TN)

        @pl.when(j == step)
        def _(kv_ref=kv_ref, off=off):
            for c in range(A_KV_HEADS):
                lo = off + c * HEAD_DIM
                _store_head_rows(kv_ref, c, tm, A_KV_HEADS, o_ref[:, lo:lo + HEAD_DIM])


def project_in(x, w, layer, tm, n_prompt, kv_prev):
    m, k = x.shape
    depth, _, n = w.shape
    kv_shape = jax.ShapeDtypeStruct((depth, n_prompt, A_KV_HEADS, HEAD_DIM), F32)
    n_prev = 0 if kv_prev is None else len(kv_prev)
    kv_spec = pl.BlockSpec((None, tm, A_KV_HEADS, HEAD_DIM), lambda i, j: (layer, i, 0, 0))
    outs = pl.pallas_call(
        functools.partial(_project_in_kernel, tm=tm),
        grid=(n_prompt // tm, n // H_TN),
        in_specs=[pl.BlockSpec((tm, k), lambda i, j: (i, 0)),
                  pl.BlockSpec((None, k, H_TN), lambda i, j: (layer, 0, j))]
        + [pl.BlockSpec(memory_space=pl.ANY)] * n_prev,
        out_specs=[pl.BlockSpec((tm, H_TN), lambda i, j: (i, j))] + [kv_spec] * len(_KV_PIECES),
        out_shape=[jax.ShapeDtypeStruct((m, n), F32)] + [kv_shape] * len(_KV_PIECES),
        input_output_aliases={2 + i: 1 + i for i in range(n_prev)},
        scratch_shapes=[pltpu.VMEM((tm, k), BF16)],
        compiler_params=_cparams(("parallel", "arbitrary")),
        name="project_in",
    )(x, w, *(kv_prev or ()))
    return outs[0], tuple(outs[1:])


def _mm_res_ln_kernel(a_ref, w_ref, x_ref, g_ref, b_ref, o_ref, acc_ref, *, nk):
    kk = pl.program_id(1)

    @pl.when(kk == 0)
    def _():
        acc_ref[...] = jnp.zeros_like(acc_ref)

    acc_ref[...] += jnp.dot(a_ref[...].astype(BF16), w_ref[...], preferred_element_type=F32)

    @pl.when(kk == nk - 1)
    def _():
        y = ALPHA * x_ref[...] + acc_ref[...]
        o_ref[...] = _layer_norm_rows(y, g_ref[...], b_ref[...])


def matmul_residual_ln(a, w, x, g, b, layer, tm, tk):
    m, k = a.shape
    n = w.shape[2]
    nk = k // tk
    return pl.pallas_call(
        functools.partial(_mm_res_ln_kernel, nk=nk),
        grid=(m // tm, nk),
        in_specs=[pl.BlockSpec((tm, tk), lambda i, j: (i, j)),
                  pl.BlockSpec((None, tk, n), lambda i, j: (layer, j, 0)),
                  pl.BlockSpec((tm, n), lambda i, j: (i, 0)),
                  pl.BlockSpec((None, 1, n), lambda i, j: (layer, 0, 0)),
                  pl.BlockSpec((None, 1, n), lambda i, j: (layer, 0, 0))],
        out_specs=pl.BlockSpec((tm, n), lambda i, j: (i, 0)),
        out_shape=jax.ShapeDtypeStruct((m, n), F32),
        scratch_shapes=[pltpu.VMEM((tm, n), F32)],
        compiler_params=_cparams(("parallel", "arbitrary")),
        name="matmul_residual_ln",
    )(a, w, x, g, b)


def _ffn_kernel(x_ref, w1_ref, w3_ref, w2_ref, g_ref, b_ref, o_ref, xb_ref, acc_ref, *, nf):
    j = pl.program_id(1)

    @pl.when(j == 0)
    def _():
        xb_ref[...] = x_ref[...].astype(BF16)
        acc_ref[...] = jnp.zeros_like(acc_ref)

    xb = xb_ref[...]
    h1 = jnp.dot(xb, w1_ref[...], preferred_element_type=F32)
    h3 = jnp.dot(xb, w3_ref[...], preferred_element_type=F32)
    h = (h1 / (1.0 + jnp.exp(-h1))) * h3
    acc_ref[...] += jnp.dot(h.astype(BF16), w2_ref[...], preferred_element_type=F32)

    @pl.when(j == nf - 1)
    def _():
        y = ALPHA * x_ref[...] + acc_ref[...]
        o_ref[...] = _layer_norm_rows(y, g_ref[...], b_ref[...])


def ffn_ln(x, w1, w3, w2, g, b, layer, tm, tf):
    m, d = x.shape
    f = w1.shape[2]
    nf = f // tf
    return pl.pallas_call(
        functools.partial(_ffn_kernel, nf=nf),
        grid=(m // tm, nf),
        in_specs=[pl.BlockSpec((tm, d), lambda i, j: (i, 0)),
                  pl.BlockSpec((None, d, tf), lambda i, j: (layer, 0, j)),
                  pl.BlockSpec((None, d, tf), lambda i, j: (layer, 0, j)),
                  pl.BlockSpec((None, tf, d), lambda i, j: (layer, j, 0)),
                  pl.BlockSpec((None, 1, d), lambda i, j: (layer, 0, 0)),
                  pl.BlockSpec((None, 1, d), lambda i, j: (layer, 0, 0))],
        out_specs=pl.BlockSpec((tm, d), lambda i, j: (i, 0)),
        out_shape=jax.ShapeDtypeStruct((m, d), F32),
        scratch_shapes=[pltpu.VMEM((tm, d), BF16), pltpu.VMEM((tm, d), F32)],
        compiler_params=_cparams(("parallel", "arbitrary")),
        name="ffn_ln",
    )(x, w1, w3, w2, g, b)


def _sortable_key(score):
    score = jnp.where(score == 0.0, 0.0, score)
    bits = pltpu.bitcast(score, I32)
    return jnp.where(bits < 0, bits ^ 0x7FFFFFFF, bits)


def _kth_largest(count_ge, shape, k_sel):
    c0 = count_ge(jnp.zeros(shape, I32))
    ok0 = c0 >= k_sel
    base = jnp.where(ok0, 0, INT_MIN).astype(I32)
    cnt = jnp.where(ok0, c0, jnp.iinfo(jnp.int32).max).astype(I32)

    def bit_body(i, carry):
        base, cnt = carry
        cand = base | jnp.left_shift(jnp.int32(1), 30 - i)
        c = count_ge(cand)
        ok = c >= k_sel
        return jnp.where(ok, cand, base), jnp.where(ok, c, cnt)

    return lax.fori_loop(0, 31, bit_body, (base, cnt))


def _flash_step_t(kc, vt, qt, bias, m, l, acc):
    lg = jnp.dot(kc, qt, preferred_element_type=F32) + bias
    m_new = jnp.maximum(m, jnp.max(lg, axis=0, keepdims=True))
    alpha = jnp.exp2(m - m_new)
    p = jnp.exp2(lg - m_new)
    l_new = alpha * l + jnp.sum(p, axis=0, keepdims=True)
    acc_new = alpha * acc + jnp.dot(vt, p.astype(BF16), preferred_element_type=F32)
    return m_new, l_new, acc_new


def _prep_kv_t(k_ref, v_ref, k_s, vt_s, rows, n_heads):
    kk = k_ref[rows, :]
    vv = v_ref[rows, :]
    for c in range(n_heads):
        cols = slice(c * HEAD_DIM, (c + 1) * HEAD_DIM)
        k_s[c, rows, :] = kk[:, cols].astype(BF16)
        vt_s[c, :, rows] = vv[:, cols].T.astype(BF16)
    return kk


def _stage_queries_t(q, qt_s, n_kv, group):
    q = q * SOFTMAX_C
    for c in range(n_kv):
        qt_s[c] = jnp.concatenate(
            [q[:, (c * group + g) * HEAD_DIM:(c * group + g + 1) * HEAD_DIM].T for g in range(group)],
            axis=1).astype(BF16)


def _head_q(qt_s, hh, tq):
    c, g = divmod(hh, A_GROUP)
    return qt_s[c, :, g * tq:(g + 1) * tq]


def _write_heads(acc_s, ls, o_ref, n_heads):
    for hh in range(n_heads):
        o_ref[:, hh * HEAD_DIM:(hh + 1) * HEAD_DIM] = (acc_s[hh] / ls[hh]).T


def _dsa_prompt_kernel(aq_ref, iq_ref, ikw_ref, ak_ref, av_ref, o_ref,
                       ik_s, k_s, vt_s, iqt_s, iwt_s, qt_s, keys_s, bias_s, acc_s, *, seq, k_sel):
    tq, ck = DSA_TQ, DSA_CK
    qi = pl.program_id(1)
    q0 = qi * tq
    prep_rows = min(512, seq)

    @pl.when(qi == 0)
    def _prep():
        lane = lax.broadcasted_iota(I32, (prep_rows, LANES), 1)

        def body(r, _):
            rows = pl.ds(pl.multiple_of(r * prep_rows, prep_rows), prep_rows)
            ik_s[rows, :] = jnp.where(lane < IDX_DIM, ikw_ref[rows, :], 0.0).astype(BF16)
            _prep_kv_t(ak_ref, av_ref, k_s, vt_s, rows, A_KV_HEADS)
            return 0

        lax.fori_loop(0, seq // prep_rows, body, 0)

    n_ck = (q0 + tq + ck - 1) // ck
    ikw_q = ikw_ref[pl.ds(pl.multiple_of(q0, tq), tq), :]
    iwt_s[...] = ikw_q.T[IDX_DIM:IDX_DIM + IDX_HEADS] * (IDX_HEADS ** -0.5)
    iqt = (iq_ref[...] * (IDX_DIM ** -0.5)).T
    zpad = jnp.zeros((LANES - IDX_DIM, 2 * tq), F32)
    for hp in range(IDX_HEADS // 2):
        pair = jnp.concatenate([iqt[(2 * hp) * IDX_DIM:(2 * hp + 1) * IDX_DIM],
                                iqt[(2 * hp + 1) * IDX_DIM:(2 * hp + 2) * IDX_DIM]], axis=1)
        iqt_s[hp] = jnp.concatenate([pair, zpad], axis=0).astype(BF16)
    _stage_queries_t(aq_ref[...], qt_s, A_KV_HEADS, A_GROUP)

    qpos = q0 + lax.broadcasted_iota(I32, (ck, tq), 1)
    krow = lax.broadcasted_iota(I32, (ck, tq), 0)

    def score_chunk(off):
        ikc = ik_s[pl.ds(off, ck), :]
        acc = jnp.zeros((ck, tq), F32)
        for hp in range(IDX_HEADS // 2):
            s = jnp.dot(ikc, iqt_s[hp], preferred_element_type=F32)
            acc = acc + (jnp.maximum(s[:, :tq], 0.0) * iwt_s[2 * hp:2 * hp + 1, :]
                         + jnp.maximum(s[:, tq:], 0.0) * iwt_s[2 * hp + 1:2 * hp + 2, :])
        keys_s[pl.ds(off, ck), :] = jnp.where(off + krow <= qpos, _sortable_key(acc), INT_MIN)

    def score_body(c, _):
        off = pl.multiple_of(c * (2 * ck), 2 * ck)
        score_chunk(off)
        score_chunk(off + ck)
        return 0

    lax.fori_loop(0, (n_ck + 1) // 2, score_body, 0)

    part = ck // 4

    def count_ge(cand):
        def body(c, acc):
            ind = jnp.where(keys_s[pl.ds(pl.multiple_of(c * ck, ck), ck), :] >= cand, 1, 0).astype(I32)
            return acc + ((ind[0:part] + ind[part:2 * part]) + (ind[2 * part:3 * part] + ind[3 * part:]))

        acc = lax.fori_loop(0, n_ck, body, jnp.zeros((part, tq), I32))
        return jnp.sum(acc, axis=0, keepdims=True)

    thr, cnt_thr = _kth_largest(count_ge, (1, tq), k_sel)
    has_tie = jnp.logical_and(cnt_thr > k_sel, thr > INT_MIN)
    any_tie = jnp.max(has_tie.astype(I32)) > 0
    thr_c = jnp.maximum(thr, INT_MIN + 1)

    @pl.when(jnp.logical_not(any_tie))
    def _fast():
        def body(c, _):
            rows = pl.ds(pl.multiple_of(c * ck, ck), ck)
            bias_s[rows, :] = jnp.where(keys_s[rows, :] >= thr_c, 0.0, NEG_INF)
            return 0

        lax.fori_loop(0, n_ck, body, 0)

    @pl.when(any_tie)
    def _ties():
        need = (k_sel - count_ge(thr_c + 1)).astype(F32)
        tri = jnp.where(lax.broadcasted_iota(I32, (LANES, LANES), 0)
                        >= lax.broadcasted_iota(I32, (LANES, LANES), 1), 1.0, 0.0).astype(BF16)

        def body(c, run):
            rows = pl.ds(pl.multiple_of(c * LANES, LANES), LANES)
            kchunk = keys_s[rows, :]
            eq = kchunk == thr_c
            eqf = jnp.where(eq, 1.0, 0.0)
            pre = jnp.dot(tri, eqf.astype(BF16), preferred_element_type=F32) + run
            sel = jnp.logical_or(kchunk > thr_c, jnp.logical_and(eq, pre <= need))
            bias_s[rows, :] = jnp.where(sel, 0.0, NEG_INF)
            return run + jnp.sum(eqf, axis=0, keepdims=True)

        lax.fori_loop(0, n_ck * (ck // LANES), body, jnp.zeros((1, tq), F32))

    ca = DSA_CA
    n_ca = (q0 + tq + ca - 1) // ca
    for extra in range(1, ca // ck):
        @pl.when(n_ck + extra <= n_ca * (ca // ck))
        def _mask_tail(extra=extra):
            bias_s[pl.ds(pl.multiple_of((n_ck + extra - 1) * ck, ck), ck), :] = jnp.full((ck, tq), NEG_INF, F32)

    acc_s[...] = jnp.zeros_like(acc_s)

    def att_body(j, carry):
        ms, ls = carry
        rows = pl.ds(pl.multiple_of(j * ca, ca), ca)
        bias = bias_s[rows, :]
        new_m, new_l = [], []
        for hh in range(A_HEADS):
            c = hh // A_GROUP
            m, l, acc = _flash_step_t(k_s[c, rows, :], vt_s[c, :, rows], _head_q(qt_s, hh, tq), bias,
                                      ms[hh], ls[hh], acc_s[hh])
            acc_s[hh] = acc
            new_m.append(m)
            new_l.append(l)
        return tuple(new_m), tuple(new_l)

    init = (tuple(jnp.full((1, tq), NEG_INF, F32) for _ in range(A_HEADS)),
            tuple(jnp.zeros((1, tq), F32) for _ in range(A_HEADS)))
    _, ls = lax.fori_loop(0, n_ca, att_body, init)
    _write_heads(acc_s, ls, o_ref, A_HEADS)


def dsa_prompt(h, batch, seq, n_rows):
    tq = DSA_TQ
    nq = seq // tq
    k_sel = min(DSA_TOPK, seq // 4)
    kern = functools.partial(_dsa_prompt_kernel, seq=seq, k_sel=k_sel)
    return pl.pallas_call(
        kern,
        grid=(batch, nq),
        in_specs=[pl.BlockSpec((tq, A_DIM), lambda b, i: (b * nq + i, H_AQ // A_DIM)),
                  pl.BlockSpec((tq, IDX_HEADS * IDX_DIM), lambda b, i: (b * nq + i, H_IQ // (IDX_HEADS * IDX_DIM))),
                  pl.BlockSpec((seq, LANES), lambda b, i: (b, H_IKW // LANES)),
                  pl.BlockSpec((seq, A_KV_DIM), lambda b, i: (b, H_AK // A_KV_DIM)),
                  pl.BlockSpec((seq, A_KV_DIM), lambda b, i: (b, H_AV // A_KV_DIM))],
        out_specs=pl.BlockSpec((tq, A_DIM), lambda b, i: (b * nq + i, 0)),
        out_shape=jax.ShapeDtypeStruct((n_rows, MIX_DIM), F32),
        scratch_shapes=[pltpu.VMEM((seq, LANES), BF16),
                        pltpu.VMEM((A_KV_HEADS, seq, HEAD_DIM), BF16),
                        pltpu.VMEM((A_KV_HEADS, HEAD_DIM, seq), BF16),
                        pltpu.VMEM((IDX_HEADS // 2, LANES, 2 * tq), BF16),
                        pltpu.VMEM((IDX_HEADS, tq), F32),
                        pltpu.VMEM((A_KV_HEADS, HEAD_DIM, A_GROUP * tq), BF16),
                        pltpu.VMEM((seq, tq), I32),
                        pltpu.VMEM((seq, tq), F32),
                        pltpu.VMEM((A_HEADS, HEAD_DIM, tq), F32)],
        compiler_params=_cparams(("arbitrary", "arbitrary")),
        name="dsa_prompt",
    )(h, h, h, h, h)


def _rank_select(gate, n_past_blocks, axis):
    idx = lax.broadcasted_iota(I32, gate.shape, axis)

    def body(m, rank):
        gm = jnp.sum(jnp.where(idx == m, gate, 0.0), axis=axis, keepdims=True)
        beats = jnp.logical_or(gm > gate, jnp.logical_and(gm == gate, m < idx))
        return rank + jnp.where(beats, 1, 0).astype(I32)

    rank = lax.fori_loop(0, n_past_blocks, body, jnp.zeros(gate.shape, I32))
    sel = jnp.logical_and(idx < n_past_blocks, rank < MOBA_TOPK)
    return jnp.where(sel, 0.0, NEG_INF)


def _moba_prompt_kernel(bq_ref, bk_ref, bv_ref, mix_ref, o_ref, k_s, vt_s, km_s, kmb_s, qt_s, sb_s, acc_s,
                        *, seq, nbp):
    del mix_ref
    tq, blk = MOBA_TQ, MOBA_BLOCK
    qi = pl.program_id(1)
    q0 = qi * tq
    nb = seq // blk

    @pl.when(qi == 0)
    def _prep():
        km_s[...] = jnp.zeros_like(km_s)

        def body(n, _):
            rows = pl.ds(pl.multiple_of(n * blk, blk), blk)
            kk = _prep_kv_t(bk_ref, bv_ref, k_s, vt_s, rows, B_KV_HEADS)
            km_s[pl.ds(n, 1), :] = jnp.mean(kk, axis=0, keepdims=True)
            return 0

        lax.fori_loop(0, nb, body, 0)
        for c in range(B_KV_HEADS):
            kmb_s[c] = km_s[:, c * HEAD_DIM:(c + 1) * HEAD_DIM].astype(BF16)

    own = q0 // blk
    own_rows = pl.ds(pl.multiple_of(own * blk, blk), blk)
    bq = bq_ref[...]
    _stage_queries_t(bq, qt_s, B_KV_HEADS, B_GROUP)
    gates = []
    for c in range(B_KV_HEADS):
        qg = jnp.concatenate(
            [bq[:, (c * B_GROUP + g) * HEAD_DIM:(c * B_GROUP + g + 1) * HEAD_DIM].T for g in range(B_GROUP)],
            axis=1).astype(BF16)
        gates.append(jnp.dot(kmb_s[c], qg, preferred_element_type=F32)[:nbp])
    sel_bias = _rank_select(jnp.concatenate(gates, axis=1), own, axis=0)
    for hh in range(B_HEADS):
        sb_s[hh] = sel_bias[:, hh * tq:(hh + 1) * tq]

    qpos = q0 + lax.broadcasted_iota(I32, (blk, tq), 1)
    kpos = own * blk + lax.broadcasted_iota(I32, (blk, tq), 0)
    causal_bias = jnp.where(kpos <= qpos, 0.0, NEG_INF)
    ms, ls = [], []
    for hh in range(B_HEADS):
        c = hh // B_GROUP
        m, l, acc = _flash_step_t(k_s[c, own_rows, :], vt_s[c, :, own_rows], _head_q(qt_s, hh, tq), causal_bias,
                                  jnp.full((1, tq), NEG_INF, F32), jnp.zeros((1, tq), F32),
                                  jnp.zeros((HEAD_DIM, tq), F32))
        acc_s[hh] = acc
        ms.append(m)
        ls.append(l)

    odd = own % 2

    def pair_body(p, carry):
        ms, ls = carry
        b0 = jnp.maximum(2 * p - odd, 0)
        rows = pl.ds(pl.multiple_of(b0 * blk, blk), 2 * blk)
        skip_second = jnp.where(jnp.logical_and(p == 0, odd == 1), NEG_INF, 0.0)
        new_m, new_l = [], []
        for hh in range(B_HEADS):
            c = hh // B_GROUP
            bias = jnp.concatenate(
                [jnp.broadcast_to(sb_s[hh, pl.ds(b0, 1), :], (blk, tq)),
                 jnp.broadcast_to(sb_s[hh, pl.ds(b0 + 1, 1), :] + skip_second, (blk, tq))], axis=0)
            m, l, acc = _flash_step_t(k_s[c, rows, :], vt_s[c, :, rows], _head_q(qt_s, hh, tq), bias,
                                      ms[hh], ls[hh], acc_s[hh])
            acc_s[hh] = acc
            new_m.append(m)
            new_l.append(l)
        return tuple(new_m), tuple(new_l)

    _, ls = lax.fori_loop(0, (own + 1) // 2, pair_body, (tuple(ms), tuple(ls)))
    _write_heads(acc_s, ls, o_ref, B_HEADS)


def moba_prompt(h, mix, batch, seq):
    tq = MOBA_TQ
    nq = seq // tq
    nbp = -(-(seq // MOBA_BLOCK) // SUBLANES) * SUBLANES
    return pl.pallas_call(
        functools.partial(_moba_prompt_kernel, seq=seq, nbp=nbp),
        grid=(batch, nq),
        in_specs=[pl.BlockSpec((tq, B_DIM), lambda b, i: (b * nq + i, H_BQ // B_DIM)),
                  pl.BlockSpec((seq, B_KV_DIM), lambda b, i: (b, H_BK // B_KV_DIM)),
                  pl.BlockSpec((seq, B_KV_DIM), lambda b, i: (b, H_BV // B_KV_DIM)),
                  pl.BlockSpec(memory_space=pl.ANY)],
        out_specs=pl.BlockSpec((tq, B_DIM), lambda b, i: (b * nq + i, A_DIM // B_DIM)),
        out_shape=jax.ShapeDtypeStruct(mix.shape, mix.dtype),
        input_output_aliases={3: 0},
        scratch_shapes=[pltpu.VMEM((B_KV_HEADS, seq, HEAD_DIM), BF16),
                        pltpu.VMEM((B_KV_HEADS, HEAD_DIM, seq), BF16),
                        pltpu.VMEM((LANES, B_KV_DIM), F32),
                        pltpu.VMEM((B_KV_HEADS, LANES, HEAD_DIM), BF16),
                        pltpu.VMEM((B_KV_HEADS, HEAD_DIM, B_GROUP * tq), BF16),
                        pltpu.VMEM((B_HEADS, nbp, tq), F32),
                        pltpu.VMEM((B_HEADS, HEAD_DIM, tq), F32)],
        compiler_params=_cparams(("arbitrary", "arbitrary")),
        name="moba_prompt",
    )(h, h, h, mix)


POOL_HALO = 16


def _pool_groups(u, window_sum, cnt_of, w_ref, sc_ref, o_ref):
    for g, w in enumerate(POOL_WINDOWS):
        cols = slice(g * POOL_GROUP_DIM, (g + 1) * POOL_GROUP_DIM)
        ug = u[:, cols]
        d = window_sum(g, w, ug) / cnt_of(w) - ug
        y = jnp.dot(d.astype(BF16), w_ref[g], preferred_element_type=F32)
        o_ref[:, cols] = y * sc_ref[:, cols]


def _pool_prompt_kernel(u_ref, halo_ref, w_ref, sc_ref, mix_ref, o_ref, ext_s):
    del mix_ref
    tt = POOL_TT
    i = pl.program_id(1)
    u = u_ref[...]
    ext_s[0:POOL_HALO, :] = jnp.where(i == 0, 0.0, halo_ref[...])
    ext_s[POOL_HALO:POOL_HALO + tt, :] = u
    t = i * tt + lax.broadcasted_iota(I32, (tt, 1), 0)

    def window_sum(g, w, ug):
        s = ug
        for k in range(1, w):
            s = s + ext_s[pl.ds(POOL_HALO - k, tt), g * POOL_GROUP_DIM:(g + 1) * POOL_GROUP_DIM]
        return s

    _pool_groups(u, window_sum, lambda w: jnp.minimum(w, t + 1).astype(F32), w_ref, sc_ref, o_ref)


def pool_prompt(h, mix, pool_w, pool_scale, layer, batch, seq):
    tt = POOL_TT
    nt = seq // tt
    halo_blocks = tt // POOL_HALO
    return pl.pallas_call(
        _pool_prompt_kernel,
        grid=(batch, nt),
        in_specs=[pl.BlockSpec((tt, POOL_DIM), lambda b, i: (b * nt + i, H_PU // POOL_DIM)),
                  pl.BlockSpec((POOL_HALO, POOL_DIM),
                               lambda b, i: (jnp.maximum((b * nt + i) * halo_blocks - 1, 0), H_PU // POOL_DIM)),
                  pl.BlockSpec((None, len(POOL_WINDOWS), POOL_GROUP_DIM, POOL_GROUP_DIM),
                               lambda b, i: (layer, 0, 0, 0)),
                  pl.BlockSpec((None, 1, POOL_DIM), lambda b, i: (layer, 0, 0)),
                  pl.BlockSpec(memory_space=pl.ANY)],
        out_specs=pl.BlockSpec((tt, POOL_DIM), lambda b, i: (b * nt + i, (A_DIM + B_DIM) // POOL_DIM)),
        out_shape=jax.ShapeDtypeStruct(mix.shape, mix.dtype),
        input_output_aliases={4: 0},
        scratch_shapes=[pltpu.VMEM((POOL_HALO + tt, POOL_DIM), F32)],
        compiler_params=_cparams(("arbitrary", "arbitrary")),
        name="pool_prompt",
    )(h, h, pool_w, pool_scale, mix)


def _pool_sample_kernel(u_ref, st_ref, w_ref, sc_ref, mix_ref, o_ref, ext_s, *, db, ts, n_past):
    del mix_ref
    u = u_ref[...]
    for b in range(db):
        ext_s[b, POOL_HALO - POOL_STATE:POOL_HALO, :] = st_ref[b]
        ext_s[b, POOL_HALO:POOL_HALO + ts, :] = u[b * ts:(b + 1) * ts]
    t = n_past + lax.broadcasted_iota(I32, (db * ts, 1), 0) % ts

    def window_sum(g, w, ug):
        parts = []
        for b in range(db):
            s = ug[b * ts:(b + 1) * ts]
            for k in range(1, w):
                s = s + ext_s[b, pl.ds(POOL_HALO - k, ts), g * POOL_GROUP_DIM:(g + 1) * POOL_GROUP_DIM]
            parts.append(s)
        return jnp.concatenate(parts, axis=0)

    _pool_groups(u, window_sum, lambda w: jnp.minimum(w, t + 1).astype(F32), w_ref, sc_ref, o_ref)


def pool_sample(h, mix, state_pool, pool_w, pool_scale, layer, n_prompt, db, ts, n_past):
    rows = db * ts
    return pl.pallas_call(
        functools.partial(_pool_sample_kernel, db=db, ts=ts, n_past=n_past),
        grid=(1,),
        in_specs=[pl.BlockSpec((rows, POOL_DIM), lambda i: (n_prompt // rows, H_PU // POOL_DIM)),
                  pl.BlockSpec((None, db, POOL_STATE, POOL_DIM), lambda i: (layer, 0, 0, 0)),
                  pl.BlockSpec((None, len(POOL_WINDOWS), POOL_GROUP_DIM, POOL_GROUP_DIM),
                               lambda i: (layer, 0, 0, 0)),
                  pl.BlockSpec((None, 1, POOL_DIM), lambda i: (layer, 0, 0)),
                  pl.BlockSpec(memory_space=pl.ANY)],
        out_specs=pl.BlockSpec((rows, POOL_DIM), lambda i: (n_prompt // rows, (A_DIM + B_DIM) // POOL_DIM)),
        out_shape=jax.ShapeDtypeStruct(mix.shape, mix.dtype),
        input_output_aliases={4: 0},
        scratch_shapes=[pltpu.VMEM((db, POOL_HALO + ts, POOL_DIM), F32)],
        compiler_params=_cparams(("arbitrary",)),
        name="pool_sample",
    )(h, state_pool, pool_w, pool_scale, mix)


def _memattn_heads(q_ref, head_k, head_v, o_ref):
    scale = MEM_HEAD_DIM ** -0.5
    for hh in range(MEM_HEADS):
        cols = slice(hh * MEM_HEAD_DIM, (hh + 1) * MEM_HEAD_DIM)
        qh = q_ref[:, cols].astype(BF16)
        lg = lax.dot_general(qh, head_k(hh).astype(BF16), (((1,), (1,)), ((), ())),
                             preferred_element_type=F32) * scale
        m = jnp.max(lg, axis=1, keepdims=True)
        p = jnp.exp(lg - m)
        l = jnp.sum(p, axis=1, keepdims=True)
        o_ref[:, cols] = jnp.dot(p.astype(BF16), head_v(hh).astype(BF16), preferred_element_type=F32) / l


def _memattn_prompt_kernel(q_ref, mk_ref, mv_ref, o_ref):
    _memattn_heads(q_ref, lambda hh: mk_ref[:, hh * MEM_HEAD_DIM:(hh + 1) * MEM_HEAD_DIM],
                   lambda hh: mv_ref[:, hh * MEM_HEAD_DIM:(hh + 1) * MEM_HEAD_DIM], o_ref)


def _memattn_sample_kernel(q_ref, mk_ref, mv_ref, o_in_ref, o_ref):
    del o_in_ref
    _memattn_heads(q_ref, lambda hh: _head_rows(mk_ref, hh, MEM_TOKENS, MEM_HEADS),
                   lambda hh: _head_rows(mv_ref, hh, MEM_TOKENS, MEM_HEADS), o_ref)


def memattn_prompt(q, mkv, batch, seq):
    tr = min(MEM_TR, seq)
    nt = seq // tr
    return pl.pallas_call(
        _memattn_prompt_kernel,
        grid=(batch, nt),
        in_specs=[pl.BlockSpec((tr, MEM_DIM), lambda b, i: (b * nt + i, 0)),
                  pl.BlockSpec((MEM_TOKENS, MEM_DIM), lambda b, i: (b, 0)),
                  pl.BlockSpec((MEM_TOKENS, MEM_DIM), lambda b, i: (b, 1))],
        out_specs=pl.BlockSpec((tr, MEM_DIM), lambda b, i: (b * nt + i, 0)),
        out_shape=jax.ShapeDtypeStruct(q.shape, F32),
        compiler_params=_cparams(("arbitrary", "arbitrary")),
        name="memattn_prompt",
    )(q, mkv, mkv)


def memattn_sample(q, cache_k, cache_v, o, layer, n_prompt, db, ts):
    cache_spec = pl.BlockSpec((None, None, MEM_TOKENS, MEM_HEADS, MEM_HEAD_DIM), lambda b: (layer, b, 0, 0, 0))
    return pl.pallas_call(
        _memattn_sample_kernel,
        grid=(db,),
        in_specs=[pl.BlockSpec((ts, MEM_DIM), lambda b: (n_prompt // ts + b, 0)),
                  cache_spec, cache_spec,
                  pl.BlockSpec(memory_space=pl.ANY)],
        out_specs=pl.BlockSpec((ts, MEM_DIM), lambda b: (n_prompt // ts + b, 0)),
        out_shape=jax.ShapeDtypeStruct(o.shape, o.dtype),
        input_output_aliases={3: 0},
        compiler_params=_cparams(("arbitrary",)),
        name="memattn_sample",
    )(q, cache_k, cache_v, o)


def _page_specs(npg, page_shape, layer, step_of):
    zeros = (0,) * len(page_shape)

    def spec(i):
        return pl.BlockSpec((None, None) + page_shape,
                            lambda b, j, pt: (layer, pt[b, step_of(j) * npg + i]) + zeros)
    return [spec(i) for i in range(npg)]


def _pad_rows(x, rows):
    return jnp.concatenate([x, jnp.zeros((rows - x.shape[0], x.shape[1]), x.dtype)], axis=0)


def _head_rows(ref, c, rows, n_heads):
    return ref.reshape(rows * n_heads, HEAD_DIM)[pl.ds(c, rows, stride=n_heads), :]


def _gather_head(pages, c, n_heads):
    return jnp.concatenate([_head_rows(p, c, PAGE_SIZE, n_heads) for p in pages], axis=0)


def _dsa_sample_select_kernel(pt_ref, iq_ref, ikw_ref, *refs, npg, n_steps, n_past, ts, k_sel):
    del pt_ref
    pages = refs[:npg]
    bias_ref, keys_s = refs[npg], refs[npg + 1]
    j = pl.program_id(1)
    ck = npg * PAGE_SIZE
    lp = n_past + LANES
    iq = iq_ref[...] * (IDX_DIM ** -0.5)
    iq_st = jnp.concatenate([iq[:, h * IDX_DIM:(h + 1) * IDX_DIM] for h in range(IDX_HEADS)],
                            axis=0).astype(BF16)
    ikw = ikw_ref[...]
    iw = ikw[:, IDX_DIM:IDX_DIM + IDX_HEADS] * (IDX_HEADS ** -0.5)

    def scores(ik_t):
        s = jnp.dot(iq_st, ik_t.astype(BF16), preferred_element_type=F32)
        acc = jnp.zeros((ts, ik_t.shape[1]), F32)
        for h in range(IDX_HEADS):
            acc = acc + jnp.maximum(s[h * ts:(h + 1) * ts], 0.0) * iw[:, h:h + 1]
        return acc

    ik_past = jnp.concatenate([pages[i][...] for i in range(npg)], axis=1)
    keys_s[:, pl.ds(pl.multiple_of(j * ck, ck), ck)] = _sortable_key(scores(ik_past))

    @pl.when(j == n_steps - 1)
    def _select():
        sc_new = scores(_pad_rows(ikw, LANES).T[:IDX_DIM])
        causal = lax.broadcasted_iota(I32, (ts, LANES), 1) <= lax.broadcasted_iota(I32, (ts, LANES), 0)
        keys_s[:, n_past:lp] = jnp.where(causal, _sortable_key(sc_new), INT_MIN)

        def count_ge(cand):
            return jnp.sum(jnp.where(keys_s[...] >= cand, 1, 0).astype(I32), axis=1, keepdims=True)

        thr, cnt_thr = _kth_largest(count_ge, (ts, 1), k_sel)
        has_tie = jnp.logical_and(cnt_thr > k_sel, thr > INT_MIN)
        any_tie = jnp.max(has_tie.astype(I32)) > 0
        thr_c = jnp.maximum(thr, INT_MIN + 1)

        @pl.when(jnp.logical_not(any_tie))
        def _fast():
            bias_ref[...] = jnp.where(keys_s[...] >= thr_c, 0.0, NEG_INF)

        @pl.when(any_tie)
        def _ties():
            need = (k_sel - count_ge(thr_c + 1)).astype(F32)
            tri = jnp.where(lax.broadcasted_iota(I32, (LANES, LANES), 0)
                            <= lax.broadcasted_iota(I32, (LANES, LANES), 1), 1.0, 0.0).astype(BF16)

            def body(c, run):
                cols = pl.ds(pl.multiple_of(c * LANES, LANES), LANES)
                kchunk = keys_s[:, cols]
                eq = kchunk == thr_c
                eqf = jnp.where(eq, 1.0, 0.0)
                pre = jnp.dot(eqf.astype(BF16), tri, preferred_element_type=F32) + run
                sel = jnp.logical_or(kchunk > thr_c, jnp.logical_and(eq, pre <= need))
                bias_ref[:, cols] = jnp.where(sel, 0.0, NEG_INF)
                return run + jnp.sum(eqf, axis=1, keepdims=True)

            lax.fori_loop(0, lp // LANES, body, jnp.zeros((ts, 1), F32))


def dsa_sample_select(page_table, h, cache_idx, layer, n_prompt, db, ts):
    n_pages = page_table.shape[1]
    n_past = n_pages * PAGE_SIZE
    npg = min(IDX_PAGES_PER_STEP, n_pages)
    n_steps = n_pages // npg
    k_sel = min(DSA_TOPK, (n_past + ts) // 4)
    lp = n_past + LANES
    kern = functools.partial(_dsa_sample_select_kernel, npg=npg, n_steps=n_steps, n_past=n_past, ts=ts, k_sel=k_sel)
    row_blk = n_prompt // ts
    grid_spec = pltpu.PrefetchScalarGridSpec(
        num_scalar_prefetch=1,
        grid=(db, n_steps),
        in_specs=[pl.BlockSpec((ts, IDX_HEADS * IDX_DIM), lambda b, j, pt: (row_blk + b, H_IQ // (IDX_HEADS * IDX_DIM))),
                  pl.BlockSpec((ts, LANES), lambda b, j, pt: (row_blk + b, H_IKW // LANES))]
        + _page_specs(npg, (IDX_DIM, PAGE_SIZE), layer, lambda j: j),
        out_specs=pl.BlockSpec((None, ts, lp), lambda b, j, pt: (b, 0, 0)),
        scratch_shapes=[pltpu.VMEM((ts, lp), I32)],
    )
    return pl.pallas_call(
        kern,
        grid_spec=grid_spec,
        out_shape=jax.ShapeDtypeStruct((db, ts, lp), F32),
        compiler_params=_cparams(("arbitrary", "arbitrary")),
        name="dsa_sample_select",
    )(page_table, h, h, *([cache_idx] * npg))


_NT = (((1,), (1,)), ((), ()))


def _sample_rows(group, ts):
    return -(-(group * ts) // 16) * 16


def _stack_heads_rows(q, c, group, ts, rows):
    return jnp.concatenate(
        [q[:, (c * group + g) * HEAD_DIM:(c * group + g + 1) * HEAD_DIM] for g in range(group)]
        + [jnp.zeros((rows - group * ts, HEAD_DIM), q.dtype)], axis=0)


def _tile_rows(bs, group, rows):
    return jnp.concatenate([bs] * group + [jnp.zeros((rows - group * bs.shape[0], bs.shape[1]), bs.dtype)], axis=0)


def _write_head_rows(out, o_ref, c, group, ts):
    for g in range(group):
        hcol = (c * group + g) * HEAD_DIM
        o_ref[:, hcol:hcol + HEAD_DIM] = out[g * ts:(g + 1) * ts]


def _dsa_sample_attend_kernel(pt_ref, q_ref, kn_ref, vn_ref, bias_ref, *refs, npg, n_steps, n_past, ts, rq):
    del pt_ref
    kp, vp = refs[:npg], refs[npg:2 * npg]
    o_ref = refs[2 * npg + 1]
    q_s, m_s, l_s, acc_s = refs[2 * npg + 2:]
    j = pl.program_id(1)
    ck = npg * PAGE_SIZE
    scale = HEAD_DIM ** -0.5

    @pl.when(j == 0)
    def _init():
        q = q_ref[...]
        for c in range(A_KV_HEADS):
            q_s[c] = _stack_heads_rows(q, c, A_GROUP, ts, rq).astype(BF16)
        m_s[...] = jnp.full(m_s.shape, NEG_INF, F32)
        l_s[...] = jnp.zeros_like(l_s)
        acc_s[...] = jnp.zeros_like(acc_s)

    def update(c, kc, vc, bias_rows):
        lg = lax.dot_general(q_s[c], kc.astype(BF16), _NT, preferred_element_type=F32) * scale + bias_rows
        m_old = m_s[c]
        m_new = jnp.maximum(m_old, jnp.max(lg, axis=1, keepdims=True))
        alpha = jnp.exp(m_old - m_new)
        p = jnp.exp(lg - m_new)
        l_s[c] = alpha * l_s[c] + jnp.sum(p, axis=1, keepdims=True)
        acc_s[c] = alpha * acc_s[c] + jnp.dot(p.astype(BF16), vc.astype(BF16), preferred_element_type=F32)
        m_s[c] = m_new

    bias_rows = _tile_rows(bias_ref[:, pl.ds(pl.multiple_of(j * ck, ck), ck)], A_GROUP, rq)
    for c in range(A_KV_HEADS):
        update(c, _gather_head(kp, c, A_KV_HEADS), _gather_head(vp, c, A_KV_HEADS), bias_rows)

    @pl.when(j == n_steps - 1)
    def _finish():
        bias_n = _tile_rows(bias_ref[:, n_past:n_past + LANES], A_GROUP, rq)
        for c in range(A_KV_HEADS):
            cols = slice(c * HEAD_DIM, (c + 1) * HEAD_DIM)
            update(c, _pad_rows(kn_ref[:, cols], LANES), _pad_rows(vn_ref[:, cols], LANES), bias_n)
            _write_head_rows(acc_s[c] / l_s[c], o_ref, c, A_GROUP, ts)


def dsa_sample_attend(page_table, h, bias, cache_k, cache_v, mix, layer, n_prompt, db, ts):
    n_pages = page_table.shape[1]
    n_past = n_pages * PAGE_SIZE
    npg = min(KV_PAGES_PER_STEP, n_pages)
    n_steps = n_pages // npg
    lp = n_past + LANES
    rq = _sample_rows(A_GROUP, ts)
    kern = functools.partial(_dsa_sample_attend_kernel, npg=npg, n_steps=n_steps, n_past=n_past, ts=ts, rq=rq)
    row_blk = n_prompt // ts
    kv_tail = (PAGE_SIZE, A_KV_HEADS, HEAD_DIM)
    grid_spec = pltpu.PrefetchScalarGridSpec(
        num_scalar_prefetch=1,
        grid=(db, n_steps),
        in_specs=[pl.BlockSpec((ts, A_DIM), lambda b, j, pt: (row_blk + b, H_AQ // A_DIM)),
                  pl.BlockSpec((ts, A_KV_DIM), lambda b, j, pt: (row_blk + b, H_AK // A_KV_DIM)),
                  pl.BlockSpec((ts, A_KV_DIM), lambda b, j, pt: (row_blk + b, H_AV // A_KV_DIM)),
                  pl.BlockSpec((None, ts, lp), lambda b, j, pt: (b, 0, 0))]
        + _page_specs(npg, kv_tail, layer, lambda j: j)
        + _page_specs(npg, kv_tail, layer, lambda j: j)
        + [pl.BlockSpec(memory_space=pl.ANY)],
        out_specs=pl.BlockSpec((ts, A_DIM), lambda b, j, pt: (row_blk + b, 0)),
        scratch_shapes=[pltpu.VMEM((A_KV_HEADS, rq, HEAD_DIM), BF16),
                        pltpu.VMEM((A_KV_HEADS, rq, 1), F32),
                        pltpu.VMEM((A_KV_HEADS, rq, 1), F32),
                        pltpu.VMEM((A_KV_HEADS, rq, HEAD_DIM), F32)],
    )
    return pl.pallas_call(
        kern,
        grid_spec=grid_spec,
        out_shape=jax.ShapeDtypeStruct(mix.shape, mix.dtype),
        input_output_aliases={5 + 2 * npg: 0},
        compiler_params=_cparams(("arbitrary", "arbitrary")),
        name="dsa_sample_attend",
    )(page_table, h, h, h, bias, *([cache_k] * npg), *([cache_v] * npg), mix)


def _moba_sample_k_kernel(pt_ref, q_ref, kn_ref, *refs, npg, n_steps, ts, rq):
    del pt_ref
    kp = refs[:npg]
    p_ref, pn_ref, l_ref = refs[npg:npg + 3]
    q_s, lg_s, km_s = refs[npg + 3:]
    j = pl.program_id(1)
    ck = npg * PAGE_SIZE
    blk = MOBA_BLOCK
    bps = ck // blk
    nbk = n_steps * bps
    scale = HEAD_DIM ** -0.5

    @pl.when(j == 0)
    def _init():
        q = q_ref[...]
        for c in range(B_KV_HEADS):
            q_s[c] = _stack_heads_rows(q, c, B_GROUP, ts, rq).astype(BF16)
        km_s[...] = jnp.zeros_like(km_s)

    for c in range(B_KV_HEADS):
        kc = _gather_head(kp, c, B_KV_HEADS)
        means = jnp.concatenate(
            [jnp.mean(kc[n * blk:(n + 1) * blk], axis=0, keepdims=True) for n in range(bps)], axis=0)
        km_s[c, pl.ds(pl.multiple_of(j * bps, bps), bps), :] = means
        lg_s[c, :, pl.ds(pl.multiple_of(j * ck, ck), ck)] = (
            lax.dot_general(q_s[c], kc.astype(BF16), _NT, preferred_element_type=F32) * scale)

    @pl.when(j == n_steps - 1)
    def _select_softmax():
        key_i = lax.broadcasted_iota(I32, (rq, LANES), 1)
        tok = lax.broadcasted_iota(I32, (rq, LANES), 0) % ts
        new_ok = jnp.logical_and(key_i <= tok, key_i < ts)
        blk_row = lax.broadcasted_iota(I32, (LANES, ck), 0)
        blk_of_key = lax.broadcasted_iota(I32, (LANES, ck), 1) // blk
        for c in range(B_KV_HEADS):
            cols = slice(c * HEAD_DIM, (c + 1) * HEAD_DIM)
            gate_t = lax.dot_general(km_s[c].astype(BF16), _pad_rows(q_s[c], LANES), _NT, preferred_element_type=F32)
            nbk8 = -(-nbk // SUBLANES) * SUBLANES
            sel_bias = _pad_rows(_rank_select(gate_t[:nbk8], nbk, axis=0), LANES).T[:rq].astype(BF16)
            lg_new = lax.dot_general(q_s[c], _pad_rows(kn_ref[:, cols], LANES).astype(BF16), _NT,
                                     preferred_element_type=F32) * scale
            lg_new = jnp.where(new_ok, lg_new, NEG_INF)
            m = jnp.max(lg_new, axis=1, keepdims=True)
            for t in range(n_steps):
                spread = jnp.where(blk_row == blk_of_key + t * bps, 1.0, 0.0).astype(BF16)
                lanes = slice(t * ck, (t + 1) * ck)
                lg = lg_s[c, :, lanes] + jnp.dot(sel_bias, spread, preferred_element_type=F32)
                lg_s[c, :, lanes] = lg
                m = jnp.maximum(m, jnp.max(lg, axis=1, keepdims=True))
            p_new = jnp.exp(lg_new - m)
            l = jnp.sum(p_new, axis=1, keepdims=True)
            for t in range(n_steps):
                lanes = slice(t * ck, (t + 1) * ck)
                p = jnp.exp(lg_s[c, :, lanes] - m)
                p_ref[c, :, lanes] = p.astype(BF16)
                l = l + jnp.sum(p, axis=1, keepdims=True)
            pn_ref[c] = p_new
            l_ref[c] = jnp.broadcast_to(l, (rq, LANES))


def _moba_sample_v_kernel(pt_ref, vn_ref, p_ref, pn_ref, l_ref, *refs, npg, n_steps, ts):
    del pt_ref
    vp = refs[:npg]
    o_ref = refs[npg + 1]
    acc_s = refs[npg + 2]
    j = pl.program_id(1)

    @pl.when(j == 0)
    def _init():
        acc_s[...] = jnp.zeros_like(acc_s)

    for c in range(B_KV_HEADS):
        vc = _gather_head(vp, c, B_KV_HEADS)
        acc_s[c] += jnp.dot(p_ref[c], vc.astype(BF16), preferred_element_type=F32)

    @pl.when(j == n_steps - 1)
    def _finish():
        for c in range(B_KV_HEADS):
            cols = slice(c * HEAD_DIM, (c + 1) * HEAD_DIM)
            vn = _pad_rows(vn_ref[:, cols], LANES)
            acc = acc_s[c] + jnp.dot(pn_ref[c].astype(BF16), vn.astype(BF16), preferred_element_type=F32)
            _write_head_rows(acc / l_ref[c][:, 0:1], o_ref, c, B_GROUP, ts)


def moba_sample(page_table, h, cache_k, cache_v, mix, layer, n_prompt, db, ts):
    n_pages = page_table.shape[1]
    n_past = n_pages * PAGE_SIZE
    npg = min(KV_PAGES_PER_STEP, n_pages)
    n_steps = n_pages // npg
    ck = npg * PAGE_SIZE
    rq = _sample_rows(B_GROUP, ts)
    row_blk = n_prompt // ts
    page_shape = (PAGE_SIZE, B_KV_HEADS, HEAD_DIM)
    k_spec = pltpu.PrefetchScalarGridSpec(
        num_scalar_prefetch=1,
        grid=(db, n_steps),
        in_specs=[pl.BlockSpec((ts, B_DIM), lambda b, j, pt: (row_blk + b, H_BQ // B_DIM)),
                  pl.BlockSpec((ts, B_KV_DIM), lambda b, j, pt: (row_blk + b, H_BK // B_KV_DIM))]
        + _page_specs(npg, page_shape, layer, lambda j: j),
        out_specs=[pl.BlockSpec((None, B_KV_HEADS, rq, n_past), lambda b, j, pt: (b, 0, 0, 0)),
                   pl.BlockSpec((None, B_KV_HEADS, rq, LANES), lambda b, j, pt: (b, 0, 0, 0)),
                   pl.BlockSpec((None, B_KV_HEADS, rq, LANES), lambda b, j, pt: (b, 0, 0, 0))],
        scratch_shapes=[pltpu.VMEM((B_KV_HEADS, rq, HEAD_DIM), BF16),
                        pltpu.VMEM((B_KV_HEADS, rq, n_past), F32),
                        pltpu.VMEM((B_KV_HEADS, LANES, HEAD_DIM), F32)],
    )
    p, pn, l = pl.pallas_call(
        functools.partial(_moba_sample_k_kernel, npg=npg, n_steps=n_steps, ts=ts, rq=rq),
        grid_spec=k_spec,
        out_shape=[jax.ShapeDtypeStruct((db, B_KV_HEADS, rq, n_past), BF16),
                   jax.ShapeDtypeStruct((db, B_KV_HEADS, rq, LANES), F32),
                   jax.ShapeDtypeStruct((db, B_KV_HEADS, rq, LANES), F32)],
        compiler_params=_cparams(("arbitrary", "arbitrary")),
        name="moba_sample_k",
    )(page_table, h, h, *([cache_k] * npg))
    v_spec = pltpu.PrefetchScalarGridSpec(
        num_scalar_prefetch=1,
        grid=(db, n_steps),
        in_specs=[pl.BlockSpec((ts, B_KV_DIM), lambda b, j, pt: (row_blk + b, H_BV // B_KV_DIM)),
                  pl.BlockSpec((None, B_KV_HEADS, rq, ck), lambda b, j, pt: (b, 0, 0, j)),
                  pl.BlockSpec((None, B_KV_HEADS, rq, LANES), lambda b, j, pt: (b, 0, 0, 0)),
                  pl.BlockSpec((None, B_KV_HEADS, rq, LANES), lambda b, j, pt: (b, 0, 0, 0))]
        + _page_specs(npg, page_shape, layer, lambda j: j)
        + [pl.BlockSpec(memory_space=pl.ANY)],
        out_specs=pl.BlockSpec((ts, B_DIM), lambda b, j, pt: (row_blk + b, A_DIM // B_DIM)),
        scratch_shapes=[pltpu.VMEM((B_KV_HEADS, rq, HEAD_DIM), F32)],
    )
    return pl.pallas_call(
        functools.partial(_moba_sample_v_kernel, npg=npg, n_steps=n_steps, ts=ts),
        grid_spec=v_spec,
        out_shape=jax.ShapeDtypeStruct(mix.shape, mix.dtype),
        input_output_aliases={5 + npg: 0},
        compiler_params=_cparams(("arbitrary", "arbitrary")),
        name="moba_sample_v",
    )(page_table, h, p, pn, l, *([cache_v] * npg), mix)


def _row_tile(n_rows, target):
    best = None
    for t in range(16, target + 1, 16):
        if n_rows % t == 0:
            best = t
    assert best is not None, n_rows
    return best


def _pack_w_in(w_in):
    order = ("aq", "bq", "iq", "pu", "ak", "av", "bk", "bv", "ik", "iw")
    parts = [w_in[..., _SRC[n][0]:_SRC[n][0] + _SRC[n][1]] for n in order]
    used = sum(_SRC[n][1] for n in order)
    parts.append(jnp.zeros(w_in.shape[:-1] + (H_DIM - used,), w_in.dtype))
    return jnp.concatenate(parts, axis=-1).astype(BF16)


def kernel(x_prompt, x_sample, cache_a_k, cache_a_v, cache_a_idx, cache_b_k, cache_b_v, state_pool,
           cache_mem_k, cache_mem_v, page_table, mem_prompt, w_in, w_out, pool_w, pool_scale, ln1_g, ln1_b,
           w_mem_q, w_mem_k, w_mem_v, w_mem_o, ln2_g, ln2_b, w_ffn_1, w_ffn_3, w_ffn_2, ln3_g, ln3_b):
    batch, seq, d = x_prompt.shape
    db, ts, _ = x_sample.shape
    depth = w_in.shape[0]
    n_past = page_table.shape[1] * PAGE_SIZE
    n_prompt = batch * seq
    n_rows = n_prompt + db * ts
    tm = _row_tile(n_rows, 700)
    tm_p = _row_tile(n_prompt, 1100)

    w_in_p = _pack_w_in(w_in)
    w_out_b = w_out.astype(BF16)
    w_mq_b = w_mem_q.astype(BF16)
    w_mkv_b = jnp.concatenate([w_mem_k, w_mem_v], axis=-1).astype(BF16)
    w_mo_b = w_mem_o.astype(BF16)
    w1_b, w3_b, w2_b = w_ffn_1.astype(BF16), w_ffn_3.astype(BF16), w_ffn_2.astype(BF16)
    pool_w_b = pool_w.astype(BF16)
    pool_scale3 = pool_scale.reshape(depth, 1, POOL_DIM)
    ln = [a.reshape(depth, 1, d) for a in (ln1_g, ln1_b, ln2_g, ln2_b, ln3_g, ln3_b)]
    mem = mem_prompt.reshape(batch * MEM_TOKENS, d)
    idx_t = jnp.swapaxes(cache_a_idx, 2, 3)

    x = jnp.concatenate([x_prompt.reshape(n_prompt, d), x_sample.reshape(db * ts, d)], axis=0)
    outs = {k: [] for k in ("pa_i", "p_pool", "pm_k", "pm_v", "sa_k", "sa_v", "sa_i", "sb_k", "sb_v", "s_pool")}
    kv_new = None
    for l in range(depth):
        h, kv_new = project_in(x, w_in_p, l, tm_p, n_prompt, kv_new)
        h = project_rows(x, w_in_p, l, h, n_prompt // (db * ts), db * ts, H_TN)
        mkv = matmul(mem, w_mkv_b, l, _row_tile(mem.shape[0], 512), MEM_DIM)

        mix = dsa_prompt(h, batch, seq, n_rows)
        mix = moba_prompt(h, mix, batch, seq)
        mix = pool_prompt(h, mix, pool_w_b, pool_scale3, l, batch, seq)
        bias = dsa_sample_select(page_table, h, idx_t, l, n_prompt, db, ts)
        mix = dsa_sample_attend(page_table, h, bias, cache_a_k, cache_a_v, mix, l, n_prompt, db, ts)
        mix = moba_sample(page_table, h, cache_b_k, cache_b_v, mix, l, n_prompt, db, ts)
        mix = pool_sample(h, mix, state_pool, pool_w_b, pool_scale3, l, n_prompt, db, ts, n_past)

        x1 = matmul_residual_ln(mix, w_out_b, x, ln[0], ln[1], l, tm, 1024)
        q = matmul(x1, w_mq_b, l, tm, MEM_DIM)
        o = memattn_prompt(q, mkv, batch, seq)
        o = memattn_sample(q, cache_mem_k, cache_mem_v, o, l, n_prompt, db, ts)
        x2 = matmul_residual_ln(o, w_mo_b, x1, ln[2], ln[3], l, tm, MEM_DIM)
        x = ffn_ln(x2, w1_b, w3_b, w2_b, ln[4], ln[5], l, tm, 512)

        hp, hs = h[:n_prompt], h[n_prompt:]

        def cols(a, off, width, lead):
            return a[:, off:off + width].reshape(lead)

        kv_s = (db, ts, A_KV_HEADS, HEAD_DIM)
        outs["pa_i"].append(cols(hp, H_IKW, IDX_DIM, (batch, seq, IDX_DIM)))
        outs["p_pool"].append(jnp.stack(
            [h[(b + 1) * seq - POOL_STATE:(b + 1) * seq, H_PU:H_PU + POOL_DIM] for b in range(batch)]))
        outs["pm_k"].append(mkv[:, :MEM_DIM].reshape(batch, MEM_TOKENS, MEM_HEADS, MEM_HEAD_DIM))
        outs["pm_v"].append(mkv[:, MEM_DIM:].reshape(batch, MEM_TOKENS, MEM_HEADS, MEM_HEAD_DIM))
        outs["sa_k"].append(cols(hs, H_AK, A_KV_DIM, kv_s))
        outs["sa_v"].append(cols(hs, H_AV, A_KV_DIM, kv_s))
        outs["sa_i"].append(cols(hs, H_IKW, IDX_DIM, (db, ts, IDX_DIM)))
        outs["sb_k"].append(cols(hs, H_BK, B_KV_DIM, kv_s))
        outs["sb_v"].append(cols(hs, H_BV, B_KV_DIM, kv_s))
        pu_s = cols(hs, H_PU, POOL_DIM, (db, ts, POOL_DIM))
        outs["s_pool"].append(jnp.concatenate([state_pool[l], pu_s], axis=1)[:, -POOL_STATE:])

    st = {k: jnp.stack(v) for k, v in outs.items()}
    for name, arr in zip(("pa_k", "pa_v", "pb_k", "pb_v"), kv_new):
        st[name] = arr.reshape(depth, batch, seq, A_KV_HEADS, HEAD_DIM)
    return (x[:n_prompt].reshape(batch, seq, d), x[n_prompt:].reshape(db, ts, d),
            st["pa_k"], st["pa_v"], st["pa_i"], st["pb_k"], st["pb_v"], st["p_pool"], st["pm_k"], st["pm_v"],
            st["sa_k"], st["sa_v"], st["sa_i"], st["sb_k"], st["sb_v"], st["s_pool"])
```

```python
import functools
import math

import jax
import jax.numpy as jnp
from jax import lax
from jax.experimental import pallas as pl
from jax.experimental.pallas import tpu as pltpu

F32 = jnp.float32
BF16 = jnp.bfloat16
I32 = jnp.int32
I16 = jnp.int16

D_MODEL = 2048
DEPTH = 4
PAGE_SIZE = 128
HEAD_DIM = 128
A_HEADS = 6
A_KV_HEADS = 2
A_GROUP = A_HEADS // A_KV_HEADS
IDX_HEADS = 8
IDX_DIM = 64
DSA_TOPK = 256
B_HEADS = 6
B_KV_HEADS = 2
B_GROUP = B_HEADS // B_KV_HEADS
MOBA_BLOCK = 256
MOBA_TOPK = 3
POOL_WINDOWS = (2, 4, 8, 16)
POOL_GROUP_DIM = 128
POOL_DIM = len(POOL_WINDOWS) * POOL_GROUP_DIM
POOL_STATE = max(POOL_WINDOWS) - 1
A_DIM = A_HEADS * HEAD_DIM
A_KV_DIM = A_KV_HEADS * HEAD_DIM
B_DIM = B_HEADS * HEAD_DIM
B_KV_DIM = B_KV_HEADS * HEAD_DIM
MIX_DIM = A_DIM + B_DIM + POOL_DIM
MEM_TOKENS = 256
MEM_HEADS = 4
MEM_HEAD_DIM = 128
MEM_DIM = MEM_HEADS * MEM_HEAD_DIM
FFN_HIDDEN = -(-(8 * D_MODEL) // (3 * 256)) * 256
ALPHA = (2.0 * DEPTH) ** 0.25
LN_EPS = 1e-5
NEG_INF = -1e30
INT_MIN = -(2 ** 31)
SOFTMAX_C = (HEAD_DIM ** -0.5) * math.log2(math.e)

LANES = 128
SUBLANES = 8
VMEM_LIMIT_BYTES = 56 * 1024 * 1024

_SRC = dict(aq=(0, 768), ak=(768, 256), av=(1024, 256), iq=(1280, 512), ik=(1792, 64), iw=(1856, 8),
            bq=(1864, 768), bk=(2632, 256), bv=(2888, 256), pu=(3144, 512))
H_AQ, H_BQ, H_IQ, H_PU, H_AK, H_AV, H_BK, H_BV, H_IKW = 0, 768, 1536, 2048, 2560, 2816, 3072, 3328, 3584
H_DIM = 3840

DSA_TQ = 128
DSA_CK = 256
DSA_CA = 512
MOBA_TQ = 128
POOL_TT = 512
MEM_TR = 512
IDX_PAGES_PER_STEP = 64
KV_PAGES_PER_STEP = 32


def _cparams(sem):
    return pltpu.CompilerParams(dimension_semantics=sem, vmem_limit_bytes=VMEM_LIMIT_BYTES)


def _layer_norm_rows(y, g, b):
    mu = jnp.mean(y, axis=-1, keepdims=True)
    yc = y - mu
    var = jnp.mean(yc * yc, axis=-1, keepdims=True)
    return yc * lax.rsqrt(var + LN_EPS) * g + b


def _mm_kernel(x_ref, w_ref, o_ref, xb_ref):
    @pl.when(pl.program_id(1) == 0)
    def _():
        xb_ref[...] = x_ref[...].astype(BF16)

    o_ref[...] = jnp.dot(xb_ref[...], w_ref[...], preferred_element_type=F32)


def matmul(x, w, layer, tm, tn):
    m, k = x.shape
    n = w.shape[2]
    return pl.pallas_call(
        _mm_kernel,
        grid=(m // tm, n // tn),
        in_specs=[pl.BlockSpec((tm, k), lambda i, j: (i, 0)),
                  pl.BlockSpec((None, k, tn), lambda i, j: (layer, 0, j))],
        out_specs=pl.BlockSpec((tm, tn), lambda i, j: (i, j)),
        out_shape=jax.ShapeDtypeStruct((m, n), F32),
        scratch_shapes=[pltpu.VMEM((tm, k), BF16)],
        compiler_params=_cparams(("parallel", "arbitrary")),
        name="matmul",
    )(x, w)


def _mm_rows_kernel(x_ref, w_ref, o_in_ref, o_ref):
    del o_in_ref
    o_ref[...] = jnp.dot(x_ref[...].astype(BF16), w_ref[...], preferred_element_type=F32)


def project_rows(x, w, layer, out, row_block, rows, tn):
    k = x.shape[1]
    n = w.shape[2]
    return pl.pallas_call(
        _mm_rows_kernel,
        grid=(n // tn,),
        in_specs=[pl.BlockSpec((rows, k), lambda j: (row_block, 0)),
                  pl.BlockSpec((None, k, tn), lambda j: (layer, 0, j)),
                  pl.BlockSpec(memory_space=pl.ANY)],
        out_specs=pl.BlockSpec((rows, tn), lambda j: (row_block, j)),
        out_shape=jax.ShapeDtypeStruct(out.shape, out.dtype),
        input_output_aliases={2: 0},
        compiler_params=_cparams(("arbitrary",)),
        name="project_rows",
    )(x, w, out)


H_TN = 768
_KV_PIECES = (H_AK, H_AV, H_BK, H_BV)


def _store_head_rows(ref, c, rows, n_heads, value):
    ref.reshape(rows * n_heads, HEAD_DIM)[pl.ds(c, rows, stride=n_heads), :] = value


def _project_in_kernel(x_ref, w_ref, *refs, tm):
    o_ref = refs[-6]
    kv_refs = refs[-5:-1]
    xb_ref = refs[-1]
    j = pl.program_id(1)

    @pl.when(j == 0)
    def _():
        xb_ref[...] = x_ref[...].astype(BF16)

    o_ref[...] = jnp.dot(xb_ref[...], w_ref[...], preferred_element_type=F32)
    for piece, kv_ref in zip(_KV_PIECES, kv_refs):
        step, off = divmod(piece, H_TN)

        @pl.when(j == step)
        def _(kv_ref=kv_ref, off=off):
            for c in range(A_KV_HEADS):
                lo = off + c * HEAD_DIM
                _store_head_rows(kv_ref, c, tm, A_KV_HEADS, o_ref[:, lo:lo + HEAD_DIM])


def project_in(x, w, layer, tm, n_prompt, kv_prev):
    m, k = x.shape
    depth, _, n = w.shape
    kv_shape = jax.ShapeDtypeStruct((depth, n_prompt, A_KV_HEADS, HEAD_DIM), F32)
    n_prev = 0 if kv_prev is None else len(kv_prev)
    kv_spec = pl.BlockSpec((None, tm, A_KV_HEADS, HEAD_DIM), lambda i, j: (layer, i, 0, 0))
    outs = pl.pallas_call(
        functools.partial(_project_in_kernel, tm=tm),
        grid=(n_prompt // tm, n // H_TN),
        in_specs=[pl.BlockSpec((tm, k), lambda i, j: (i, 0)),
                  pl.BlockSpec((None, k, H_TN), lambda i, j: (layer, 0, j))]
        + [pl.BlockSpec(memory_space=pl.ANY)] * n_prev,
        out_specs=[pl.BlockSpec((tm, H_TN), lambda i, j: (i, j))] + [kv_spec] * len(_KV_PIECES),
        out_shape=[jax.ShapeDtypeStruct((m, n), F32)] + [kv_shape] * len(_KV_PIECES),
        input_output_aliases={2 + i: 1 + i for i in range(n_prev)},
        scratch_shapes=[pltpu.VMEM((tm, k), BF16)],
        compiler_params=_cparams(("parallel", "arbitrary")),
        name="project_in",
    )(x, w, *(kv_prev or ()))
    return outs[0], tuple(outs[1:])


def _mm_res_ln_kernel(a_ref, w_ref, x_ref, g_ref, b_ref, o_ref, acc_ref, *, nk):
    kk = pl.program_id(1)

    @pl.when(kk == 0)
    def _():
        acc_ref[...] = jnp.zeros_like(acc_ref)

    acc_ref[...] += jnp.dot(a_ref[...].astype(BF16), w_ref[...], preferred_element_type=F32)

    @pl.when(kk == nk - 1)
    def _():
        y = ALPHA * x_ref[...] + acc_ref[...]
        o_ref[...] = _layer_norm_rows(y, g_ref[...], b_ref[...])


def matmul_residual_ln(a, w, x, g, b, layer, tm, tk):
    m, k = a.shape
    n = w.shape[2]
    nk = k // tk
    return pl.pallas_call(
        functools.partial(_mm_res_ln_kernel, nk=nk),
        grid=(m // tm, nk),
        in_specs=[pl.BlockSpec((tm, tk), lambda i, j: (i, j)),
                  pl.BlockSpec((None, tk, n), lambda i, j: (layer, j, 0)),
                  pl.BlockSpec((tm, n), lambda i, j: (i, 0)),
                  pl.BlockSpec((None, 1, n), lambda i, j: (layer, 0, 0)),
                  pl.BlockSpec((None, 1, n), lambda i, j: (layer, 0, 0))],
        out_specs=pl.BlockSpec((tm, n), lambda i, j: (i, 0)),
        out_shape=jax.ShapeDtypeStruct((m, n), F32),
        scratch_shapes=[pltpu.VMEM((tm, n), F32)],
        compiler_params=_cparams(("parallel", "arbitrary")),
        name="matmul_residual_ln",
    )(a, w, x, g, b)


def _ffn_kernel(x_ref, w1_ref, w3_ref, w2_ref, g_ref, b_ref, o_ref, xb_ref, acc_ref, *, nf):
    j = pl.program_id(1)

    @pl.when(j == 0)
    def _():
        xb_ref[...] = x_ref[...].astype(BF16)
        acc_ref[...] = jnp.zeros_like(acc_ref)

    xb = xb_ref[...]
    h1 = jnp.dot(xb, w1_ref[...], preferred_element_type=F32)
    h3 = jnp.dot(xb, w3_ref[...], preferred_element_type=F32)
    h = (h1 / (1.0 + jnp.exp(-h1))) * h3
    acc_ref[...] += jnp.dot(h.astype(BF16), w2_ref[...], preferred_element_type=F32)

    @pl.when(j == nf - 1)
    def _():
        y = ALPHA * x_ref[...] + acc_ref[...]
        o_ref[...] = _layer_norm_rows(y, g_ref[...], b_ref[...])


def ffn_ln(x, w1, w3, w2, g, b, layer, tm, tf):
    m, d = x.shape
    f = w1.shape[2]
    nf = f // tf
    return pl.pallas_call(
        functools.partial(_ffn_kernel, nf=nf),
        grid=(m // tm, nf),
        in_specs=[pl.BlockSpec((tm, d), lambda i, j: (i, 0)),
                  pl.BlockSpec((None, d, tf), lambda i, j: (layer, 0, j)),
                  pl.BlockSpec((None, d, tf), lambda i, j: (layer, 0, j)),
                  pl.BlockSpec((None, tf, d), lambda i, j: (layer, j, 0)),
                  pl.BlockSpec((None, 1, d), lambda i, j: (layer, 0, 0)),
                  pl.BlockSpec((None, 1, d), lambda i, j: (layer, 0, 0))],
        out_specs=pl.BlockSpec((tm, d), lambda i, j: (i, 0)),
        out_shape=jax.ShapeDtypeStruct((m, d), F32),
        scratch_shapes=[pltpu.VMEM((tm, d), BF16), pltpu.VMEM((tm, d), F32)],
        compiler_params=_cparams(("parallel", "arbitrary")),
        name="ffn_ln",
    )(x, w1, w3, w2, g, b)


def _sortable_key(score):
    score = jnp.where(score == 0.0, 0.0, score)
    bits = pltpu.bitcast(score, I32)
    return jnp.where(bits < 0, bits ^ 0x7FFFFFFF, bits)


def _kth_largest(count_ge, shape, k_sel):
    c0 = count_ge(jnp.zeros(shape, I32))
    ok0 = c0 >= k_sel
    base = jnp.where(ok0, 0, INT_MIN).astype(I32)
    cnt = jnp.where(ok0, c0, jnp.iinfo(jnp.int32).max).astype(I32)

    def bit_body(i, carry):
        base, cnt = carry
        cand = base | jnp.left_shift(jnp.int32(1), 30 - i)
        c = count_ge(cand)
        ok = c >= k_sel
        return jnp.where(ok, cand, base), jnp.where(ok, c, cnt)

    return lax.fori_loop(0, 31, bit_body, (base, cnt))


def _kth_largest16(count_ge16, shape, k_sel, cnt_none):
    c0 = count_ge16(jnp.zeros(shape, I16))
    ok0 = c0 >= k_sel
    base = jnp.where(ok0, 0, -32768).astype(I32)
    cnt = jnp.where(ok0, c0, cnt_none).astype(I32)

    def bit_body(i, carry):
        base, cnt = carry
        cand = base | jnp.left_shift(jnp.int32(1), 14 - i)
        c = count_ge16(cand.astype(I16))
        ok = c >= k_sel
        return jnp.where(ok, cand, base), jnp.where(ok, c, cnt)

    return lax.fori_loop(0, 15, bit_body, (base, cnt))


def _flash_step_t(kc, vt, qt, bias, m, l, acc):
    lg = jnp.dot(kc, qt, preferred_element_type=F32) + bias
    m_new = jnp.maximum(m, jnp.max(lg, axis=0, keepdims=True))
    alpha = jnp.exp2(m - m_new)
    p = jnp.exp2(lg - m_new)
    l_new = alpha * l + jnp.sum(p, axis=0, keepdims=True)
    acc_new = alpha * acc + jnp.dot(vt, p.astype(BF16), preferred_element_type=F32)
    return m_new, l_new, acc_new


def _prep_kv_t(k_ref, v_ref, k_s, vt_s, rows, n_heads):
    kk = k_ref[rows, :]
    vv = v_ref[rows, :]
    for c in range(n_heads):
        cols = slice(c * HEAD_DIM, (c + 1) * HEAD_DIM)
        k_s[c, rows, :] = kk[:, cols].astype(BF16)
        vt_s[c, :, rows] = vv[:, cols].T.astype(BF16)
    return kk


def _stage_queries_t(q, qt_s, n_kv, group):
    q = q * SOFTMAX_C
    for c in range(n_kv):
        qt_s[c] = jnp.concatenate(
            [q[:, (c * group + g) * HEAD_DIM:(c * group + g + 1) * HEAD_DIM].T for g in range(group)],
            axis=1).astype(BF16)


def _head_q(qt_s, hh, tq):
    c, g = divmod(hh, A_GROUP)
    return qt_s[c, :, g * tq:(g + 1) * tq]


def _write_heads(acc_s, ls, o_ref, n_heads):
    for hh in range(n_heads):
        o_ref[:, hh * HEAD_DIM:(hh + 1) * HEAD_DIM] = (acc_s[hh] / ls[hh]).T


def _dsa_prompt_kernel(aq_ref, iq_ref, ikw_ref, ak_ref, av_ref, o_ref,
                       ik_s, k_s, vt_s, iqt_s, iwt_s, qt_s, keys_s, half_s, bias_s, acc_s, *, seq, k_sel):
    tq, ck = DSA_TQ, DSA_CK
    qi = pl.program_id(1)
    q0 = qi * tq
    prep_rows = min(512, seq)

    @pl.when(qi == 0)
    def _prep():
        lane = lax.broadcasted_iota(I32, (prep_rows, LANES), 1)

        def body(r, _):
            rows = pl.ds(pl.multiple_of(r * prep_rows, prep_rows), prep_rows)
            ik_s[rows, :] = jnp.where(lane < IDX_DIM, ikw_ref[rows, :], 0.0).astype(BF16)
            _prep_kv_t(ak_ref, av_ref, k_s, vt_s, rows, A_KV_HEADS)
            return 0

        lax.fori_loop(0, seq // prep_rows, body, 0)

    n_ck = (q0 + tq + ck - 1) // ck
    ikw_q = ikw_ref[pl.ds(pl.multiple_of(q0, tq), tq), :]
    iwt_s[...] = ikw_q.T[IDX_DIM:IDX_DIM + IDX_HEADS] * (IDX_HEADS ** -0.5)
    iqt = (iq_ref[...] * (IDX_DIM ** -0.5)).T
    zpad = jnp.zeros((LANES - IDX_DIM, 2 * tq), F32)
    for hp in range(IDX_HEADS // 2):
        pair = jnp.concatenate([iqt[(2 * hp) * IDX_DIM:(2 * hp + 1) * IDX_DIM],
                                iqt[(2 * hp + 1) * IDX_DIM:(2 * hp + 2) * IDX_DIM]], axis=1)
        iqt_s[hp] = jnp.concatenate([pair, zpad], axis=0).astype(BF16)
    _stage_queries_t(aq_ref[...], qt_s, A_KV_HEADS, A_GROUP)

    qpos = q0 + lax.broadcasted_iota(I32, (ck, tq), 1)
    krow = lax.broadcasted_iota(I32, (ck, tq), 0)

    def score_chunk(off):
        ikc = ik_s[pl.ds(off, ck), :]
        acc = jnp.zeros((ck, tq), F32)
        for hp in range(IDX_HEADS // 2):
            s = jnp.dot(ikc, iqt_s[hp], preferred_element_type=F32)
            acc = acc + (jnp.maximum(s[:, :tq], 0.0) * iwt_s[2 * hp:2 * hp + 1, :]
                         + jnp.maximum(s[:, tq:], 0.0) * iwt_s[2 * hp + 1:2 * hp + 2, :])
        key = jnp.where(off + krow <= qpos, _sortable_key(acc), INT_MIN)
        keys_s[pl.ds(off, ck), :] = key
        half_s[pl.ds(off, ck), :] = jnp.right_shift(key, 16).astype(I16)

    def score_body(c, _):
        off = pl.multiple_of(c * (2 * ck), 2 * ck)
        score_chunk(off)
        score_chunk(off + ck)
        return 0

    lax.fori_loop(0, (n_ck + 1) // 2, score_body, 0)

    part = ck // 4

    def count_ge(cand):
        def body(c, acc):
            ind = jnp.where(keys_s[pl.ds(pl.multiple_of(c * ck, ck), ck), :] >= cand, 1, 0).astype(I32)
            return acc + ((ind[0:part] + ind[part:2 * part]) + (ind[2 * part:3 * part] + ind[3 * part:]))

        acc = lax.fori_loop(0, n_ck, body, jnp.zeros((part, tq), I32))
        return jnp.sum(acc, axis=0, keepdims=True)

    n_pairs = (n_ck + 1) // 2

    def count_ge16(cand):
        def body(c2, acc):
            for u in range(2):
                v = half_s[pl.ds(pl.multiple_of((2 * c2 + u) * ck, ck), ck), :]
                ind = jnp.where(v >= cand, jnp.int16(1), jnp.int16(0))
                acc = acc + ((ind[0:part] + ind[part:2 * part]) + (ind[2 * part:3 * part] + ind[3 * part:]))
            return acc

        acc = lax.fori_loop(0, n_pairs, body, jnp.zeros((part, tq), I16))
        return jnp.sum(acc.astype(I32), axis=0, keepdims=True)

    hi_t, cnt_hi = _kth_largest16(count_ge16, (1, tq), k_sel, jnp.iinfo(jnp.int32).max)

    def low_body(c, _):
        rows = pl.ds(pl.multiple_of(c * ck, ck), ck)
        key = keys_s[rows, :]
        hi = jnp.right_shift(key, 16)
        lo = (key & 0xFFFF) - 32768
        half_s[rows, :] = jnp.where(hi > hi_t, 32767, jnp.where(hi < hi_t, -32768, lo)).astype(I16)
        return 0

    lax.fori_loop(0, 2 * n_pairs, low_body, 0)
    lo_t, cnt_thr = _kth_largest16(count_ge16, (1, tq), k_sel, cnt_hi)
    thr = jnp.where(hi_t == -32768, INT_MIN, hi_t * 65536 + (lo_t + 32768))
    has_tie = jnp.logical_and(cnt_thr > k_sel, thr > INT_MIN)
    any_tie = jnp.max(has_tie.astype(I32)) > 0
    thr_c = jnp.maximum(thr, INT_MIN + 1)

    @pl.when(jnp.logical_not(any_tie))
    def _fast():
        def body(c, _):
            rows = pl.ds(pl.multiple_of(c * ck, ck), ck)
            bias_s[rows, :] = jnp.where(keys_s[rows, :] >= thr_c, 0.0, NEG_INF)
            return 0

        lax.fori_loop(0, n_ck, body, 0)

    @pl.when(any_tie)
    def _ties():
        need = (k_sel - count_ge(thr_c + 1)).astype(F32)
        tri = jnp.where(lax.broadcasted_iota(I32, (LANES, LANES), 0)
                        >= lax.broadcasted_iota(I32, (LANES, LANES), 1), 1.0, 0.0).astype(BF16)

        def body(c, run):
            rows = pl.ds(pl.multiple_of(c * LANES, LANES), LANES)
            kchunk = keys_s[rows, :]
            eq = kchunk == thr_c
            eqf = jnp.where(eq, 1.0, 0.0)
            pre = jnp.dot(tri, eqf.astype(BF16), preferred_element_type=F32) + run
            sel = jnp.logical_or(kchunk > thr_c, jnp.logical_and(eq, pre <= need))
            bias_s[rows, :] = jnp.where(sel, 0.0, NEG_INF)
            return run + jnp.sum(eqf, axis=0, keepdims=True)

        lax.fori_loop(0, n_ck * (ck // LANES), body, jnp.zeros((1, tq), F32))

    ca = DSA_CA
    n_ca = (q0 + tq + ca - 1) // ca
    for extra in range(1, ca // ck):
        @pl.when(n_ck + extra <= n_ca * (ca // ck))
        def _mask_tail(extra=extra):
            bias_s[pl.ds(pl.multiple_of((n_ck + extra - 1) * ck, ck), ck), :] = jnp.full((ck, tq), NEG_INF, F32)

    acc_s[...] = jnp.zeros_like(acc_s)

    def att_body(j, carry):
        ms, ls = carry
        rows = pl.ds(pl.multiple_of(j * ca, ca), ca)
        bias = bias_s[rows, :]
        new_m, new_l = [], []
        for hh in range(A_HEADS):
            c = hh // A_GROUP
            m, l, acc = _flash_step_t(k_s[c, rows, :], vt_s[c, :, rows], _head_q(qt_s, hh, tq), bias,
                                      ms[hh], ls[hh], acc_s[hh])
            acc_s[hh] = acc
            new_m.append(m)
            new_l.append(l)
        return tuple(new_m), tuple(new_l)

    init = (tuple(jnp.full((1, tq), NEG_INF, F32) for _ in range(A_HEADS)),
            tuple(jnp.zeros((1, tq), F32) for _ in range(A_HEADS)))
    _, ls = lax.fori_loop(0, n_ca, att_body, init)
    _write_heads(acc_s, ls, o_ref, A_HEADS)


def dsa_prompt(h, batch, seq, n_rows):
    tq = DSA_TQ
    nq = seq // tq
    k_sel = min(DSA_TOPK, seq // 4)
    kern = functools.partial(_dsa_prompt_kernel, seq=seq, k_sel=k_sel)
    return pl.pallas_call(
        kern,
        grid=(batch, nq),
        in_specs=[pl.BlockSpec((tq, A_DIM), lambda b, i: (b * nq + i, H_AQ // A_DIM)),
                  pl.BlockSpec((tq, IDX_HEADS * IDX_DIM), lambda b, i: (b * nq + i, H_IQ // (IDX_HEADS * IDX_DIM))),
                  pl.BlockSpec((seq, LANES), lambda b, i: (b, H_IKW // LANES)),
                  pl.BlockSpec((seq, A_KV_DIM), lambda b, i: (b, H_AK // A_KV_DIM)),
                  pl.BlockSpec((seq, A_KV_DIM), lambda b, i: (b, H_AV // A_KV_DIM))],
        out_specs=pl.BlockSpec((tq, A_DIM), lambda b, i: (b * nq + i, 0)),
        out_shape=jax.ShapeDtypeStruct((n_rows, MIX_DIM), F32),
        scratch_shapes=[pltpu.VMEM((seq, LANES), BF16),
                        pltpu.VMEM((A_KV_HEADS, seq, HEAD_DIM), BF16),
                        pltpu.VMEM((A_KV_HEADS, HEAD_DIM, seq), BF16),
                        pltpu.VMEM((IDX_HEADS // 2, LANES, 2 * tq), BF16),
                        pltpu.VMEM((IDX_HEADS, tq), F32),
                        pltpu.VMEM((A_KV_HEADS, HEAD_DIM, A_GROUP * tq), BF16),
                        pltpu.VMEM((seq, tq), I32),
                        pltpu.VMEM((seq, tq), I16),
                        pltpu.VMEM((seq, tq), F32),
                        pltpu.VMEM((A_HEADS, HEAD_DIM, tq), F32)],
        compiler_params=_cparams(("arbitrary", "arbitrary")),
        name="dsa_prompt",
    )(h, h, h, h, h)


def _rank_select(gate, n_past_blocks, axis):
    idx = lax.broadcasted_iota(I32, gate.shape, axis)

    def body(m, rank):
        gm = jnp.sum(jnp.where(idx == m, gate, 0.0), axis=axis, keepdims=True)
        beats = jnp.logical_or(gm > gate, jnp.logical_and(gm == gate, m < idx))
        return rank + jnp.where(beats, 1, 0).astype(I32)

    rank = lax.fori_loop(0, n_past_blocks, body, jnp.zeros(gate.shape, I32))
    sel = jnp.logical_and(idx < n_past_blocks, rank < MOBA_TOPK)
    return jnp.where(sel, 0.0, NEG_INF)


def _moba_prompt_kernel(bq_ref, bk_ref, bv_ref, mix_ref, o_ref, k_s, vt_s, km_s, kmb_s, qt_s, sb_s, acc_s,
                        *, seq, nbp):
    del mix_ref
    tq, blk = MOBA_TQ, MOBA_BLOCK
    qi = pl.program_id(1)
    q0 = qi * tq
    nb = seq // blk

    @pl.when(qi == 0)
    def _prep():
        km_s[...] = jnp.zeros_like(km_s)

        def body(n, _):
            rows = pl.ds(pl.multiple_of(n * blk, blk), blk)
            kk = _prep_kv_t(bk_ref, bv_ref, k_s, vt_s, rows, B_KV_HEADS)
            km_s[pl.ds(n, 1), :] = jnp.mean(kk, axis=0, keepdims=True)
            return 0

        lax.fori_loop(0, nb, body, 0)
        for c in range(B_KV_HEADS):
            kmb_s[c] = km_s[:, c * HEAD_DIM:(c + 1) * HEAD_DIM].astype(BF16)

    own = q0 // blk
    own_rows = pl.ds(pl.multiple_of(own * blk, blk), blk)
    bq = bq_ref[...]
    _stage_queries_t(bq, qt_s, B_KV_HEADS, B_GROUP)
    gates = []
    for c in range(B_KV_HEADS):
        qg = jnp.concatenate(
            [bq[:, (c * B_GROUP + g) * HEAD_DIM:(c * B_GROUP + g + 1) * HEAD_DIM].T for g in range(B_GROUP)],
            axis=1).astype(BF16)
        gates.append(jnp.dot(kmb_s[c], qg, preferred_element_type=F32)[:nbp])
    sel_bias = _rank_select(jnp.concatenate(gates, axis=1), own, axis=0)
    for hh in range(B_HEADS):
        sb_s[hh] = sel_bias[:, hh * tq:(hh + 1) * tq]

    qpos = q0 + lax.broadcasted_iota(I32, (blk, tq), 1)
    kpos = own * blk + lax.broadcasted_iota(I32, (blk, tq), 0)
    causal_bias = jnp.where(kpos <= qpos, 0.0, NEG_INF)
    ms, ls = [], []
    for hh in range(B_HEADS):
        c = hh // B_GROUP
        m, l, acc = _flash_step_t(k_s[c, own_rows, :], vt_s[c, :, own_rows], _head_q(qt_s, hh, tq), causal_bias,
                                  jnp.full((1, tq), NEG_INF, F32), jnp.zeros((1, tq), F32),
                                  jnp.zeros((HEAD_DIM, tq), F32))
        acc_s[hh] = acc
        ms.append(m)
        ls.append(l)

    odd = own % 2

    def pair_body(p, carry):
        ms, ls = carry
        b0 = jnp.maximum(2 * p - odd, 0)
        rows = pl.ds(pl.multiple_of(b0 * blk, blk), 2 * blk)
        skip_second = jnp.where(jnp.logical_and(p == 0, odd == 1), NEG_INF, 0.0)
        new_m, new_l = [], []
        for hh in range(B_HEADS):
            c = hh // B_GROUP
            bias = jnp.concatenate(
                [jnp.broadcast_to(sb_s[hh, pl.ds(b0, 1), :], (blk, tq)),
                 jnp.broadcast_to(sb_s[hh, pl.ds(b0 + 1, 1), :] + skip_second, (blk, tq))], axis=0)
            m, l, acc = _flash_step_t(k_s[c, rows, :], vt_s[c, :, rows], _head_q(qt_s, hh, tq), bias,
                                      ms[hh], ls[hh], acc_s[hh])
            acc_s[hh] = acc
            new_m.append(m)
            new_l.append(l)
        return tuple(new_m), tuple(new_l)

    _, ls = lax.fori_loop(0, (own + 1) // 2, pair_body, (tuple(ms), tuple(ls)))
    _write_heads(acc_s, ls, o_ref, B_HEADS)


def moba_prompt(h, mix, batch, seq):
    tq = MOBA_TQ
    nq = seq // tq
    nbp = -(-(seq // MOBA_BLOCK) // SUBLANES) * SUBLANES
    return pl.pallas_call(
        functools.partial(_moba_prompt_kernel, seq=seq, nbp=nbp),
        grid=(batch, nq),
        in_specs=[pl.BlockSpec((tq, B_DIM), lambda b, i: (b * nq + i, H_BQ // B_DIM)),
                  pl.BlockSpec((seq, B_KV_DIM), lambda b, i: (b, H_BK // B_KV_DIM)),
                  pl.BlockSpec((seq, B_KV_DIM), lambda b, i: (b, H_BV // B_KV_DIM)),
                  pl.BlockSpec(memory_space=pl.ANY)],
        out_specs=pl.BlockSpec((tq, B_DIM), lambda b, i: (b * nq + i, A_DIM // B_DIM)),
        out_shape=jax.ShapeDtypeStruct(mix.shape, mix.dtype),
        input_output_aliases={3: 0},
        scratch_shapes=[pltpu.VMEM((B_KV_HEADS, seq, HEAD_DIM), BF16),
                        pltpu.VMEM((B_KV_HEADS, HEAD_DIM, seq), BF16),
                        pltpu.VMEM((LANES, B_KV_DIM), F32),
                        pltpu.VMEM((B_KV_HEADS, LANES, HEAD_DIM), BF16),
                        pltpu.VMEM((B_KV_HEADS, HEAD_DIM, B_GROUP * tq), BF16),
                        pltpu.VMEM((B_HEADS, nbp, tq), F32),
                        pltpu.VMEM((B_HEADS, HEAD_DIM, tq), F32)],
        compiler_params=_cparams(("arbitrary", "arbitrary")),
        name="moba_prompt",
    )(h, h, h, mix)


POOL_HALO = 16


def _pool_groups(u, window_sum, cnt_of, w_ref, sc_ref, o_ref):
    for g, w in enumerate(POOL_WINDOWS):
        cols = slice(g * POOL_GROUP_DIM, (g + 1) * POOL_GROUP_DIM)
        ug = u[:, cols]
        d = window_sum(g, w, ug) / cnt_of(w) - ug
        y = jnp.dot(d.astype(BF16), w_ref[g], preferred_element_type=F32)
        o_ref[:, cols] = y * sc_ref[:, cols]


def _pool_prompt_kernel(u_ref, halo_ref, w_ref, sc_ref, mix_ref, o_ref, ext_s):
    del mix_ref
    tt = POOL_TT
    i = pl.program_id(1)
    u = u_ref[...]
    ext_s[0:POOL_HALO, :] = jnp.where(i == 0, 0.0, halo_ref[...])
    ext_s[POOL_HALO:POOL_HALO + tt, :] = u
    t = i * tt + lax.broadcasted_iota(I32, (tt, 1), 0)

    def window_sum(g, w, ug):
        s = ug
        for k in range(1, w):
            s = s + ext_s[pl.ds(POOL_HALO - k, tt), g * POOL_GROUP_DIM:(g + 1) * POOL_GROUP_DIM]
        return s

    _pool_groups(u, window_sum, lambda w: jnp.minimum(w, t + 1).astype(F32), w_ref, sc_ref, o_ref)


def pool_prompt(h, mix, pool_w, pool_scale, layer, batch, seq):
    tt = POOL_TT
    nt = seq // tt
    halo_blocks = tt // POOL_HALO
    return pl.pallas_call(
        _pool_prompt_kernel,
        grid=(batch, nt),
        in_specs=[pl.BlockSpec((tt, POOL_DIM), lambda b, i: (b * nt + i, H_PU // POOL_DIM)),
                  pl.BlockSpec((POOL_HALO, POOL_DIM),
                               lambda b, i: (jnp.maximum((b * nt + i) * halo_blocks - 1, 0), H_PU // POOL_DIM)),
                  pl.BlockSpec((None, len(POOL_WINDOWS), POOL_GROUP_DIM, POOL_GROUP_DIM),
                               lambda b, i: (layer, 0, 0, 0)),
                  pl.BlockSpec((None, 1, POOL_DIM), lambda b, i: (layer, 0, 0)),
                  pl.BlockSpec(memory_space=pl.ANY)],
        out_specs=pl.BlockSpec((tt, POOL_DIM), lambda b, i: (b * nt + i, (A_DIM + B_DIM) // POOL_DIM)),
        out_shape=jax.ShapeDtypeStruct(mix.shape, mix.dtype),
        input_output_aliases={4: 0},
        scratch_shapes=[pltpu.VMEM((POOL_HALO + tt, POOL_DIM), F32)],
        compiler_params=_cparams(("arbitrary", "arbitrary")),
        name="pool_prompt",
    )(h, h, pool_w, pool_scale, mix)


def _pool_sample_kernel(u_ref, st_ref, w_ref, sc_ref, mix_ref, o_ref, ext_s, *, db, ts, n_past):
    del mix_ref
    u = u_ref[...]
    for b in range(db):
        ext_s[b, POOL_HALO - POOL_STATE:POOL_HALO, :] = st_ref[b]
        ext_s[b, POOL_HALO:POOL_HALO + ts, :] = u[b * ts:(b + 1) * ts]
    t = n_past + lax.broadcasted_iota(I32, (db * ts, 1), 0) % ts

    def window_sum(g, w, ug):
        parts = []
        for b in range(db):
            s = ug[b * ts:(b + 1) * ts]
            for k in range(1, w):
                s = s + ext_s[b, pl.ds(POOL_HALO - k, ts), g * POOL_GROUP_DIM:(g + 1) * POOL_GROUP_DIM]
            parts.append(s)
        return jnp.concatenate(parts, axis=0)

    _pool_groups(u, window_sum, lambda w: jnp.minimum(w, t + 1).astype(F32), w_ref, sc_ref, o_ref)


def pool_sample(h, mix, state_pool, pool_w, pool_scale, layer, n_prompt, db, ts, n_past):
    rows = db * ts
    return pl.pallas_call(
        functools.partial(_pool_sample_kernel, db=db, ts=ts, n_past=n_past),
        grid=(1,),
        in_specs=[pl.BlockSpec((rows, POOL_DIM), lambda i: (n_prompt // rows, H_PU // POOL_DIM)),
                  pl.BlockSpec((None, db, POOL_STATE, POOL_DIM), lambda i: (layer, 0, 0, 0)),
                  pl.BlockSpec((None, len(POOL_WINDOWS), POOL_GROUP_DIM, POOL_GROUP_DIM),
                               lambda i: (layer, 0, 0, 0)),
                  pl.BlockSpec((None, 1, POOL_DIM), lambda i: (layer, 0, 0)),
                  pl.BlockSpec(memory_space=pl.ANY)],
        out_specs=pl.BlockSpec((rows, POOL_DIM), lambda i: (n_prompt // rows, (A_DIM + B_DIM) // POOL_DIM)),
        out_shape=jax.ShapeDtypeStruct(mix.shape, mix.dtype),
        input_output_aliases={4: 0},
        scratch_shapes=[pltpu.VMEM((db, POOL_HALO + ts, POOL_DIM), F32)],
        compiler_params=_cparams(("arbitrary",)),
        name="pool_sample",
    )(h, state_pool, pool_w, pool_scale, mix)


def _memattn_heads(q_ref, head_k, head_v, o_ref):
    scale = MEM_HEAD_DIM ** -0.5
    for hh in range(MEM_HEADS):
        cols = slice(hh * MEM_HEAD_DIM, (hh + 1) * MEM_HEAD_DIM)
        qh = q_ref[:, cols].astype(BF16)
        lg = lax.dot_general(qh, head_k(hh).astype(BF16), (((1,), (1,)), ((), ())),
                             preferred_element_type=F32) * scale
        m = jnp.max(lg, axis=1, keepdims=True)
        p = jnp.exp(lg - m)
        l = jnp.sum(p, axis=1, keepdims=True)
        o_ref[:, cols] = jnp.dot(p.astype(BF16), head_v(hh).astype(BF16), preferred_element_type=F32) / l


def _memattn_prompt_kernel(q_ref, mk_ref, mv_ref, o_ref):
    _memattn_heads(q_ref, lambda hh: mk_ref[:, hh * MEM_HEAD_DIM:(hh + 1) * MEM_HEAD_DIM],
                   lambda hh: mv_ref[:, hh * MEM_HEAD_DIM:(hh + 1) * MEM_HEAD_DIM], o_ref)


def _memattn_sample_kernel(q_ref, mk_ref, mv_ref, o_in_ref, o_ref):
    del o_in_ref
    _memattn_heads(q_ref, lambda hh: _head_rows(mk_ref, hh, MEM_TOKENS, MEM_HEADS),
                   lambda hh: _head_rows(mv_ref, hh, MEM_TOKENS, MEM_HEADS), o_ref)


def memattn_prompt(q, mkv, batch, seq):
    tr = min(MEM_TR, seq)
    nt = seq // tr
    return pl.pallas_call(
        _memattn_prompt_kernel,
        grid=(batch, nt),
        in_specs=[pl.BlockSpec((tr, MEM_DIM), lambda b, i: (b * nt + i, 0)),
                  pl.BlockSpec((MEM_TOKENS, MEM_DIM), lambda b, i: (b, 0)),
                  pl.BlockSpec((MEM_TOKENS, MEM_DIM), lambda b, i: (b, 1))],
        out_specs=pl.BlockSpec((tr, MEM_DIM), lambda b, i: (b * nt + i, 0)),
        out_shape=jax.ShapeDtypeStruct(q.shape, F32),
        compiler_params=_cparams(("arbitrary", "arbitrary")),
        name="memattn_prompt",
    )(q, mkv, mkv)


def memattn_sample(q, cache_k, cache_v, o, layer, n_prompt, db, ts):
    cache_spec = pl.BlockSpec((None, None, MEM_TOKENS, MEM_HEADS, MEM_HEAD_DIM), lambda b: (layer, b, 0, 0, 0))
    return pl.pallas_call(
        _memattn_sample_kernel,
        grid=(db,),
        in_specs=[pl.BlockSpec((ts, MEM_DIM), lambda b: (n_prompt // ts + b, 0)),
                  cache_spec, cache_spec,
                  pl.BlockSpec(memory_space=pl.ANY)],
        out_specs=pl.BlockSpec((ts, MEM_DIM), lambda b: (n_prompt // ts + b, 0)),
        out_shape=jax.ShapeDtypeStruct(o.shape, o.dtype),
        input_output_aliases={3: 0},
        compiler_params=_cparams(("arbitrary",)),
        name="memattn_sample",
    )(q, cache_k, cache_v, o)


def _page_specs(npg, page_shape, layer, step_of):
    zeros = (0,) * len(page_shape)

    def spec(i):
        return pl.BlockSpec((None, None) + page_shape,
                            lambda b, j, pt: (layer, pt[b, step_of(j) * npg + i]) + zeros)
    return [spec(i) for i in range(npg)]


def _pad_rows(x, rows):
    return jnp.concatenate([x, jnp.zeros((rows - x.shape[0], x.shape[1]), x.dtype)], axis=0)


def _head_rows(ref, c, rows, n_heads):
    return ref.reshape(rows * n_heads, HEAD_DIM)[pl.ds(c, rows, stride=n_heads), :]


def _gather_head(pages, c, n_heads):
    return jnp.concatenate([_head_rows(p, c, PAGE_SIZE, n_heads) for p in pages], axis=0)


def _dsa_sample_select_kernel(pt_ref, iq_ref, ikw_ref, *refs, npg, n_steps, n_past, ts, k_sel):
    del pt_ref
    pages = refs[:npg]
    bias_ref, keys_s = refs[npg], refs[npg + 1]
    j = pl.program_id(1)
    ck = npg * PAGE_SIZE
    lp = n_past + LANES
    iq = iq_ref[...] * (IDX_DIM ** -0.5)
    iq_st = jnp.concatenate([iq[:, h * IDX_DIM:(h + 1) * IDX_DIM] for h in range(IDX_HEADS)],
                            axis=0).astype(BF16)
    ikw = ikw_ref[...]
    iw = ikw[:, IDX_DIM:IDX_DIM + IDX_HEADS] * (IDX_HEADS ** -0.5)

    def scores(ik_t):
        s = jnp.dot(iq_st, ik_t.astype(BF16), preferred_element_type=F32)
        acc = jnp.zeros((ts, ik_t.shape[1]), F32)
        for h in range(IDX_HEADS):
            acc = acc + jnp.maximum(s[h * ts:(h + 1) * ts], 0.0) * iw[:, h:h + 1]
        return acc

    ik_past = jnp.concatenate([pages[i][...] for i in range(npg)], axis=1)
    keys_s[:, pl.ds(pl.multiple_of(j * ck, ck), ck)] = _sortable_key(scores(ik_past))

    @pl.when(j == n_steps - 1)
    def _select():
        sc_new = scores(_pad_rows(ikw, LANES).T[:IDX_DIM])
        causal = lax.broadcasted_iota(I32, (ts, LANES), 1) <= lax.broadcasted_iota(I32, (ts, LANES), 0)
        keys_s[:, n_past:lp] = jnp.where(causal, _sortable_key(sc_new), INT_MIN)

        def count_ge(cand):
            return jnp.sum(jnp.where(keys_s[...] >= cand, 1, 0).astype(I32), axis=1, keepdims=True)

        thr, cnt_thr = _kth_largest(count_ge, (ts, 1), k_sel)
        has_tie = jnp.logical_and(cnt_thr > k_sel, thr > INT_MIN)
        any_tie = jnp.max(has_tie.astype(I32)) > 0
        thr_c = jnp.maximum(thr, INT_MIN + 1)

        @pl.when(jnp.logical_not(any_tie))
        def _fast():
            bias_ref[...] = jnp.where(keys_s[...] >= thr_c, 0.0, NEG_INF)

        @pl.when(any_tie)
        def _ties():
            need = (k_sel - count_ge(thr_c + 1)).astype(F32)
            tri = jnp.where(lax.broadcasted_iota(I32, (LANES, LANES), 0)
                            <= lax.broadcasted_iota(I32, (LANES, LANES), 1), 1.0, 0.0).astype(BF16)

            def body(c, run):
                cols = pl.ds(pl.multiple_of(c * LANES, LANES), LANES)
                kchunk = keys_s[:, cols]
                eq = kchunk == thr_c
                eqf = jnp.where(eq, 1.0, 0.0)
                pre = jnp.dot(eqf.astype(BF16), tri, preferred_element_type=F32) + run
                sel = jnp.logical_or(kchunk > thr_c, jnp.logical_and(eq, pre <= need))
                bias_ref[:, cols] = jnp.where(sel, 0.0, NEG_INF)
                return run + jnp.sum(eqf, axis=1, keepdims=True)

            lax.fori_loop(0, lp // LANES, body, jnp.zeros((ts, 1), F32))


def dsa_sample_select(page_table, h, cache_idx, layer, n_prompt, db, ts):
    n_pages = page_table.shape[1]
    n_past = n_pages * PAGE_SIZE
    npg = min(IDX_PAGES_PER_STEP, n_pages)
    n_steps = n_pages // npg
    k_sel = min(DSA_TOPK, (n_past + ts) // 4)
    lp = n_past + LANES
    kern = functools.partial(_dsa_sample_select_kernel, npg=npg, n_steps=n_steps, n_past=n_past, ts=ts, k_sel=k_sel)
    row_blk = n_prompt // ts
    grid_spec = pltpu.PrefetchScalarGridSpec(
        num_scalar_prefetch=1,
        grid=(db, n_steps),
        in_specs=[pl.BlockSpec((ts, IDX_HEADS * IDX_DIM), lambda b, j, pt: (row_blk + b, H_IQ // (IDX_HEADS * IDX_DIM))),
                  pl.BlockSpec((ts, LANES), lambda b, j, pt: (row_blk + b, H_IKW // LANES))]
        + _page_specs(npg, (IDX_DIM, PAGE_SIZE), layer, lambda j: j),
        out_specs=pl.BlockSpec((None, ts, lp), lambda b, j, pt: (b, 0, 0)),
        scratch_shapes=[pltpu.VMEM((ts, lp), I32)],
    )
    return pl.pallas_call(
        kern,
        grid_spec=grid_spec,
        out_shape=jax.ShapeDtypeStruct((db, ts, lp), F32),
        compiler_params=_cparams(("arbitrary", "arbitrary")),
        name="dsa_sample_select",
    )(page_table, h, h, *([cache_idx] * npg))


_NT = (((1,), (1,)), ((), ()))


def _sample_rows(group, ts):
    return -(-(group * ts) // 16) * 16


def _stack_heads_rows(q, c, group, ts, rows):
    return jnp.concatenate(
        [q[:, (c * group + g) * HEAD_DIM:(c * group + g + 1) * HEAD_DIM] for g in range(group)]
        + [jnp.zeros((rows - group * ts, HEAD_DIM), q.dtype)], axis=0)


def _tile_rows(bs, group, rows):
    return jnp.concatenate([bs] * group + [jnp.zeros((rows - group * bs.shape[0], bs.shape[1]), bs.dtype)], axis=0)


def _write_head_rows(out, o_ref, c, group, ts):
    for g in range(group):
        hcol = (c * group + g) * HEAD_DIM
        o_ref[:, hcol:hcol + HEAD_DIM] = out[g * ts:(g + 1) * ts]


def _dsa_sample_attend_kernel(pt_ref, q_ref, kn_ref, vn_ref, bias_ref, *refs, npg, n_steps, n_past, ts, rq):
    del pt_ref
    kp, vp = refs[:npg], refs[npg:2 * npg]
    o_ref = refs[2 * npg + 1]
    q_s, m_s, l_s, acc_s = refs[2 * npg + 2:]
    j = pl.program_id(1)
    ck = npg * PAGE_SIZE
    scale = HEAD_DIM ** -0.5

    @pl.when(j == 0)
    def _init():
        q = q_ref[...]
        for c in range(A_KV_HEADS):
            q_s[c] = _stack_heads_rows(q, c, A_GROUP, ts, rq).astype(BF16)
        m_s[...] = jnp.full(m_s.shape, NEG_INF, F32)
        l_s[...] = jnp.zeros_like(l_s)
        acc_s[...] = jnp.zeros_like(acc_s)

    def update(c, kc, vc, bias_rows):
        lg = lax.dot_general(q_s[c], kc.astype(BF16), _NT, preferred_element_type=F32) * scale + bias_rows
        m_old = m_s[c]
        m_new = jnp.maximum(m_old, jnp.max(lg, axis=1, keepdims=True))
        alpha = jnp.exp(m_old - m_new)
        p = jnp.exp(lg - m_new)
        l_s[c] = alpha * l_s[c] + jnp.sum(p, axis=1, keepdims=True)
        acc_s[c] = alpha * acc_s[c] + jnp.dot(p.astype(BF16), vc.astype(BF16), preferred_element_type=F32)
        m_s[c] = m_new

    bias_rows = _tile_rows(bias_ref[:, pl.ds(pl.multiple_of(j * ck, ck), ck)], A_GROUP, rq)
    for c in range(A_KV_HEADS):
        update(c, _gather_head(kp, c, A_KV_HEADS), _gather_head(vp, c, A_KV_HEADS), bias_rows)

    @pl.when(j == n_steps - 1)
    def _finish():
        bias_n = _tile_rows(bias_ref[:, n_past:n_past + LANES], A_GROUP, rq)
        for c in range(A_KV_HEADS):
            cols = slice(c * HEAD_DIM, (c + 1) * HEAD_DIM)
            update(c, _pad_rows(kn_ref[:, cols], LANES), _pad_rows(vn_ref[:, cols], LANES), bias_n)
            _write_head_rows(acc_s[c] / l_s[c], o_ref, c, A_GROUP, ts)


def dsa_sample_attend(page_table, h, bias, cache_k, cache_v, mix, layer, n_prompt, db, ts):
    n_pages = page_table.shape[1]
    n_past = n_pages * PAGE_SIZE
    npg = min(KV_PAGES_PER_STEP, n_pages)
    n_steps = n_pages // npg
    lp = n_past + LANES
    rq = _sample_rows(A_GROUP, ts)
    kern = functools.partial(_dsa_sample_attend_kernel, npg=npg, n_steps=n_steps, n_past=n_past, ts=ts, rq=rq)
    row_blk = n_prompt // ts
    kv_tail = (PAGE_SIZE, A_KV_HEADS, HEAD_DIM)
    grid_spec = pltpu.PrefetchScalarGridSpec(
        num_scalar_prefetch=1,
        grid=(db, n_steps),
        in_specs=[pl.BlockSpec((ts, A_DIM), lambda b, j, pt: (row_blk + b, H_AQ // A_DIM)),
                  pl.BlockSpec((ts, A_KV_DIM), lambda b, j, pt: (row_blk + b, H_AK // A_KV_DIM)),
                  pl.BlockSpec((ts, A_KV_DIM), lambda b, j, pt: (row_blk + b, H_AV // A_KV_DIM)),
                  pl.BlockSpec((None, ts, lp), lambda b, j, pt: (b, 0, 0))]
        + _page_specs(npg, kv_tail, layer, lambda j: j)
        + _page_specs(npg, kv_tail, layer, lambda j: j)
        + [pl.BlockSpec(memory_space=pl.ANY)],
        out_specs=pl.BlockSpec((ts, A_DIM), lambda b, j, pt: (row_blk + b, 0)),
        scratch_shapes=[pltpu.VMEM((A_KV_HEADS, rq, HEAD_DIM), BF16),
                        pltpu.VMEM((A_KV_HEADS, rq, 1), F32),
                        pltpu.VMEM((A_KV_HEADS, rq, 1), F32),
                        pltpu.VMEM((A_KV_HEADS, rq, HEAD_DIM), F32)],
    )
    return pl.pallas_call(
        kern,
        grid_spec=grid_spec,
        out_shape=jax.ShapeDtypeStruct(mix.shape, mix.dtype),
        input_output_aliases={5 + 2 * npg: 0},
        compiler_params=_cparams(("arbitrary", "arbitrary")),
        name="dsa_sample_attend",
    )(page_table, h, h, h, bias, *([cache_k] * npg), *([cache_v] * npg), mix)


def _moba_sample_k_kernel(pt_ref, q_ref, kn_ref, *refs, npg, n_steps, ts, rq):
    del pt_ref
    kp = refs[:npg]
    p_ref, pn_ref, l_ref = refs[npg:npg + 3]
    q_s, lg_s, km_s = refs[npg + 3:]
    j = pl.program_id(1)
    ck = npg * PAGE_SIZE
    blk = MOBA_BLOCK
    bps = ck // blk
    nbk = n_steps * bps
    scale = HEAD_DIM ** -0.5

    @pl.when(j == 0)
    def _init():
        q = q_ref[...]
        for c in range(B_KV_HEADS):
            q_s[c] = _stack_heads_rows(q, c, B_GROUP, ts, rq).astype(BF16)
        km_s[...] = jnp.zeros_like(km_s)

    for c in range(B_KV_HEADS):
        kc = _gather_head(kp, c, B_KV_HEADS)
        means = jnp.concatenate(
            [jnp.mean(kc[n * blk:(n + 1) * blk], axis=0, keepdims=True) for n in range(bps)], axis=0)
        km_s[c, pl.ds(pl.multiple_of(j * bps, bps), bps), :] = means
        lg_s[c, :, pl.ds(pl.multiple_of(j * ck, ck), ck)] = (
            lax.dot_general(q_s[c], kc.astype(BF16), _NT, preferred_element_type=F32) * scale)

    @pl.when(j == n_steps - 1)
    def _select_softmax():
        key_i = lax.broadcasted_iota(I32, (rq, LANES), 1)
        tok = lax.broadcasted_iota(I32, (rq, LANES), 0) % ts
        new_ok = jnp.logical_and(key_i <= tok, key_i < ts)
        blk_row = lax.broadcasted_iota(I32, (LANES, ck), 0)
        blk_of_key = lax.broadcasted_iota(I32, (LANES, ck), 1) // blk
        for c in range(B_KV_HEADS):
            cols = slice(c * HEAD_DIM, (c + 1) * HEAD_DIM)
            gate_t = lax.dot_general(km_s[c].astype(BF16), _pad_rows(q_s[c], LANES), _NT, preferred_element_type=F32)
            nbk8 = -(-nbk // SUBLANES) * SUBLANES
            sel_bias = _pad_rows(_rank_select(gate_t[:nbk8], nbk, axis=0), LANES).T[:rq].astype(BF16)
            lg_new = lax.dot_general(q_s[c], _pad_rows(kn_ref[:, cols], LANES).astype(BF16), _NT,
                                     preferred_element_type=F32) * scale
            lg_new = jnp.where(new_ok, lg_new, NEG_INF)
            m = jnp.max(lg_new, axis=1, keepdims=True)
            for t in range(n_steps):
                spread = jnp.where(blk_row == blk_of_key + t * bps, 1.0, 0.0).astype(BF16)
                lanes = slice(t * ck, (t + 1) * ck)
                lg = lg_s[c, :, lanes] + jnp.dot(sel_bias, spread, preferred_element_type=F32)
                lg_s[c, :, lanes] = lg
                m = jnp.maximum(m, jnp.max(lg, axis=1, keepdims=True))
            p_new = jnp.exp(lg_new - m)
            l = jnp.sum(p_new, axis=1, keepdims=True)
            for t in range(n_steps):
                lanes = slice(t * ck, (t + 1) * ck)
                p = jnp.exp(lg_s[c, :, lanes] - m)
                p_ref[c, :, lanes] = p.astype(BF16)
                l = l + jnp.sum(p, axis=1, keepdims=True)
            pn_ref[c] = p_new
            l_ref[c] = jnp.broadcast_to(l, (rq, LANES))


def _moba_sample_v_kernel(pt_ref, vn_ref, p_ref, pn_ref, l_ref, *refs, npg, n_steps, ts):
    del pt_ref
    vp = refs[:npg]
    o_ref = refs[npg + 1]
    acc_s = refs[npg + 2]
    j = pl.program_id(1)

    @pl.when(j == 0)
    def _init():
        acc_s[...] = jnp.zeros_like(acc_s)

    for c in range(B_KV_HEADS):
        vc = _gather_head(vp, c, B_KV_HEADS)
        acc_s[c] += jnp.dot(p_ref[c], vc.astype(BF16), preferred_element_type=F32)

    @pl.when(j == n_steps - 1)
    def _finish():
        for c in range(B_KV_HEADS):
            cols = slice(c * HEAD_DIM, (c + 1) * HEAD_DIM)
            vn = _pad_rows(vn_ref[:, cols], LANES)
            acc = acc_s[c] + jnp.dot(pn_ref[c].astype(BF16), vn.astype(BF16), preferred_element_type=F32)
            _write_head_rows(acc / l_ref[c][:, 0:1], o_ref, c, B_GROUP, ts)


def moba_sample(page_table, h, cache_k, cache_v, mix, layer, n_prompt, db, ts):
    n_pages = page_table.shape[1]
    n_past = n_pages * PAGE_SIZE
    npg = min(KV_PAGES_PER_STEP, n_pages)
    n_steps = n_pages // npg
    ck = npg * PAGE_SIZE
    rq = _sample_rows(B_GROUP, ts)
    row_blk = n_prompt // ts
    page_shape = (PAGE_SIZE, B_KV_HEADS, HEAD_DIM)
    k_spec = pltpu.PrefetchScalarGridSpec(
        num_scalar_prefetch=1,
        grid=(db, n_steps),
        in_specs=[pl.BlockSpec((ts, B_DIM), lambda b, j, pt: (row_blk + b, H_BQ // B_DIM)),
                  pl.BlockSpec((ts, B_KV_DIM), lambda b, j, pt: (row_blk + b, H_BK // B_KV_DIM))]
        + _page_specs(npg, page_shape, layer, lambda j: j),
        out_specs=[pl.BlockSpec((None, B_KV_HEADS, rq, n_past), lambda b, j, pt: (b, 0, 0, 0)),
                   pl.BlockSpec((None, B_KV_HEADS, rq, LANES), lambda b, j, pt: (b, 0, 0, 0)),
                   pl.BlockSpec((None, B_KV_HEADS, rq, LANES), lambda b, j, pt: (b, 0, 0, 0))],
        scratch_shapes=[pltpu.VMEM((B_KV_HEADS, rq, HEAD_DIM), BF16),
                        pltpu.VMEM((B_KV_HEADS, rq, n_past), F32),
                        pltpu.VMEM((B_KV_HEADS, LANES, HEAD_DIM), F32)],
    )
    p, pn, l = pl.pallas_call(
        functools.partial(_moba_sample_k_kernel, npg=npg, n_steps=n_steps, ts=ts, rq=rq),
        grid_spec=k_spec,
        out_shape=[jax.ShapeDtypeStruct((db, B_KV_HEADS, rq, n_past), BF16),
                   jax.ShapeDtypeStruct((db, B_KV_HEADS, rq, LANES), F32),
                   jax.ShapeDtypeStruct((db, B_KV_HEADS, rq, LANES), F32)],
        compiler_params=_cparams(("arbitrary", "arbitrary")),
        name="moba_sample_k",
    )(page_table, h, h, *([cache_k] * npg))
    v_spec = pltpu.PrefetchScalarGridSpec(
        num_scalar_prefetch=1,
        grid=(db, n_steps),
        in_specs=[pl.BlockSpec((ts, B_KV_DIM), lambda b, j, pt: (row_blk + b, H_BV // B_KV_DIM)),
                  pl.BlockSpec((None, B_KV_HEADS, rq, ck), lambda b, j, pt: (b, 0, 0, j)),
                  pl.BlockSpec((None, B_KV_HEADS, rq, LANES), lambda b, j, pt: (b, 0, 0, 0)),
                  pl.BlockSpec((None, B_KV_HEADS, rq, LANES), lambda b, j, pt: (b, 0, 0, 0))]
        + _page_specs(npg, page_shape, layer, lambda j: j)
        + [pl.BlockSpec(memory_space=pl.ANY)],
        out_specs=pl.BlockSpec((ts, B_DIM), lambda b, j, pt: (row_blk + b, A_DIM // B_DIM)),
        scratch_shapes=[pltpu.VMEM((B_KV_HEADS, rq, HEAD_DIM), F32)],
    )
    return pl.pallas_call(
        functools.partial(_moba_sample_v_kernel, npg=npg, n_steps=n_steps, ts=ts),
        grid_spec=v_spec,
        out_shape=jax.ShapeDtypeStruct(mix.shape, mix.dtype),
        input_output_aliases={5 + npg: 0},
        compiler_params=_cparams(("arbitrary", "arbitrary")),
        name="moba_sample_v",
    )(page_table, h, p, pn, l, *([cache_v] * npg), mix)


def _row_tile(n_rows, target):
    best = None
    for t in range(16, target + 1, 16):
        if n_rows % t == 0:
            best = t
    assert best is not None, n_rows
    return best


def _pack_w_in(w_in):
    order = ("aq", "bq", "iq", "pu", "ak", "av", "bk", "bv", "ik", "iw")
    parts = [w_in[..., _SRC[n][0]:_SRC[n][0] + _SRC[n][1]] for n in order]
    used = sum(_SRC[n][1] for n in order)
    parts.append(jnp.zeros(w_in.shape[:-1] + (H_DIM - used,), w_in.dtype))
    return jnp.concatenate(parts, axis=-1).astype(BF16)


def kernel(x_prompt, x_sample, cache_a_k, cache_a_v, cache_a_idx, cache_b_k, cache_b_v, state_pool,
           cache_mem_k, cache_mem_v, page_table, mem_prompt, w_in, w_out, pool_w, pool_scale, ln1_g, ln1_b,
           w_mem_q, w_mem_k, w_mem_v, w_mem_o, ln2_g, ln2_b, w_ffn_1, w_ffn_3, w_ffn_2, ln3_g, ln3_b):
    batch, seq, d = x_prompt.shape
    db, ts, _ = x_sample.shape
    depth = w_in.shape[0]
    n_past = page_table.shape[1] * PAGE_SIZE
    n_prompt = batch * seq
    n_rows = n_prompt + db * ts
    tm = _row_tile(n_rows, 700)
    tm_p = _row_tile(n_prompt, 1100)

    w_in_p = _pack_w_in(w_in)
    w_out_b = w_out.astype(BF16)
    w_mq_b = w_mem_q.astype(BF16)
    w_mkv_b = jnp.concatenate([w_mem_k, w_mem_v], axis=-1).astype(BF16)
    w_mo_b = w_mem_o.astype(BF16)
    w1_b, w3_b, w2_b = w_ffn_1.astype(BF16), w_ffn_3.astype(BF16), w_ffn_2.astype(BF16)
    pool_w_b = pool_w.astype(BF16)
    pool_scale3 = pool_scale.reshape(depth, 1, POOL_DIM)
    ln = [a.reshape(depth, 1, d) for a in (ln1_g, ln1_b, ln2_g, ln2_b, ln3_g, ln3_b)]
    mem = mem_prompt.reshape(batch * MEM_TOKENS, d)
    idx_t = jnp.swapaxes(cache_a_idx, 2, 3)

    x = jnp.concatenate([x_prompt.reshape(n_prompt, d), x_sample.reshape(db * ts, d)], axis=0)
    outs = {k: [] for k in ("pa_i", "p_pool", "pm_k", "pm_v", "sa_k", "sa_v", "sa_i", "sb_k", "sb_v", "s_pool")}
    kv_new = None
    for l in range(depth):
        h, kv_new = project_in(x, w_in_p, l, tm_p, n_prompt, kv_new)
        h = project_rows(x, w_in_p, l, h, n_prompt // (db * ts), db * ts, H_TN)
        mkv = matmul(mem, w_mkv_b, l, _row_tile(mem.shape[0], 512), MEM_DIM)

        mix = dsa_prompt(h, batch, seq, n_rows)
        mix = moba_prompt(h, mix, batch, seq)
        mix = pool_prompt(h, mix, pool_w_b, pool_scale3, l, batch, seq)
        bias = dsa_sample_select(page_table, h, idx_t, l, n_prompt, db, ts)
        mix = dsa_sample_attend(page_table, h, bias, cache_a_k, cache_a_v, mix, l, n_prompt, db, ts)
        mix = moba_sample(page_table, h, cache_b_k, cache_b_v, mix, l, n_prompt, db, ts)
        mix = pool_sample(h, mix, state_pool, pool_w_b, pool_scale3, l, n_prompt, db, ts, n_past)

        x1 = matmul_residual_ln(mix, w_out_b, x, ln[0], ln[1], l, tm, 1024)
        q = matmul(x1, w_mq_b, l, tm, MEM_DIM)
        o = memattn_prompt(q, mkv, batch, seq)
        o = memattn_sample(q, cache_mem_k, cache_mem_v, o, l, n_prompt, db, ts)
        x2 = matmul_residual_ln(o, w_mo_b, x1, ln[2], ln[3], l, tm, MEM_DIM)
        x = ffn_ln(x2, w1_b, w3_b, w2_b, ln[4], ln[5], l, tm, 512)

        hp, hs = h[:n_prompt], h[n_prompt:]

        def cols(a, off, width, lead):
            return a[:, off:off + width].reshape(lead)

        kv_s = (db, ts, A_KV_HEADS, HEAD_DIM)
        outs["pa_i"].append(cols(hp, H_IKW, IDX_DIM, (batch, seq, IDX_DIM)))
        outs["p_pool"].append(jnp.stack(
            [h[(b + 1) * seq - POOL_STATE:(b + 1) * seq, H_PU:H_PU + POOL_DIM] for b in range(batch)]))
        outs["pm_k"].append(mkv[:, :MEM_DIM].reshape(batch, MEM_TOKENS, MEM_HEADS, MEM_HEAD_DIM))
        outs["pm_v"].append(mkv[:, MEM_DIM:].reshape(batch, MEM_TOKENS, MEM_HEADS, MEM_HEAD_DIM))
        outs["sa_k"].append(cols(hs, H_AK, A_KV_DIM, kv_s))
        outs["sa_v"].append(cols(hs, H_AV, A_KV_DIM, kv_s))
        outs["sa_i"].append(cols(hs, H_IKW, IDX_DIM, (db, ts, IDX_DIM)))
        outs["sb_k"].append(cols(hs, H_BK, B_KV_DIM, kv_s))
        outs["sb_v"].append(cols(hs, H_BV, B_KV_DIM, kv_s))
        pu_s = cols(hs, H_PU, POOL_DIM, (db, ts, POOL_DIM))
        outs["s_pool"].append(jnp.concatenate([state_pool[l], pu_s], axis=1)[:, -POOL_STATE:])

    st = {k: jnp.stack(v) for k, v in outs.items()}
    for name, arr in zip(("pa_k", "pa_v", "pb_k", "pb_v"), kv_new):
        st[name] = arr.reshape(depth, batch, seq, A_KV_HEADS, HEAD_DIM)
    return (x[:n_prompt].reshape(batch, seq, d), x[n_prompt:].reshape(db, ts, d),
            st["pa_k"], st["pa_v"], st["pa_i"], st["pb_k"], st["pb_v"], st["p_pool"], st["pm_k"], st["pm_v"],
            st["sa_k"], st["sa_v"], st["sa_i"], st["sb_k"], st["sb_v"], st["s_pool"])
```

```python
import functools
import math

import jax
import jax.numpy as jnp
from jax import lax
from jax.experimental import pallas as pl
from jax.experimental.pallas import tpu as pltpu

F32 = jnp.float32
BF16 = jnp.bfloat16
I32 = jnp.int32

D_MODEL = 2048
DEPTH = 4
PAGE_SIZE = 128
HEAD_DIM = 128
A_HEADS = 6
A_KV_HEADS = 2
A_GROUP = A_HEADS // A_KV_HEADS
IDX_HEADS = 8
IDX_DIM = 64
DSA_TOPK = 256
B_HEADS = 6
B_KV_HEADS = 2
B_GROUP = B_HEADS // B_KV_HEADS
MOBA_BLOCK = 256
MOBA_TOPK = 3
POOL_WINDOWS = (2, 4, 8, 16)
POOL_GROUP_DIM = 128
POOL_DIM = len(POOL_WINDOWS) * POOL_GROUP_DIM
POOL_STATE = max(POOL_WINDOWS) - 1
A_DIM = A_HEADS * HEAD_DIM
A_KV_DIM = A_KV_HEADS * HEAD_DIM
B_DIM = B_HEADS * HEAD_DIM
B_KV_DIM = B_KV_HEADS * HEAD_DIM
MIX_DIM = A_DIM + B_DIM + POOL_DIM
MEM_TOKENS = 256
MEM_HEADS = 4
MEM_HEAD_DIM = 128
MEM_DIM = MEM_HEADS * MEM_HEAD_DIM
FFN_HIDDEN = -(-(8 * D_MODEL) // (3 * 256)) * 256
ALPHA = (2.0 * DEPTH) ** 0.25
LN_EPS = 1e-5
NEG_INF = -1e30
INT_MIN = -(2 ** 31)
SOFTMAX_C = (HEAD_DIM ** -0.5) * math.log2(math.e)

LANES = 128
SUBLANES = 8
VMEM_LIMIT_BYTES = 56 * 1024 * 1024

_SRC = dict(aq=(0, 768), ak=(768, 256), av=(1024, 256), iq=(1280, 512), ik=(1792, 64), iw=(1856, 8),
            bq=(1864, 768), bk=(2632, 256), bv=(2888, 256), pu=(3144, 512))
H_AQ, H_BQ, H_IQ, H_PU, H_AK, H_AV, H_BK, H_BV, H_IKW = 0, 768, 1536, 2048, 2560, 2816, 3072, 3328, 3584
H_DIM = 3840

DSA_TQ = 128
DSA_CK = 256
DSA_CA = 512
MOBA_TQ = 128
POOL_TT = 512
MEM_TR = 512
IDX_PAGES_PER_STEP = 64
KV_PAGES_PER_STEP = 32


def _cparams(sem):
    return pltpu.CompilerParams(dimension_semantics=sem, vmem_limit_bytes=VMEM_LIMIT_BYTES)


def _layer_norm_rows(y, g, b):
    mu = jnp.mean(y, axis=-1, keepdims=True)
    yc = y - mu
    var = jnp.mean(yc * yc, axis=-1, keepdims=True)
    return yc * lax.rsqrt(var + LN_EPS) * g + b


def _mm_kernel(x_ref, w_ref, o_ref, xb_ref):
    @pl.when(pl.program_id(1) == 0)
    def _():
        xb_ref[...] = x_ref[...].astype(BF16)

    o_ref[...] = jnp.dot(xb_ref[...], w_ref[...], preferred_element_type=F32)


def matmul(x, w, layer, tm, tn):
    m, k = x.shape
    n = w.shape[2]
    return pl.pallas_call(
        _mm_kernel,
        grid=(m // tm, n // tn),
        in_specs=[pl.BlockSpec((tm, k), lambda i, j: (i, 0)),
                  pl.BlockSpec((None, k, tn), lambda i, j: (layer, 0, j))],
        out_specs=pl.BlockSpec((tm, tn), lambda i, j: (i, j)),
        out_shape=jax.ShapeDtypeStruct((m, n), F32),
        scratch_shapes=[pltpu.VMEM((tm, k), BF16)],
        compiler_params=_cparams(("parallel", "arbitrary")),
        name="matmul",
    )(x, w)


def _mm_rows_kernel(x_ref, w_ref, o_in_ref, o_ref):
    del o_in_ref
    o_ref[...] = jnp.dot(x_ref[...].astype(BF16), w_ref[...], preferred_element_type=F32)


def project_rows(x, w, layer, out, row_block, rows, tn):
    k = x.shape[1]
    n = w.shape[2]
    return pl.pallas_call(
        _mm_rows_kernel,
        grid=(n // tn,),
        in_specs=[pl.BlockSpec((rows, k), lambda j: (row_block, 0)),
                  pl.BlockSpec((None, k, tn), lambda j: (layer, 0, j)),
                  pl.BlockSpec(memory_space=pl.ANY)],
        out_specs=pl.BlockSpec((rows, tn), lambda j: (row_block, j)),
        out_shape=jax.ShapeDtypeStruct(out.shape, out.dtype),
        input_output_aliases={2: 0},
        compiler_params=_cparams(("arbitrary",)),
        name="project_rows",
    )(x, w, out)


H_TN = 1280
_KV_PIECES = (H_AK, H_AV, H_BK, H_BV)


def _store_head_rows(ref, c, rows, n_heads, value):
    ref.reshape(rows * n_heads, HEAD_DIM)[pl.ds(c, rows, stride=n_heads), :] = value


def _project_in_kernel(x_ref, w_ref, *refs, tm):
    o_ref = refs[-6]
    kv_refs = refs[-5:-1]
    xb_ref = refs[-1]
    j = pl.program_id(1)

    @pl.when(j == 0)
    def _():
        xb_ref[...] = x_ref[...].astype(BF16)

    o_ref[...] = jnp.dot(xb_ref[...], w_ref[...], preferred_element_type=F32)
    for piece, kv_ref in zip(_KV_PIECES, kv_refs):
        step, off = divmod(piece, H_TN)

        @pl.when(j == step)
        def _(kv_ref=kv_ref, off=off):
            for c in range(A_KV_HEADS):
                lo = off + c * HEAD_DIM
                _store_head_rows(kv_ref, c, tm, A_KV_HEADS, o_ref[:, lo:lo + HEAD_DIM])


def project_in(x, w, layer, tm, n_prompt, kv_prev):
    m, k = x.shape
    depth, _, n = w.shape
    kv_shape = jax.ShapeDtypeStruct((depth, n_prompt, A_KV_HEADS, HEAD_DIM), F32)
    n_prev = 0 if kv_prev is None else len(kv_prev)
    kv_spec = pl.BlockSpec((None, tm, A_KV_HEADS, HEAD_DIM), lambda i, j: (layer, i, 0, 0))
    outs = pl.pallas_call(
        functools.partial(_project_in_kernel, tm=tm),
        grid=(n_prompt // tm, n // H_TN),
        in_specs=[pl.BlockSpec((tm, k), lambda i, j: (i, 0)),
                  pl.BlockSpec((None, k, H_TN), lambda i, j: (layer, 0, j))]
        + [pl.BlockSpec(memory_space=pl.ANY)] * n_prev,
        out_specs=[pl.BlockSpec((tm, H_TN), lambda i, j: (i, j))] + [kv_spec] * len(_KV_PIECES),
        out_shape=[jax.ShapeDtypeStruct((m, n), F32)] + [kv_shape] * len(_KV_PIECES),
        input_output_aliases={2 + i: 1 + i for i in range(n_prev)},
        scratch_shapes=[pltpu.VMEM((tm, k), BF16)],
        compiler_params=_cparams(("parallel", "arbitrary")),
        name="project_in",
    )(x, w, *(kv_prev or ()))
    return outs[0], tuple(outs[1:])


def _mm_res_ln_kernel(a_ref, w_ref, x_ref, g_ref, b_ref, o_ref, acc_ref, *, nk):
    kk = pl.program_id(1)

    @pl.when(kk == 0)
    def _():
        acc_ref[...] = jnp.zeros_like(acc_ref)

    acc_ref[...] += jnp.dot(a_ref[...].astype(BF16), w_ref[...], preferred_element_type=F32)

    @pl.when(kk == nk - 1)
    def _():
        y = ALPHA * x_ref[...] + acc_ref[...]
        o_ref[...] = _layer_norm_rows(y, g_ref[...], b_ref[...])


def matmul_residual_ln(a, w, x, g, b, layer, tm, tk):
    m, k = a.shape
    n = w.shape[2]
    nk = k // tk
    return pl.pallas_call(
        functools.partial(_mm_res_ln_kernel, nk=nk),
        grid=(m // tm, nk),
        in_specs=[pl.BlockSpec((tm, tk), lambda i, j: (i, j)),
                  pl.BlockSpec((None, tk, n), lambda i, j: (layer, j, 0)),
                  pl.BlockSpec((tm, n), lambda i, j: (i, 0)),
                  pl.BlockSpec((None, 1, n), lambda i, j: (layer, 0, 0)),
                  pl.BlockSpec((None, 1, n), lambda i, j: (layer, 0, 0))],
        out_specs=pl.BlockSpec((tm, n), lambda i, j: (i, 0)),
        out_shape=jax.ShapeDtypeStruct((m, n), F32),
        scratch_shapes=[pltpu.VMEM((tm, n), F32)],
        compiler_params=_cparams(("parallel", "arbitrary")),
        name="matmul_residual_ln",
    )(a, w, x, g, b)


def _ffn_kernel(x_ref, w1_ref, w3_ref, w2_ref, g_ref, b_ref, o_ref, xb_ref, acc_ref, *, nf):
    j = pl.program_id(1)

    @pl.when(j == 0)
    def _():
        xb_ref[...] = x_ref[...].astype(BF16)
        acc_ref[...] = jnp.zeros_like(acc_ref)

    xb = xb_ref[...]
    h1 = jnp.dot(xb, w1_ref[...], preferred_element_type=F32)
    h3 = jnp.dot(xb, w3_ref[...], preferred_element_type=F32)
    h = (h1 / (1.0 + jnp.exp(-h1))) * h3
    acc_ref[...] += jnp.dot(h.astype(BF16), w2_ref[...], preferred_element_type=F32)

    @pl.when(j == nf - 1)
    def _():
        y = ALPHA * x_ref[...] + acc_ref[...]
        o_ref[...] = _layer_norm_rows(y, g_ref[...], b_ref[...])


def ffn_ln(x, w1, w3, w2, g, b, layer, tm, tf):
    m, d = x.shape
    f = w1.shape[2]
    nf = f // tf
    return pl.pallas_call(
        functools.partial(_ffn_kernel, nf=nf),
        grid=(m // tm, nf),
        in_specs=[pl.BlockSpec((tm, d), lambda i, j: (i, 0)),
                  pl.BlockSpec((None, d, tf), lambda i, j: (layer, 0, j)),
                  pl.BlockSpec((None, d, tf), lambda i, j: (layer, 0, j)),
                  pl.BlockSpec((None, tf, d), lambda i, j: (layer, j, 0)),
                  pl.BlockSpec((None, 1, d), lambda i, j: (layer, 0, 0)),
                  pl.BlockSpec((None, 1, d), lambda i, j: (layer, 0, 0))],
        out_specs=pl.BlockSpec((tm, d), lambda i, j: (i, 0)),
        out_shape=jax.ShapeDtypeStruct((m, d), F32),
        scratch_shapes=[pltpu.VMEM((tm, d), BF16), pltpu.VMEM((tm, d), F32)],
        compiler_params=_cparams(("parallel", "arbitrary")),
        name="ffn_ln",
    )(x, w1, w3, w2, g, b)


def _sortable_key(score):
    score = jnp.where(score == 0.0, 0.0, score)
    bits = pltpu.bitcast(score, I32)
    return jnp.where(bits < 0, bits ^ 0x7FFFFFFF, bits)


def _kth_largest(count_ge, shape, k_sel):
    c0 = count_ge(jnp.zeros(shape, I32))
    ok0 = c0 >= k_sel
    base = jnp.where(ok0, 0, INT_MIN).astype(I32)
    cnt = jnp.where(ok0, c0, jnp.iinfo(jnp.int32).max).astype(I32)

    def bit_body(i, carry):
        base, cnt = carry
        cand = base | jnp.left_shift(jnp.int32(1), 30 - i)
        c = count_ge(cand)
        ok = c >= k_sel
        return jnp.where(ok, cand, base), jnp.where(ok, c, cnt)

    return lax.fori_loop(0, 31, bit_body, (base, cnt))


def _flash_step_t(kc, vt, qt, bias, m, l, acc):
    lg = jnp.dot(kc, qt, preferred_element_type=F32) + bias
    m_new = jnp.maximum(m, jnp.max(lg, axis=0, keepdims=True))
    alpha = jnp.exp2(m - m_new)
    p = jnp.exp2(lg - m_new)
    l_new = alpha * l + jnp.sum(p, axis=0, keepdims=True)
    acc_new = alpha * acc + jnp.dot(vt, p.astype(BF16), preferred_element_type=F32)
    return m_new, l_new, acc_new


def _prep_kv_t(k_ref, v_ref, k_s, vt_s, rows, n_heads):
    kk = k_ref[rows, :]
    vv = v_ref[rows, :]
    for c in range(n_heads):
        cols = slice(c * HEAD_DIM, (c + 1) * HEAD_DIM)
        k_s[c, rows, :] = kk[:, cols].astype(BF16)
        vt_s[c, :, rows] = vv[:, cols].T.astype(BF16)
    return kk


def _stage_queries_t(q, qt_s, n_kv, group):
    q = q * SOFTMAX_C
    for c in range(n_kv):
        qt_s[c] = jnp.concatenate(
            [q[:, (c * group + g) * HEAD_DIM:(c * group + g + 1) * HEAD_DIM].T for g in range(group)],
            axis=1).astype(BF16)


def _head_q(qt_s, hh, tq):
    c, g = divmod(hh, A_GROUP)
    return qt_s[c, :, g * tq:(g + 1) * tq]


def _write_heads(acc_s, ls, o_ref, n_heads):
    for hh in range(n_heads):
        o_ref[:, hh * HEAD_DIM:(hh + 1) * HEAD_DIM] = (acc_s[hh] / ls[hh]).T


def _dsa_prompt_kernel(aq_ref, iq_ref, ikw_ref, ak_ref, av_ref, o_ref,
                       ik_s, k_s, vt_s, iqt_s, iwt_s, qt_s, keys_s, bias_s, acc_s, *, seq, k_sel):
    tq, ck = DSA_TQ, DSA_CK
    qi = pl.program_id(1)
    q0 = qi * tq
    prep_rows = min(512, seq)

    @pl.when(qi == 0)
    def _prep():
        lane = lax.broadcasted_iota(I32, (prep_rows, LANES), 1)

        def body(r, _):
            rows = pl.ds(pl.multiple_of(r * prep_rows, prep_rows), prep_rows)
            ik_s[rows, :] = jnp.where(lane < IDX_DIM, ikw_ref[rows, :], 0.0).astype(BF16)
            _prep_kv_t(ak_ref, av_ref, k_s, vt_s, rows, A_KV_HEADS)
            return 0

        lax.fori_loop(0, seq // prep_rows, body, 0)

    n_ck = (q0 + tq + ck - 1) // ck
    ikw_q = ikw_ref[pl.ds(pl.multiple_of(q0, tq), tq), :]
    iwt_s[...] = ikw_q.T[IDX_DIM:IDX_DIM + IDX_HEADS] * (IDX_HEADS ** -0.5)
    iqt = (iq_ref[...] * (IDX_DIM ** -0.5)).T
    zpad = jnp.zeros((LANES - IDX_DIM, 2 * tq), F32)
    for hp in range(IDX_HEADS // 2):
        pair = jnp.concatenate([iqt[(2 * hp) * IDX_DIM:(2 * hp + 1) * IDX_DIM],
                                iqt[(2 * hp + 1) * IDX_DIM:(2 * hp + 2) * IDX_DIM]], axis=1)
        iqt_s[hp] = jnp.concatenate([pair, zpad], axis=0).astype(BF16)
    _stage_queries_t(aq_ref[...], qt_s, A_KV_HEADS, A_GROUP)

    qpos = q0 + lax.broadcasted_iota(I32, (ck, tq), 1)
    krow = lax.broadcasted_iota(I32, (ck, tq), 0)

    def score_chunk(off):
        ikc = ik_s[pl.ds(off, ck), :]
        acc = jnp.zeros((ck, tq), F32)
        for hp in range(IDX_HEADS // 2):
            s = jnp.dot(ikc, iqt_s[hp], preferred_element_type=F32)
            acc = acc + (jnp.maximum(s[:, :tq], 0.0) * iwt_s[2 * hp:2 * hp + 1, :]
                         + jnp.maximum(s[:, tq:], 0.0) * iwt_s[2 * hp + 1:2 * hp + 2, :])
        keys_s[pl.ds(off, ck), :] = jnp.where(off + krow <= qpos, _sortable_key(acc), INT_MIN)

    def score_body(c, _):
        off = pl.multiple_of(c * (2 * ck), 2 * ck)
        score_chunk(off)
        score_chunk(off + ck)
        return 0

    lax.fori_loop(0, (n_ck + 1) // 2, score_body, 0)

    part = ck // 4

    def count_ge(cand):
        def body(c, acc):
            ind = jnp.where(keys_s[pl.ds(pl.multiple_of(c * ck, ck), ck), :] >= cand, 1, 0).astype(I32)
            return acc + ((ind[0:part] + ind[part:2 * part]) + (ind[2 * part:3 * part] + ind[3 * part:]))

        acc = lax.fori_loop(0, n_ck, body, jnp.zeros((part, tq), I32))
        return jnp.sum(acc, axis=0, keepdims=True)

    thr, cnt_thr = _kth_largest(count_ge, (1, tq), k_sel)
    has_tie = jnp.logical_and(cnt_thr > k_sel, thr > INT_MIN)
    any_tie = jnp.max(has_tie.astype(I32)) > 0
    thr_c = jnp.maximum(thr, INT_MIN + 1)

    @pl.when(jnp.logical_not(any_tie))
    def _fast():
        def body(c, _):
            rows = pl.ds(pl.multiple_of(c * ck, ck), ck)
            bias_s[rows, :] = jnp.where(keys_s[rows, :] >= thr_c, 0.0, NEG_INF)
            return 0

        lax.fori_loop(0, n_ck, body, 0)

    @pl.when(any_tie)
    def _ties():
        need = (k_sel - count_ge(thr_c + 1)).astype(F32)
        tri = jnp.where(lax.broadcasted_iota(I32, (LANES, LANES), 0)
                        >= lax.broadcasted_iota(I32, (LANES, LANES), 1), 1.0, 0.0).astype(BF16)

        def body(c, run):
            rows = pl.ds(pl.multiple_of(c * LANES, LANES), LANES)
            kchunk = keys_s[rows, :]
            eq = kchunk == thr_c
            eqf = jnp.where(eq, 1.0, 0.0)
            pre = jnp.dot(tri, eqf.astype(BF16), preferred_element_type=F32) + run
            sel = jnp.logical_or(kchunk > thr_c, jnp.logical_and(eq, pre <= need))
            bias_s[rows, :] = jnp.where(sel, 0.0, NEG_INF)
            return run + jnp.sum(eqf, axis=0, keepdims=True)

        lax.fori_loop(0, n_ck * (ck // LANES), body, jnp.zeros((1, tq), F32))

    ca = DSA_CA
    n_ca = (q0 + tq + ca - 1) // ca
    for extra in range(1, ca // ck):
        @pl.when(n_ck + extra <= n_ca * (ca // ck))
        def _mask_tail(extra=extra):
            bias_s[pl.ds(pl.multiple_of((n_ck + extra - 1) * ck, ck), ck), :] = jnp.full((ck, tq), NEG_INF, F32)

    acc_s[...] = jnp.zeros_like(acc_s)

    def att_body(j, carry):
        ms, ls = carry
        rows = pl.ds(pl.multiple_of(j * ca, ca), ca)
        bias = bias_s[rows, :]
        new_m, new_l = [], []
        for hh in range(A_HEADS):
            c = hh // A_GROUP
            m, l, acc = _flash_step_t(k_s[c, rows, :], vt_s[c, :, rows], _head_q(qt_s, hh, tq), bias,
                                      ms[hh], ls[hh], acc_s[hh])
            acc_s[hh] = acc
            new_m.append(m)
            new_l.append(l)
        return tuple(new_m), tuple(new_l)

    init = (tuple(jnp.full((1, tq), NEG_INF, F32) for _ in range(A_HEADS)),
            tuple(jnp.zeros((1, tq), F32) for _ in range(A_HEADS)))
    _, ls = lax.fori_loop(0, n_ca, att_body, init)
    _write_heads(acc_s, ls, o_ref, A_HEADS)


def dsa_prompt(h, batch, seq, n_rows):
    tq = DSA_TQ
    nq = seq // tq
    k_sel = min(DSA_TOPK, seq // 4)
    kern = functools.partial(_dsa_prompt_kernel, seq=seq, k_sel=k_sel)
    return pl.pallas_call(
        kern,
        grid=(batch, nq),
        in_specs=[pl.BlockSpec((tq, A_DIM), lambda b, i: (b * nq + i, H_AQ // A_DIM)),
                  pl.BlockSpec((tq, IDX_HEADS * IDX_DIM), lambda b, i: (b * nq + i, H_IQ // (IDX_HEADS * IDX_DIM))),
                  pl.BlockSpec((seq, LANES), lambda b, i: (b, H_IKW // LANES)),
                  pl.BlockSpec((seq, A_KV_DIM), lambda b, i: (b, H_AK // A_KV_DIM)),
                  pl.BlockSpec((seq, A_KV_DIM), lambda b, i: (b, H_AV // A_KV_DIM))],
        out_specs=pl.BlockSpec((tq, A_DIM), lambda b, i: (b * nq + i, 0)),
        out_shape=jax.ShapeDtypeStruct((n_rows, MIX_DIM), F32),
        scratch_shapes=[pltpu.VMEM((seq, LANES), BF16),
                        pltpu.VMEM((A_KV_HEADS, seq, HEAD_DIM), BF16),
                        pltpu.VMEM((A_KV_HEADS, HEAD_DIM, seq), BF16),
                        pltpu.VMEM((IDX_HEADS // 2, LANES, 2 * tq), BF16),
                        pltpu.VMEM((IDX_HEADS, tq), F32),
                        pltpu.VMEM((A_KV_HEADS, HEAD_DIM, A_GROUP * tq), BF16),
                        pltpu.VMEM((seq, tq), I32),
                        pltpu.VMEM((seq, tq), F32),
                        pltpu.VMEM((A_HEADS, HEAD_DIM, tq), F32)],
        compiler_params=_cparams(("arbitrary", "arbitrary")),
        name="dsa_prompt",
    )(h, h, h, h, h)


def _rank_select(gate, n_past_blocks, axis):
    idx = lax.broadcasted_iota(I32, gate.shape, axis)

    def body(m, rank):
        gm = jnp.sum(jnp.where(idx == m, gate, 0.0), axis=axis, keepdims=True)
        beats = jnp.logical_or(gm > gate, jnp.logical_and(gm == gate, m < idx))
        return rank + jnp.where(beats, 1, 0).astype(I32)

    rank = lax.fori_loop(0, n_past_blocks, body, jnp.zeros(gate.shape, I32))
    sel = jnp.logical_and(idx < n_past_blocks, rank < MOBA_TOPK)
    return jnp.where(sel, 0.0, NEG_INF)


def _moba_prompt_kernel(bq_ref, bk_ref, bv_ref, mix_ref, o_ref, k_s, vt_s, km_s, kmb_s, qt_s, sb_s, acc_s,
                        *, seq, nbp):
    del mix_ref
    tq, blk = MOBA_TQ, MOBA_BLOCK
    qi = pl.program_id(1)
    q0 = qi * tq
    nb = seq // blk

    @pl.when(qi == 0)
    def _prep():
        km_s[...] = jnp.zeros_like(km_s)

        def body(n, _):
            rows = pl.ds(pl.multiple_of(n * blk, blk), blk)
            kk = _prep_kv_t(bk_ref, bv_ref, k_s, vt_s, rows, B_KV_HEADS)
            km_s[pl.ds(n, 1), :] = jnp.mean(kk, axis=0, keepdims=True)
            return 0

        lax.fori_loop(0, nb, body, 0)
        for c in range(B_KV_HEADS):
            kmb_s[c] = km_s[:, c * HEAD_DIM:(c + 1) * HEAD_DIM].astype(BF16)

    own = q0 // blk
    own_rows = pl.ds(pl.multiple_of(own * blk, blk), blk)
    bq = bq_ref[...]
    _stage_queries_t(bq, qt_s, B_KV_HEADS, B_GROUP)
    gates = []
    for c in range(B_KV_HEADS):
        qg = jnp.concatenate(
            [bq[:, (c * B_GROUP + g) * HEAD_DIM:(c * B_GROUP + g + 1) * HEAD_DIM].T for g in range(B_GROUP)],
            axis=1).astype(BF16)
        gates.append(jnp.dot(kmb_s[c], qg, preferred_element_type=F32)[:nbp])
    sel_bias = _rank_select(jnp.concatenate(gates, axis=1), own, axis=0)
    for hh in range(B_HEADS):
        sb_s[hh] = sel_bias[:, hh * tq:(hh + 1) * tq]

    qpos = q0 + lax.broadcasted_iota(I32, (blk, tq), 1)
    kpos = own * blk + lax.broadcasted_iota(I32, (blk, tq), 0)
    causal_bias = jnp.where(kpos <= qpos, 0.0, NEG_INF)
    ms, ls = [], []
    for hh in range(B_HEADS):
        c = hh // B_GROUP
        m, l, acc = _flash_step_t(k_s[c, own_rows, :], vt_s[c, :, own_rows], _head_q(qt_s, hh, tq), causal_bias,
                                  jnp.full((1, tq), NEG_INF, F32), jnp.zeros((1, tq), F32),
                                  jnp.zeros((HEAD_DIM, tq), F32))
        acc_s[hh] = acc
        ms.append(m)
        ls.append(l)

    odd = own % 2

    def pair_body(p, carry):
        ms, ls = carry
        b0 = jnp.maximum(2 * p - odd, 0)
        rows = pl.ds(pl.multiple_of(b0 * blk, blk), 2 * blk)
        skip_second = jnp.where(jnp.logical_and(p == 0, odd == 1), NEG_INF, 0.0)
        new_m, new_l = [], []
        for hh in range(B_HEADS):
            c = hh // B_GROUP
            bias = jnp.concatenate(
                [jnp.broadcast_to(sb_s[hh, pl.ds(b0, 1), :], (blk, tq)),
                 jnp.broadcast_to(sb_s[hh, pl.ds(b0 + 1, 1), :] + skip_second, (blk, tq))], axis=0)
            m, l, acc = _flash_step_t(k_s[c, rows, :], vt_s[c, :, rows], _head_q(qt_s, hh, tq), bias,
                                      ms[hh], ls[hh], acc_s[hh])
            acc_s[hh] = acc
            new_m.append(m)
            new_l.append(l)
        return tuple(new_m), tuple(new_l)

    _, ls = lax.fori_loop(0, (own + 1) // 2, pair_body, (tuple(ms), tuple(ls)))
    _write_heads(acc_s, ls, o_ref, B_HEADS)


def moba_prompt(h, mix, batch, seq):
    tq = MOBA_TQ
    nq = seq // tq
    nbp = -(-(seq // MOBA_BLOCK) // SUBLANES) * SUBLANES
    return pl.pallas_call(
        functools.partial(_moba_prompt_kernel, seq=seq, nbp=nbp),
        grid=(batch, nq),
        in_specs=[pl.BlockSpec((tq, B_DIM), lambda b, i: (b * nq + i, H_BQ // B_DIM)),
                  pl.BlockSpec((seq, B_KV_DIM), lambda b, i: (b, H_BK // B_KV_DIM)),
                  pl.BlockSpec((seq, B_KV_DIM), lambda b, i: (b, H_BV // B_KV_DIM)),
                  pl.BlockSpec(memory_space=pl.ANY)],
        out_specs=pl.BlockSpec((tq, B_DIM), lambda b, i: (b * nq + i, A_DIM // B_DIM)),
        out_shape=jax.ShapeDtypeStruct(mix.shape, mix.dtype),
        input_output_aliases={3: 0},
        scratch_shapes=[pltpu.VMEM((B_KV_HEADS, seq, HEAD_DIM), BF16),
                        pltpu.VMEM((B_KV_HEADS, HEAD_DIM, seq), BF16),
                        pltpu.VMEM((LANES, B_KV_DIM), F32),
                        pltpu.VMEM((B_KV_HEADS, LANES, HEAD_DIM), BF16),
                        pltpu.VMEM((B_KV_HEADS, HEAD_DIM, B_GROUP * tq), BF16),
                        pltpu.VMEM((B_HEADS, nbp, tq), F32),
                        pltpu.VMEM((B_HEADS, HEAD_DIM, tq), F32)],
        compiler_params=_cparams(("arbitrary", "arbitrary")),
        name="moba_prompt",
    )(h, h, h, mix)


POOL_HALO = 16


def _pool_groups(u, window_sum, cnt_of, w_ref, sc_ref, o_ref):
    for g, w in enumerate(POOL_WINDOWS):
        cols = slice(g * POOL_GROUP_DIM, (g + 1) * POOL_GROUP_DIM)
        ug = u[:, cols]
        d = window_sum(g, w, ug) / cnt_of(w) - ug
        y = jnp.dot(d.astype(BF16), w_ref[g], preferred_element_type=F32)
        o_ref[:, cols] = y * sc_ref[:, cols]


def _pool_prompt_kernel(u_ref, halo_ref, w_ref, sc_ref, mix_ref, o_ref, ext_s):
    del mix_ref
    tt = POOL_TT
    i = pl.program_id(1)
    u = u_ref[...]
    ext_s[0:POOL_HALO, :] = jnp.where(i == 0, 0.0, halo_ref[...])
    ext_s[POOL_HALO:POOL_HALO + tt, :] = u
    t = i * tt + lax.broadcasted_iota(I32, (tt, 1), 0)

    def window_sum(g, w, ug):
        s = ug
        for k in range(1, w):
            s = s + ext_s[pl.ds(POOL_HALO - k, tt), g * POOL_GROUP_DIM:(g + 1) * POOL_GROUP_DIM]
        return s

    _pool_groups(u, window_sum, lambda w: jnp.minimum(w, t + 1).astype(F32), w_ref, sc_ref, o_ref)


def pool_prompt(h, mix, pool_w, pool_scale, layer, batch, seq):
    tt = POOL_TT
    nt = seq // tt
    halo_blocks = tt // POOL_HALO
    return pl.pallas_call(
        _pool_prompt_kernel,
        grid=(batch, nt),
        in_specs=[pl.BlockSpec((tt, POOL_DIM), lambda b, i: (b * nt + i, H_PU // POOL_DIM)),
                  pl.BlockSpec((POOL_HALO, POOL_DIM),
                               lambda b, i: (jnp.maximum((b * nt + i) * halo_blocks - 1, 0), H_PU // POOL_DIM)),
                  pl.BlockSpec((None, len(POOL_WINDOWS), POOL_GROUP_DIM, POOL_GROUP_DIM),
                               lambda b, i: (layer, 0, 0, 0)),
                  pl.BlockSpec((None, 1, POOL_DIM), lambda b, i: (layer, 0, 0)),
                  pl.BlockSpec(memory_space=pl.ANY)],
        out_specs=pl.BlockSpec((tt, POOL_DIM), lambda b, i: (b * nt + i, (A_DIM + B_DIM) // POOL_DIM)),
        out_shape=jax.ShapeDtypeStruct(mix.shape, mix.dtype),
        input_output_aliases={4: 0},
        scratch_shapes=[pltpu.VMEM((POOL_HALO + tt, POOL_DIM), F32)],
        compiler_params=_cparams(("arbitrary", "arbitrary")),
        name="pool_prompt",
    )(h, h, pool_w, pool_scale, mix)


def _pool_sample_kernel(u_ref, st_ref, w_ref, sc_ref, mix_ref, o_ref, ext_s, *, db, ts, n_past):
    del mix_ref
    u = u_ref[...]
    for b in range(db):
        ext_s[b, POOL_HALO - POOL_STATE:POOL_HALO, :] = st_ref[b]
        ext_s[b, POOL_HALO:POOL_HALO + ts, :] = u[b * ts:(b + 1) * ts]
    t = n_past + lax.broadcasted_iota(I32, (db * ts, 1), 0) % ts

    def window_sum(g, w, ug):
        parts = []
        for b in range(db):
            s = ug[b * ts:(b + 1) * ts]
            for k in range(1, w):
                s = s + ext_s[b, pl.ds(POOL_HALO - k, ts), g * POOL_GROUP_DIM:(g + 1) * POOL_GROUP_DIM]
            parts.append(s)
        return jnp.concatenate(parts, axis=0)

    _pool_groups(u, window_sum, lambda w: jnp.minimum(w, t + 1).astype(F32), w_ref, sc_ref, o_ref)


def pool_sample(h, mix, state_pool, pool_w, pool_scale, layer, n_prompt, db, ts, n_past):
    rows = db * ts
    return pl.pallas_call(
        functools.partial(_pool_sample_kernel, db=db, ts=ts, n_past=n_past),
        grid=(1,),
        in_specs=[pl.BlockSpec((rows, POOL_DIM), lambda i: (n_prompt // rows, H_PU // POOL_DIM)),
                  pl.BlockSpec((None, db, POOL_STATE, POOL_DIM), lambda i: (layer, 0, 0, 0)),
                  pl.BlockSpec((None, len(POOL_WINDOWS), POOL_GROUP_DIM, POOL_GROUP_DIM),
                               lambda i: (layer, 0, 0, 0)),
                  pl.BlockSpec((None, 1, POOL_DIM), lambda i: (layer, 0, 0)),
                  pl.BlockSpec(memory_space=pl.ANY)],
        out_specs=pl.BlockSpec((rows, POOL_DIM), lambda i: (n_prompt // rows, (A_DIM + B_DIM) // POOL_DIM)),
        out_shape=jax.ShapeDtypeStruct(mix.shape, mix.dtype),
        input_output_aliases={4: 0},
        scratch_shapes=[pltpu.VMEM((db, POOL_HALO + ts, POOL_DIM), F32)],
        compiler_params=_cparams(("arbitrary",)),
        name="pool_sample",
    )(h, state_pool, pool_w, pool_scale, mix)


def _memattn_heads(q_ref, head_k, head_v, o_ref):
    scale = MEM_HEAD_DIM ** -0.5
    for hh in range(MEM_HEADS):
        cols = slice(hh * MEM_HEAD_DIM, (hh + 1) * MEM_HEAD_DIM)
        qh = q_ref[:, cols].astype(BF16)
        lg = lax.dot_general(qh, head_k(hh).astype(BF16), (((1,), (1,)), ((), ())),
                             preferred_element_type=F32) * scale
        m = jnp.max(lg, axis=1, keepdims=True)
        p = jnp.exp(lg - m)
        l = jnp.sum(p, axis=1, keepdims=True)
        o_ref[:, cols] = jnp.dot(p.astype(BF16), head_v(hh).astype(BF16), preferred_element_type=F32) / l


def _memattn_prompt_kernel(q_ref, mk_ref, mv_ref, o_ref):
    _memattn_heads(q_ref, lambda hh: mk_ref[:, hh * MEM_HEAD_DIM:(hh + 1) * MEM_HEAD_DIM],
                   lambda hh: mv_ref[:, hh * MEM_HEAD_DIM:(hh + 1) * MEM_HEAD_DIM], o_ref)


def _memattn_sample_kernel(q_ref, mk_ref, mv_ref, o_in_ref, o_ref):
    del o_in_ref
    _memattn_heads(q_ref, lambda hh: _head_rows(mk_ref, hh, MEM_TOKENS, MEM_HEADS),
                   lambda hh: _head_rows(mv_ref, hh, MEM_TOKENS, MEM_HEADS), o_ref)


def memattn_prompt(q, mkv, batch, seq):
    tr = min(MEM_TR, seq)
    nt = seq // tr
    return pl.pallas_call(
        _memattn_prompt_kernel,
        grid=(batch, nt),
        in_specs=[pl.BlockSpec((tr, MEM_DIM), lambda b, i: (b * nt + i, 0)),
                  pl.BlockSpec((MEM_TOKENS, MEM_DIM), lambda b, i: (b, 0)),
                  pl.BlockSpec((MEM_TOKENS, MEM_DIM), lambda b, i: (b, 1))],
        out_specs=pl.BlockSpec((tr, MEM_DIM), lambda b, i: (b * nt + i, 0)),
        out_shape=jax.ShapeDtypeStruct(q.shape, F32),
        compiler_params=_cparams(("arbitrary", "arbitrary")),
        name="memattn_prompt",
    )(q, mkv, mkv)


def memattn_sample(q, cache_k, cache_v, o, layer, n_prompt, db, ts):
    cache_spec = pl.BlockSpec((None, None, MEM_TOKENS, MEM_HEADS, MEM_HEAD_DIM), lambda b: (layer, b, 0, 0, 0))
    return pl.pallas_call(
        _memattn_sample_kernel,
        grid=(db,),
        in_specs=[pl.BlockSpec((ts, MEM_DIM), lambda b: (n_prompt // ts + b, 0)),
                  cache_spec, cache_spec,
                  pl.BlockSpec(memory_space=pl.ANY)],
        out_specs=pl.BlockSpec((ts, MEM_DIM), lambda b: (n_prompt // ts + b, 0)),
        out_shape=jax.ShapeDtypeStruct(o.shape, o.dtype),
        input_output_aliases={3: 0},
        compiler_params=_cparams(("arbitrary",)),
        name="memattn_sample",
    )(q, cache_k, cache_v, o)


def _page_specs(npg, page_shape, layer, step_of):
    zeros = (0,) * len(page_shape)

    def spec(i):
        return pl.BlockSpec((None, None) + page_shape,
                            lambda b, j, pt: (layer, pt[b, step_of(j) * npg + i]) + zeros)
    return [spec(i) for i in range(npg)]


def _pad_rows(x, rows):
    return jnp.concatenate([x, jnp.zeros((rows - x.shape[0], x.shape[1]), x.dtype)], axis=0)


def _head_rows(ref, c, rows, n_heads):
    return ref.reshape(rows * n_heads, HEAD_DIM)[pl.ds(c, rows, stride=n_heads), :]


def _gather_head(pages, c, n_heads):
    return jnp.concatenate([_head_rows(p, c, PAGE_SIZE, n_heads) for p in pages], axis=0)


def _dsa_sample_select_kernel(pt_ref, iq_ref, ikw_ref, *refs, npg, n_steps, n_past, ts, k_sel):
    del pt_ref
    pages = refs[:npg]
    bias_ref, keys_s = refs[npg], refs[npg + 1]
    j = pl.program_id(1)
    ck = npg * PAGE_SIZE
    lp = n_past + LANES
    iq = iq_ref[...] * (IDX_DIM ** -0.5)
    iq_st = jnp.concatenate([iq[:, h * IDX_DIM:(h + 1) * IDX_DIM] for h in range(IDX_HEADS)],
                            axis=0).astype(BF16)
    ikw = ikw_ref[...]
    iw = ikw[:, IDX_DIM:IDX_DIM + IDX_HEADS] * (IDX_HEADS ** -0.5)

    def scores(ik_t):
        s = jnp.dot(iq_st, ik_t.astype(BF16), preferred_element_type=F32)
        acc = jnp.zeros((ts, ik_t.shape[1]), F32)
        for h in range(IDX_HEADS):
            acc = acc + jnp.maximum(s[h * ts:(h + 1) * ts], 0.0) * iw[:, h:h + 1]
        return acc

    ik_past = jnp.concatenate([pages[i][...] for i in range(npg)], axis=1)
    keys_s[:, pl.ds(pl.multiple_of(j * ck, ck), ck)] = _sortable_key(scores(ik_past))

    @pl.when(j == n_steps - 1)
    def _select():
        sc_new = scores(_pad_rows(ikw, LANES).T[:IDX_DIM])
        causal = lax.broadcasted_iota(I32, (ts, LANES), 1) <= lax.broadcasted_iota(I32, (ts, LANES), 0)
        keys_s[:, n_past:lp] = jnp.where(causal, _sortable_key(sc_new), INT_MIN)

        def count_ge(cand):
            return jnp.sum(jnp.where(keys_s[...] >= cand, 1, 0).astype(I32), axis=1, keepdims=True)

        thr, cnt_thr = _kth_largest(count_ge, (ts, 1), k_sel)
        has_tie = jnp.logical_and(cnt_thr > k_sel, thr > INT_MIN)
        any_tie = jnp.max(has_tie.astype(I32)) > 0
        thr_c = jnp.maximum(thr, INT_MIN + 1)

        @pl.when(jnp.logical_not(any_tie))
        def _fast():
            bias_ref[...] = jnp.where(keys_s[...] >= thr_c, 0.0, NEG_INF)

        @pl.when(any_tie)
        def _ties():
            need = (k_sel - count_ge(thr_c + 1)).astype(F32)
            tri = jnp.where(lax.broadcasted_iota(I32, (LANES, LANES), 0)
                            <= lax.broadcasted_iota(I32, (LANES, LANES), 1), 1.0, 0.0).astype(BF16)

            def body(c, run):
                cols = pl.ds(pl.multiple_of(c * LANES, LANES), LANES)
                kchunk = keys_s[:, cols]
                eq = kchunk == thr_c
                eqf = jnp.where(eq, 1.0, 0.0)
                pre = jnp.dot(eqf.astype(BF16), tri, preferred_element_type=F32) + run
                sel = jnp.logical_or(kchunk > thr_c, jnp.logical_and(eq, pre <= need))
                bias_ref[:, cols] = jnp.where(sel, 0.0, NEG_INF)
                return run + jnp.sum(eqf, axis=1, keepdims=True)

            lax.fori_loop(0, lp // LANES, body, jnp.zeros((ts, 1), F32))


def dsa_sample_select(page_table, h, cache_idx, layer, n_prompt, db, ts):
    n_pages = page_table.shape[1]
    n_past = n_pages * PAGE_SIZE
    npg = min(IDX_PAGES_PER_STEP, n_pages)
    n_steps = n_pages // npg
    k_sel = min(DSA_TOPK, (n_past + ts) // 4)
    lp = n_past + LANES
    kern = functools.partial(_dsa_sample_select_kernel, npg=npg, n_steps=n_steps, n_past=n_past, ts=ts, k_sel=k_sel)
    row_blk = n_prompt // ts
    grid_spec = pltpu.PrefetchScalarGridSpec(
        num_scalar_prefetch=1,
        grid=(db, n_steps),
        in_specs=[pl.BlockSpec((ts, IDX_HEADS * IDX_DIM), lambda b, j, pt: (row_blk + b, H_IQ // (IDX_HEADS * IDX_DIM))),
                  pl.BlockSpec((ts, LANES), lambda b, j, pt: (row_blk + b, H_IKW // LANES))]
        + _page_specs(npg, (IDX_DIM, PAGE_SIZE), layer, lambda j: j),
        out_specs=pl.BlockSpec((None, ts, lp), lambda b, j, pt: (b, 0, 0)),
        scratch_shapes=[pltpu.VMEM((ts, lp), I32)],
    )
    return pl.pallas_call(
        kern,
        grid_spec=grid_spec,
        out_shape=jax.ShapeDtypeStruct((db, ts, lp), F32),
        compiler_params=_cparams(("arbitrary", "arbitrary")),
        name="dsa_sample_select",
    )(page_table, h, h, *([cache_idx] * npg))


_NT = (((1,), (1,)), ((), ()))


def _sample_rows(group, ts):
    return -(-(group * ts) // 16) * 16


def _stack_heads_rows(q, c, group, ts, rows):
    return jnp.concatenate(
        [q[:, (c * group + g) * HEAD_DIM:(c * group + g + 1) * HEAD_DIM] for g in range(group)]
        + [jnp.zeros((rows - group * ts, HEAD_DIM), q.dtype)], axis=0)


def _tile_rows(bs, group, rows):
    return jnp.concatenate([bs] * group + [jnp.zeros((rows - group * bs.shape[0], bs.shape[1]), bs.dtype)], axis=0)


def _write_head_rows(out, o_ref, c, group, ts):
    for g in range(group):
        hcol = (c * group + g) * HEAD_DIM
        o_ref[:, hcol:hcol + HEAD_DIM] = out[g * ts:(g + 1) * ts]


def _dsa_sample_attend_kernel(pt_ref, q_ref, kn_ref, vn_ref, bias_ref, *refs, npg, n_steps, n_past, ts, rq):
    del pt_ref
    kp, vp = refs[:npg], refs[npg:2 * npg]
    o_ref = refs[2 * npg + 1]
    q_s, m_s, l_s, acc_s = refs[2 * npg + 2:]
    j = pl.program_id(1)
    ck = npg * PAGE_SIZE
    scale = HEAD_DIM ** -0.5

    @pl.when(j == 0)
    def _init():
        q = q_ref[...]
        for c in range(A_KV_HEADS):
            q_s[c] = _stack_heads_rows(q, c, A_GROUP, ts, rq).astype(BF16)
        m_s[...] = jnp.full(m_s.shape, NEG_INF, F32)
        l_s[...] = jnp.zeros_like(l_s)
        acc_s[...] = jnp.zeros_like(acc_s)

    def update(c, kc, vc, bias_rows):
        lg = lax.dot_general(q_s[c], kc.astype(BF16), _NT, preferred_element_type=F32) * scale + bias_rows
        m_old = m_s[c]
        m_new = jnp.maximum(m_old, jnp.max(lg, axis=1, keepdims=True))
        alpha = jnp.exp(m_old - m_new)
        p = jnp.exp(lg - m_new)
        l_s[c] = alpha * l_s[c] + jnp.sum(p, axis=1, keepdims=True)
        acc_s[c] = alpha * acc_s[c] + jnp.dot(p.astype(BF16), vc.astype(BF16), preferred_element_type=F32)
        m_s[c] = m_new

    bias_rows = _tile_rows(bias_ref[:, pl.ds(pl.multiple_of(j * ck, ck), ck)], A_GROUP, rq)
    for c in range(A_KV_HEADS):
        update(c, _gather_head(kp, c, A_KV_HEADS), _gather_head(vp, c, A_KV_HEADS), bias_rows)

    @pl.when(j == n_steps - 1)
    def _finish():
        bias_n = _tile_rows(bias_ref[:, n_past:n_past + LANES], A_GROUP, rq)
        for c in range(A_KV_HEADS):
            cols = slice(c * HEAD_DIM, (c + 1) * HEAD_DIM)
            update(c, _pad_rows(kn_ref[:, cols], LANES), _pad_rows(vn_ref[:, cols], LANES), bias_n)
            _write_head_rows(acc_s[c] / l_s[c], o_ref, c, A_GROUP, ts)


def dsa_sample_attend(page_table, h, bias, cache_k, cache_v, mix, layer, n_prompt, db, ts):
    n_pages = page_table.shape[1]
    n_past = n_pages * PAGE_SIZE
    npg = min(KV_PAGES_PER_STEP, n_pages)
    n_steps = n_pages // npg
    lp = n_past + LANES
    rq = _sample_rows(A_GROUP, ts)
    kern = functools.partial(_dsa_sample_attend_kernel, npg=npg, n_steps=n_steps, n_past=n_past, ts=ts, rq=rq)
    row_blk = n_prompt // ts
    kv_tail = (PAGE_SIZE, A_KV_HEADS, HEAD_DIM)
    grid_spec = pltpu.PrefetchScalarGridSpec(
        num_scalar_prefetch=1,
        grid=(db, n_steps),
        in_specs=[pl.BlockSpec((ts, A_DIM), lambda b, j, pt: (row_blk + b, H_AQ // A_DIM)),
                  pl.BlockSpec((ts, A_KV_DIM), lambda b, j, pt: (row_blk + b, H_AK // A_KV_DIM)),
                  pl.BlockSpec((ts, A_KV_DIM), lambda b, j, pt: (row_blk + b, H_AV // A_KV_DIM)),
                  pl.BlockSpec((None, ts, lp), lambda b, j, pt: (b, 0, 0))]
        + _page_specs(npg, kv_tail, layer, lambda j: j)
        + _page_specs(npg, kv_tail, layer, lambda j: j)
        + [pl.BlockSpec(memory_space=pl.ANY)],
        out_specs=pl.BlockSpec((ts, A_DIM), lambda b, j, pt: (row_blk + b, 0)),
        scratch_shapes=[pltpu.VMEM((A_KV_HEADS, rq, HEAD_DIM), BF16),
                        pltpu.VMEM((A_KV_HEADS, rq, 1), F32),
                        pltpu.VMEM((A_KV_HEADS, rq, 1), F32),
                        pltpu.VMEM((A_KV_HEADS, rq, HEAD_DIM), F32)],
    )
    return pl.pallas_call(
        kern,
        grid_spec=grid_spec,
        out_shape=jax.ShapeDtypeStruct(mix.shape, mix.dtype),
        input_output_aliases={5 + 2 * npg: 0},
        compiler_params=_cparams(("arbitrary", "arbitrary")),
        name="dsa_sample_attend",
    )(page_table, h, h, h, bias, *([cache_k] * npg), *([cache_v] * npg), mix)


def _moba_sample_k_kernel(pt_ref, q_ref, kn_ref, *refs, npg, n_steps, ts, rq):
    del pt_ref
    kp = refs[:npg]
    p_ref, pn_ref, l_ref = refs[npg:npg + 3]
    q_s, lg_s, km_s = refs[npg + 3:]
    j = pl.program_id(1)
    ck = npg * PAGE_SIZE
    blk = MOBA_BLOCK
    bps = ck // blk
    nbk = n_steps * bps
    scale = HEAD_DIM ** -0.5

    @pl.when(j == 0)
    def _init():
        q = q_ref[...]
        for c in range(B_KV_HEADS):
            q_s[c] = _stack_heads_rows(q, c, B_GROUP, ts, rq).astype(BF16)
        km_s[...] = jnp.zeros_like(km_s)

    for c in range(B_KV_HEADS):
        kc = _gather_head(kp, c, B_KV_HEADS)
        means = jnp.concatenate(
            [jnp.mean(kc[n * blk:(n + 1) * blk], axis=0, keepdims=True) for n in range(bps)], axis=0)
        km_s[c, pl.ds(pl.multiple_of(j * bps, bps), bps), :] = means
        lg_s[c, :, pl.ds(pl.multiple_of(j * ck, ck), ck)] = (
            lax.dot_general(q_s[c], kc.astype(BF16), _NT, preferred_element_type=F32) * scale)

    @pl.when(j == n_steps - 1)
    def _select_softmax():
        key_i = lax.broadcasted_iota(I32, (rq, LANES), 1)
        tok = lax.broadcasted_iota(I32, (rq, LANES), 0) % ts
        new_ok = jnp.logical_and(key_i <= tok, key_i < ts)
        blk_row = lax.broadcasted_iota(I32, (LANES, ck), 0)
        blk_of_key = lax.broadcasted_iota(I32, (LANES, ck), 1) // blk
        for c in range(B_KV_HEADS):
            cols = slice(c * HEAD_DIM, (c + 1) * HEAD_DIM)
            gate_t = lax.dot_general(km_s[c].astype(BF16), _pad_rows(q_s[c], LANES), _NT, preferred_element_type=F32)
            nbk8 = -(-nbk // SUBLANES) * SUBLANES
            sel_bias = _pad_rows(_rank_select(gate_t[:nbk8], nbk, axis=0), LANES).T[:rq].astype(BF16)
            lg_new = lax.dot_general(q_s[c], _pad_rows(kn_ref[:, cols], LANES).astype(BF16), _NT,
                                     preferred_element_type=F32) * scale
            lg_new = jnp.where(new_ok, lg_new, NEG_INF)
            m = jnp.max(lg_new, axis=1, keepdims=True)
            for t in range(n_steps):
                spread = jnp.where(blk_row == blk_of_key + t * bps, 1.0, 0.0).astype(BF16)
                lanes = slice(t * ck, (t + 1) * ck)
                lg = lg_s[c, :, lanes] + jnp.dot(sel_bias, spread, preferred_element_type=F32)
                lg_s[c, :, lanes] = lg
                m = jnp.maximum(m, jnp.max(lg, axis=1, keepdims=True))
            p_new = jnp.exp(lg_new - m)
            l = jnp.sum(p_new, axis=1, keepdims=True)
            for t in range(n_steps):
                lanes = slice(t * ck, (t + 1) * ck)
                p = jnp.exp(lg_s[c, :, lanes] - m)
                p_ref[c, :, lanes] = p.astype(BF16)
                l = l + jnp.sum(p, axis=1, keepdims=True)
            pn_ref[c] = p_new
            l_ref[c] = jnp.broadcast_to(l, (rq, LANES))


def _moba_sample_v_kernel(pt_ref, vn_ref, p_ref, pn_ref, l_ref, *refs, npg, n_steps, ts):
    del pt_ref
    vp = refs[:npg]
    o_ref = refs[npg + 1]
    acc_s = refs[npg + 2]
    j = pl.program_id(1)

    @pl.when(j == 0)
    def _init():
        acc_s[...] = jnp.zeros_like(acc_s)

    for c in range(B_KV_HEADS):
        vc = _gather_head(vp, c, B_KV_HEADS)
        acc_s[c] += jnp.dot(p_ref[c], vc.astype(BF16), preferred_element_type=F32)

    @pl.when(j == n_steps - 1)
    def _finish():
        for c in range(B_KV_HEADS):
            cols = slice(c * HEAD_DIM, (c + 1) * HEAD_DIM)
            vn = _pad_rows(vn_ref[:, cols], LANES)
            acc = acc_s[c] + jnp.dot(pn_ref[c].astype(BF16), vn.astype(BF16), preferred_element_type=F32)
            _write_head_rows(acc / l_ref[c][:, 0:1], o_ref, c, B_GROUP, ts)


def moba_sample(page_table, h, cache_k, cache_v, mix, layer, n_prompt, db, ts):
    n_pages = page_table.shape[1]
    n_past = n_pages * PAGE_SIZE
    npg = min(KV_PAGES_PER_STEP, n_pages)
    n_steps = n_pages // npg
    ck = npg * PAGE_SIZE
    rq = _sample_rows(B_GROUP, ts)
    row_blk = n_prompt // ts
    page_shape = (PAGE_SIZE, B_KV_HEADS, HEAD_DIM)
    k_spec = pltpu.PrefetchScalarGridSpec(
        num_scalar_prefetch=1,
        grid=(db, n_steps),
        in_specs=[pl.BlockSpec((ts, B_DIM), lambda b, j, pt: (row_blk + b, H_BQ // B_DIM)),
                  pl.BlockSpec((ts, B_KV_DIM), lambda b, j, pt: (row_blk + b, H_BK // B_KV_DIM))]
        + _page_specs(npg, page_shape, layer, lambda j: j),
        out_specs=[pl.BlockSpec((None, B_KV_HEADS, rq, n_past), lambda b, j, pt: (b, 0, 0, 0)),
                   pl.BlockSpec((None, B_KV_HEADS, rq, LANES), lambda b, j, pt: (b, 0, 0, 0)),
                   pl.BlockSpec((None, B_KV_HEADS, rq, LANES), lambda b, j, pt: (b, 0, 0, 0))],
        scratch_shapes=[pltpu.VMEM((B_KV_HEADS, rq, HEAD_DIM), BF16),
                        pltpu.VMEM((B_KV_HEADS, rq, n_past), F32),
                        pltpu.VMEM((B_KV_HEADS, LANES, HEAD_DIM), F32)],
    )
    p, pn, l = pl.pallas_call(
        functools.partial(_moba_sample_k_kernel, npg=npg, n_steps=n_steps, ts=ts, rq=rq),
        grid_spec=k_spec,
        out_shape=[jax.ShapeDtypeStruct((db, B_KV_HEADS, rq, n_past), BF16),
                   jax.ShapeDtypeStruct((db, B_KV_HEADS, rq, LANES), F32),
                   jax.ShapeDtypeStruct((db, B_KV_HEADS, rq, LANES), F32)],
        compiler_params=_cparams(("arbitrary", "arbitrary")),
        name="moba_sample_k",
    )(page_table, h, h, *([cache_k] * npg))
    v_spec = pltpu.PrefetchScalarGridSpec(
        num_scalar_prefetch=1,
        grid=(db, n_steps),
        in_specs=[pl.BlockSpec((ts, B_KV_DIM), lambda b, j, pt: (row_blk + b, H_BV // B_KV_DIM)),
                  pl.BlockSpec((None, B_KV_HEADS, rq, ck), lambda b, j, pt: (b, 0, 0, j)),
                  pl.BlockSpec((None, B_KV_HEADS, rq, LANES), lambda b, j, pt: (b, 0, 0, 0)),
                  pl.BlockSpec((None, B_KV_HEADS, rq, LANES), lambda b, j, pt: (b, 0, 0, 0))]
        + _page_specs(npg, page_shape, layer, lambda j: j)
        + [pl.BlockSpec(memory_space=pl.ANY)],
        out_specs=pl.BlockSpec((ts, B_DIM), lambda b, j, pt: (row_blk + b, A_DIM // B_DIM)),
        scratch_shapes=[pltpu.VMEM((B_KV_HEADS, rq, HEAD_DIM), F32)],
    )
    return pl.pallas_call(
        functools.partial(_moba_sample_v_kernel, npg=npg, n_steps=n_steps, ts=ts),
        grid_spec=v_spec,
        out_shape=jax.ShapeDtypeStruct(mix.shape, mix.dtype),
        input_output_aliases={5 + npg: 0},
        compiler_params=_cparams(("arbitrary", "arbitrary")),
        name="moba_sample_v",
    )(page_table, h, p, pn, l, *([cache_v] * npg), mix)


def _row_tile(n_rows, target):
    best = None
    for t in range(16, target + 1, 16):
        if n_rows % t == 0:
            best = t
    assert best is not None, n_rows
    return best


_DST = dict(aq=H_AQ, bq=H_BQ, iq=H_IQ, pu=H_PU, ak=H_AK, av=H_AV, bk=H_BK, bv=H_BV, ik=H_IKW, iw=H_IKW + IDX_DIM)
_H_USED = H_IKW + IDX_DIM + IDX_HEADS
PACK_ROWS = 256


def _pack_w_in_kernel(w_ref, o_ref):
    for name, dst in _DST.items():
        src, width = _SRC[name]
        o_ref[:, dst:dst + width] = w_ref[:, src:src + width].astype(BF16)
    o_ref[:, _H_USED:] = jnp.zeros((o_ref.shape[0], H_DIM - _H_USED), BF16)


def _pack_w_in(w_in):
    depth, d, n_in = w_in.shape
    rows = min(PACK_ROWS, d)
    return pl.pallas_call(
        _pack_w_in_kernel,
        grid=(depth, d // rows),
        in_specs=[pl.BlockSpec((None, rows, n_in), lambda l, i: (l, i, 0))],
        out_specs=pl.BlockSpec((None, rows, H_DIM), lambda l, i: (l, i, 0)),
        out_shape=jax.ShapeDtypeStruct((depth, d, H_DIM), BF16),
        compiler_params=_cparams(("parallel", "parallel")),
        name="pack_w_in",
    )(w_in)


def kernel(x_prompt, x_sample, cache_a_k, cache_a_v, cache_a_idx, cache_b_k, cache_b_v, state_pool,
           cache_mem_k, cache_mem_v, page_table, mem_prompt, w_in, w_out, pool_w, pool_scale, ln1_g, ln1_b,
           w_mem_q, w_mem_k, w_mem_v, w_mem_o, ln2_g, ln2_b, w_ffn_1, w_ffn_3, w_ffn_2, ln3_g, ln3_b):
    batch, seq, d = x_prompt.shape
    db, ts, _ = x_sample.shape
    depth = w_in.shape[0]
    n_past = page_table.shape[1] * PAGE_SIZE
    n_prompt = batch * seq
    n_rows = n_prompt + db * ts
    tm = _row_tile(n_rows, 700)
    tm_p = _row_tile(n_prompt, 1100)

    w_in_p = _pack_w_in(w_in)
    w_out_b = w_out.astype(BF16)
    w_mq_b = w_mem_q.astype(BF16)
    w_mkv_b = jnp.concatenate([w_mem_k, w_mem_v], axis=-1).astype(BF16)
    w_mo_b = w_mem_o.astype(BF16)
    w1_b, w3_b, w2_b = w_ffn_1.astype(BF16), w_ffn_3.astype(BF16), w_ffn_2.astype(BF16)
    pool_w_b = pool_w.astype(BF16)
    pool_scale3 = pool_scale.reshape(depth, 1, POOL_DIM)
    ln = [a.reshape(depth, 1, d) for a in (ln1_g, ln1_b, ln2_g, ln2_b, ln3_g, ln3_b)]
    mem = mem_prompt.reshape(batch * MEM_TOKENS, d)
    idx_t = jnp.swapaxes(cache_a_idx, 2, 3)

    x = jnp.concatenate([x_prompt.reshape(n_prompt, d), x_sample.reshape(db * ts, d)], axis=0)
    outs = {k: [] for k in ("pa_i", "p_pool", "pm_k", "pm_v", "sa_k", "sa_v", "sa_i", "sb_k", "sb_v", "s_pool")}
    kv_new = None
    for l in range(depth):
        h, kv_new = project_in(x, w_in_p, l, tm_p, n_prompt, kv_new)
        h = project_rows(x, w_in_p, l, h, n_prompt // (db * ts), db * ts, H_TN)
        mkv = matmul(mem, w_mkv_b, l, _row_tile(mem.shape[0], 512), MEM_DIM)

        mix = dsa_prompt(h, batch, seq, n_rows)
        mix = moba_prompt(h, mix, batch, seq)
        mix = pool_prompt(h, mix, pool_w_b, pool_scale3, l, batch, seq)
        bias = dsa_sample_select(page_table, h, idx_t, l, n_prompt, db, ts)
        mix = dsa_sample_attend(page_table, h, bias, cache_a_k, cache_a_v, mix, l, n_prompt, db, ts)
        mix = moba_sample(page_table, h, cache_b_k, cache_b_v, mix, l, n_prompt, db, ts)
        mix = pool_sample(h, mix, state_pool, pool_w_b, pool_scale3, l, n_prompt, db, ts, n_past)

        x1 = matmul_residual_ln(mix, w_out_b, x, ln[0], ln[1], l, tm // 2, MIX_DIM)
        q = matmul(x1, w_mq_b, l, tm, MEM_DIM)
        o = memattn_prompt(q, mkv, batch, seq)
        o = memattn_sample(q, cache_mem_k, cache_mem_v, o, l, n_prompt, db, ts)
        x2 = matmul_residual_ln(o, w_mo_b, x1, ln[2], ln[3], l, tm, MEM_DIM)
        x = ffn_ln(x2, w1_b, w3_b, w2_b, ln[4], ln[5], l, tm, 512)

        hp, hs = h[:n_prompt], h[n_prompt:]

        def cols(a, off, width, lead):
            return a[:, off:off + width].reshape(lead)

        kv_s = (db, ts, A_KV_HEADS, HEAD_DIM)
        outs["pa_i"].append(cols(hp, H_IKW, IDX_DIM, (batch, seq, IDX_DIM)))
        outs["p_pool"].append(jnp.stack(
            [h[(b + 1) * seq - POOL_STATE:(b + 1) * seq, H_PU:H_PU + POOL_DIM] for b in range(batch)]))
        outs["pm_k"].append(mkv[:, :MEM_DIM].reshape(batch, MEM_TOKENS, MEM_HEADS, MEM_HEAD_DIM))
        outs["pm_v"].append(mkv[:, MEM_DIM:].reshape(batch, MEM_TOKENS, MEM_HEADS, MEM_HEAD_DIM))
        outs["sa_k"].append(cols(hs, H_AK, A_KV_DIM, kv_s))
        outs["sa_v"].append(cols(hs, H_AV, A_KV_DIM, kv_s))
        outs["sa_i"].append(cols(hs, H_IKW, IDX_DIM, (db, ts, IDX_DIM)))
        outs["sb_k"].append(cols(hs, H_BK, B_KV_DIM, kv_s))
        outs["sb_v"].append(cols(hs, H_BV, B_KV_DIM, kv_s))
        pu_s = cols(hs, H_PU, POOL_DIM, (db, ts, POOL_DIM))
        outs["s_pool"].append(jnp.concatenate([state_pool[l], pu_s], axis=1)[:, -POOL_STATE:])

    st = {k: jnp.stack(v) for k, v in outs.items()}
    for name, arr in zip(("pa_k", "pa_v", "pb_k", "pb_v"), kv_new):
        st[name] = arr.reshape(depth, batch, seq, A_KV_HEADS, HEAD_DIM)
    return (x[:n_prompt].reshape(batch, seq, d), x[n_prompt:].reshape(db, ts, d),
            st["pa_k"], st["pa_v"], st["pa_i"], st["pb_k"], st["pb_v"], st["p_pool"], st["pm_k"], st["pm_v"],
            st["sa_k"], st["sa_v"], st["sa_i"], st["sb_k"], st["sb_v"], st["s_pool"])
```

```python
import functools
import math

import jax
import jax.numpy as jnp
from jax import lax
from jax.experimental import pallas as pl
from jax.experimental.pallas import tpu as pltpu

F32 = jnp.float32
BF16 = jnp.bfloat16
I32 = jnp.int32

D_MODEL = 2048
DEPTH = 4
PAGE_SIZE = 128
HEAD_DIM = 128
A_HEADS = 6
A_KV_HEADS = 2
A_GROUP = A_HEADS // A_KV_HEADS
IDX_HEADS = 8
IDX_DIM = 64
DSA_TOPK = 256
B_HEADS = 6
B_KV_HEADS = 2
B_GROUP = B_HEADS // B_KV_HEADS
MOBA_BLOCK = 256
MOBA_TOPK = 3
POOL_WINDOWS = (2, 4, 8, 16)
POOL_GROUP_DIM = 128
POOL_DIM = len(POOL_WINDOWS) * POOL_GROUP_DIM
POOL_STATE = max(POOL_WINDOWS) - 1
A_DIM = A_HEADS * HEAD_DIM
A_KV_DIM = A_KV_HEADS * HEAD_DIM
B_DIM = B_HEADS * HEAD_DIM
B_KV_DIM = B_KV_HEADS * HEAD_DIM
MIX_DIM = A_DIM + B_DIM + POOL_DIM
MEM_TOKENS = 256
MEM_HEADS = 4
MEM_HEAD_DIM = 128
MEM_DIM = MEM_HEADS * MEM_HEAD_DIM
FFN_HIDDEN = -(-(8 * D_MODEL) // (3 * 256)) * 256
ALPHA = (2.0 * DEPTH) ** 0.25
LN_EPS = 1e-5
NEG_INF = -1e30
INT_MIN = -(2 ** 31)
SOFTMAX_C = (HEAD_DIM ** -0.5) * math.log2(math.e)

LANES = 128
SUBLANES = 8
VMEM_LIMIT_BYTES = 56 * 1024 * 1024

_SRC = dict(aq=(0, 768), ak=(768, 256), av=(1024, 256), iq=(1280, 512), ik=(1792, 64), iw=(1856, 8),
            bq=(1864, 768), bk=(2632, 256), bv=(2888, 256), pu=(3144, 512))
H_AQ, H_BQ, H_IQ, H_PU, H_AK, H_AV, H_BK, H_BV, H_IKW = 0, 768, 1536, 2048, 2560, 2816, 3072, 3328, 3584
H_DIM = 3840

DSA_TQ = 128
DSA_CK = 256
DSA_CA = 512
MOBA_TQ = 128
POOL_TT = 512
MEM_TR = 512
IDX_PAGES_PER_STEP = 64
KV_PAGES_PER_STEP = 32


def _cparams(sem):
    return pltpu.CompilerParams(dimension_semantics=sem, vmem_limit_bytes=VMEM_LIMIT_BYTES)


def _layer_norm_rows(y, g, b):
    mu = jnp.mean(y, axis=-1, keepdims=True)
    yc = y - mu
    var = jnp.mean(yc * yc, axis=-1, keepdims=True)
    return yc * lax.rsqrt(var + LN_EPS) * g + b


def _mm_kernel(x_ref, w_ref, o_ref, xb_ref):
    @pl.when(pl.program_id(1) == 0)
    def _():
        xb_ref[...] = x_ref[...].astype(BF16)

    o_ref[...] = jnp.dot(xb_ref[...], w_ref[...], preferred_element_type=F32)


def matmul(x, w, layer, tm, tn):
    m, k = x.shape
    n = w.shape[2]
    return pl.pallas_call(
        _mm_kernel,
        grid=(m // tm, n // tn),
        in_specs=[pl.BlockSpec((tm, k), lambda i, j: (i, 0)),
                  pl.BlockSpec((None, k, tn), lambda i, j: (layer, 0, j))],
        out_specs=pl.BlockSpec((tm, tn), lambda i, j: (i, j)),
        out_shape=jax.ShapeDtypeStruct((m, n), F32),
        scratch_shapes=[pltpu.VMEM((tm, k), BF16)],
        compiler_params=_cparams(("parallel", "arbitrary")),
        name="matmul",
    )(x, w)


def _mm_rows_kernel(x_ref, w_ref, o_in_ref, o_ref):
    del o_in_ref
    o_ref[...] = jnp.dot(x_ref[...].astype(BF16), w_ref[...], preferred_element_type=F32)


def project_rows(x, w, layer, out, row_block, rows, tn):
    k = x.shape[1]
    n = w.shape[2]
    return pl.pallas_call(
        _mm_rows_kernel,
        grid=(n // tn,),
        in_specs=[pl.BlockSpec((rows, k), lambda j: (row_block, 0)),
                  pl.BlockSpec((None, k, tn), lambda j: (layer, 0, j)),
                  pl.BlockSpec(memory_space=pl.ANY)],
        out_specs=pl.BlockSpec((rows, tn), lambda j: (row_block, j)),
        out_shape=jax.ShapeDtypeStruct(out.shape, out.dtype),
        input_output_aliases={2: 0},
        compiler_params=_cparams(("arbitrary",)),
        name="project_rows",
    )(x, w, out)


H_TN = 1280
_KV_PIECES = (H_AK, H_AV, H_BK, H_BV)


def _store_head_rows(ref, c, rows, n_heads, value):
    ref.reshape(rows * n_heads, HEAD_DIM)[pl.ds(c, rows, stride=n_heads), :] = value


def _project_in_kernel(x_ref, w_ref, *refs, tm):
    o_ref = refs[-6]
    kv_refs = refs[-5:-1]
    xb_ref = refs[-1]
    j = pl.program_id(1)

    @pl.when(j == 0)
    def _():
        xb_ref[...] = x_ref[...].astype(BF16)

    o_ref[...] = jnp.dot(xb_ref[...], w_ref[...], preferred_element_type=F32)
    for piece, kv_ref in zip(_KV_PIECES, kv_refs):
        step, off = divmod(piece, H_TN)

        @pl.when(j == step)
        def _(kv_ref=kv_ref, off=off):
            for c in range(A_KV_HEADS):
                lo = off + c * HEAD_DIM
                _store_head_rows(kv_ref, c, tm, A_KV_HEADS, o_ref[:, lo:lo + HEAD_DIM])


def project_in(x, w, layer, tm, n_prompt, kv_prev):
    m, k = x.shape
    depth, _, n = w.shape
    kv_shape = jax.ShapeDtypeStruct((depth, n_prompt, A_KV_HEADS, HEAD_DIM), F32)
    n_prev = 0 if kv_prev is None else len(kv_prev)
    kv_spec = pl.BlockSpec((None, tm, A_KV_HEADS, HEAD_DIM), lambda i, j: (layer, i, 0, 0))
    outs = pl.pallas_call(
        functools.partial(_project_in_kernel, tm=tm),
        grid=(n_prompt // tm, n // H_TN),
        in_specs=[pl.BlockSpec((tm, k), lambda i, j: (i, 0)),
                  pl.BlockSpec((None, k, H_TN), lambda i, j: (layer, 0, j))]
        + [pl.BlockSpec(memory_space=pl.ANY)] * n_prev,
        out_specs=[pl.BlockSpec((tm, H_TN), lambda i, j: (i, j))] + [kv_spec] * len(_KV_PIECES),
        out_shape=[jax.ShapeDtypeStruct((m, n), F32)] + [kv_shape] * len(_KV_PIECES),
        input_output_aliases={2 + i: 1 + i for i in range(n_prev)},
        scratch_shapes=[pltpu.VMEM((tm, k), BF16)],
        compiler_params=_cparams(("parallel", "arbitrary")),
        name="project_in",
    )(x, w, *(kv_prev or ()))
    return outs[0], tuple(outs[1:])


def _mm_res_ln_kernel(a_ref, w_ref, x_ref, g_ref, b_ref, o_ref, acc_ref, *, nk):
    kk = pl.program_id(1)

    @pl.when(kk == 0)
    def _():
        acc_ref[...] = jnp.zeros_like(acc_ref)

    acc_ref[...] += jnp.dot(a_ref[...].astype(BF16), w_ref[...], preferred_element_type=F32)

    @pl.when(kk == nk - 1)
    def _():
        y = ALPHA * x_ref[...] + acc_ref[...]
        o_ref[...] = _layer_norm_rows(y, g_ref[...], b_ref[...])


def matmul_residual_ln(a, w, x, g, b, layer, tm, tk):
    m, k = a.shape
    n = w.shape[2]
    nk = k // tk
    return pl.pallas_call(
        functools.partial(_mm_res_ln_kernel, nk=nk),
        grid=(m // tm, nk),
        in_specs=[pl.BlockSpec((tm, tk), lambda i, j: (i, j)),
                  pl.BlockSpec((None, tk, n), lambda i, j: (layer, j, 0)),
                  pl.BlockSpec((tm, n), lambda i, j: (i, 0)),
                  pl.BlockSpec((None, 1, n), lambda i, j: (layer, 0, 0)),
                  pl.BlockSpec((None, 1, n), lambda i, j: (layer, 0, 0))],
        out_specs=pl.BlockSpec((tm, n), lambda i, j: (i, 0)),
        out_shape=jax.ShapeDtypeStruct((m, n), F32),
        scratch_shapes=[pltpu.VMEM((tm, n), F32)],
        compiler_params=_cparams(("parallel", "arbitrary")),
        name="matmul_residual_ln",
    )(a, w, x, g, b)


def _ffn_kernel(x_ref, w1_ref, w3_ref, w2_ref, g_ref, b_ref, o_ref, xb_ref, acc_ref, *, nf):
    j = pl.program_id(1)

    @pl.when(j == 0)
    def _():
        xb_ref[...] = x_ref[...].astype(BF16)
        acc_ref[...] = jnp.zeros_like(acc_ref)

    xb = xb_ref[...]
    h1 = jnp.dot(xb, w1_ref[...], preferred_element_type=F32)
    h3 = jnp.dot(xb, w3_ref[...], preferred_element_type=F32)
    h = (h1 / (1.0 + jnp.exp(-h1))) * h3
    acc_ref[...] += jnp.dot(h.astype(BF16), w2_ref[...], preferred_element_type=F32)

    @pl.when(j == nf - 1)
    def _():
        y = ALPHA * x_ref[...] + acc_ref[...]
        o_ref[...] = _layer_norm_rows(y, g_ref[...], b_ref[...])


def ffn_ln(x, w1, w3, w2, g, b, layer, tm, tf):
    m, d = x.shape
    f = w1.shape[2]
    nf = f // tf
    return pl.pallas_call(
        functools.partial(_ffn_kernel, nf=nf),
        grid=(m // tm, nf),
        in_specs=[pl.BlockSpec((tm, d), lambda i, j: (i, 0)),
                  pl.BlockSpec((None, d, tf), lambda i, j: (layer, 0, j)),
                  pl.BlockSpec((None, d, tf), lambda i, j: (layer, 0, j)),
                  pl.BlockSpec((None, tf, d), lambda i, j: (layer, j, 0)),
                  pl.BlockSpec((None, 1, d), lambda i, j: (layer, 0, 0)),
                  pl.BlockSpec((None, 1, d), lambda i, j: (layer, 0, 0))],
        out_specs=pl.BlockSpec((tm, d), lambda i, j: (i, 0)),
        out_shape=jax.ShapeDtypeStruct((m, d), F32),
        scratch_shapes=[pltpu.VMEM((tm, d), BF16), pltpu.VMEM((tm, d), F32)],
        compiler_params=_cparams(("parallel", "arbitrary")),
        name="ffn_ln",
    )(x, w1, w3, w2, g, b)


def _sortable_key(score):
    score = jnp.where(score == 0.0, 0.0, score)
    bits = pltpu.bitcast(score, I32)
    return jnp.where(bits < 0, bits ^ 0x7FFFFFFF, bits)


def _kth_largest(count_ge, shape, k_sel):
    c0 = count_ge(jnp.zeros(shape, I32))
    ok0 = c0 >= k_sel
    base = jnp.where(ok0, 0, INT_MIN).astype(I32)
    cnt = jnp.where(ok0, c0, jnp.iinfo(jnp.int32).max).astype(I32)

    def bit_body(i, carry):
        base, cnt = carry
        cand = base | jnp.left_shift(jnp.int32(1), 30 - i)
        c = count_ge(cand)
        ok = c >= k_sel
        return jnp.where(ok, cand, base), jnp.where(ok, c, cnt)

    return lax.fori_loop(0, 31, bit_body, (base, cnt))


def _flash_step_t(kc, vt, qt, bias, m, l, acc):
    lg = jnp.dot(kc, qt, preferred_element_type=F32) + bias
    m_new = jnp.maximum(m, jnp.max(lg, axis=0, keepdims=True))
    alpha = jnp.exp2(m - m_new)
    p = jnp.exp2(lg - m_new)
    l_new = alpha * l + jnp.sum(p, axis=0, keepdims=True)
    acc_new = alpha * acc + jnp.dot(vt, p.astype(BF16), preferred_element_type=F32)
    return m_new, l_new, acc_new


def _prep_kv_t(k_ref, v_ref, k_s, vt_s, rows, n_heads):
    kk = k_ref[rows, :]
    vv = v_ref[rows, :]
    for c in range(n_heads):
        cols = slice(c * HEAD_DIM, (c + 1) * HEAD_DIM)
        k_s[c, rows, :] = kk[:, cols].astype(BF16)
        vt_s[c, :, rows] = vv[:, cols].T.astype(BF16)
    return kk


def _stage_queries_t(q, qt_s, n_kv, group):
    q = q * SOFTMAX_C
    for c in range(n_kv):
        qt_s[c] = jnp.concatenate(
            [q[:, (c * group + g) * HEAD_DIM:(c * group + g + 1) * HEAD_DIM].T for g in range(group)],
            axis=1).astype(BF16)


def _head_q(qt_s, hh, tq):
    c, g = divmod(hh, A_GROUP)
    return qt_s[c, :, g * tq:(g + 1) * tq]


def _write_heads(acc_s, ls, o_ref, n_heads):
    for hh in range(n_heads):
        o_ref[:, hh * HEAD_DIM:(hh + 1) * HEAD_DIM] = (acc_s[hh] / ls[hh]).T


def _dsa_prompt_kernel(aq_ref, iq_ref, ikw_ref, ak_ref, av_ref, o_ref,
                       ik_s, k_s, vt_s, iqt_s, iwt_s, qt_s, keys_s, bias_s, acc_s, *, seq, k_sel):
    tq, ck = DSA_TQ, DSA_CK
    qi = pl.program_id(1)
    q0 = qi * tq
    prep_rows = min(512, seq)

    @pl.when(qi == 0)
    def _prep():
        lane = lax.broadcasted_iota(I32, (prep_rows, LANES), 1)

        def body(r, _):
            rows = pl.ds(pl.multiple_of(r * prep_rows, prep_rows), prep_rows)
            ik_s[rows, :] = jnp.where(lane < IDX_DIM, ikw_ref[rows, :], 0.0).astype(BF16)
            _prep_kv_t(ak_ref, av_ref, k_s, vt_s, rows, A_KV_HEADS)
            return 0

        lax.fori_loop(0, seq // prep_rows, body, 0)

    n_ck = (q0 + tq + ck - 1) // ck
    ikw_q = ikw_ref[pl.ds(pl.multiple_of(q0, tq), tq), :]
    iwt_s[...] = ikw_q.T[IDX_DIM:IDX_DIM + IDX_HEADS] * (IDX_HEADS ** -0.5)
    iqt = (iq_ref[...] * (IDX_DIM ** -0.5)).T
    zpad = jnp.zeros((LANES - IDX_DIM, 2 * tq), F32)
    for hp in range(IDX_HEADS // 2):
        pair = jnp.concatenate([iqt[(2 * hp) * IDX_DIM:(2 * hp + 1) * IDX_DIM],
                                iqt[(2 * hp + 1) * IDX_DIM:(2 * hp + 2) * IDX_DIM]], axis=1)
        iqt_s[hp] = jnp.concatenate([pair, zpad], axis=0).astype(BF16)
    _stage_queries_t(aq_ref[...], qt_s, A_KV_HEADS, A_GROUP)

    qpos = q0 + lax.broadcasted_iota(I32, (ck, tq), 1)
    krow = lax.broadcasted_iota(I32, (ck, tq), 0)

    def score_chunk(off):
        ikc = ik_s[pl.ds(off, ck), :]
        acc = jnp.zeros((ck, tq), F32)
        for hp in range(IDX_HEADS // 2):
            s = jnp.dot(ikc, iqt_s[hp], preferred_element_type=F32)
            acc = acc + (jnp.maximum(s[:, :tq], 0.0) * iwt_s[2 * hp:2 * hp + 1, :]
                         + jnp.maximum(s[:, tq:], 0.0) * iwt_s[2 * hp + 1:2 * hp + 2, :])
        keys_s[pl.ds(off, ck), :] = jnp.where(off + krow <= qpos, _sortable_key(acc), INT_MIN)

    def score_body(c, _):
        off = pl.multiple_of(c * (2 * ck), 2 * ck)
        score_chunk(off)
        score_chunk(off + ck)
        return 0

    lax.fori_loop(0, (n_ck + 1) // 2, score_body, 0)

    part = ck // 4

    def count_ge(cand):
        def body(c, acc):
            ind = jnp.where(keys_s[pl.ds(pl.multiple_of(c * ck, ck), ck), :] >= cand, 1, 0).astype(I32)
            return acc + ((ind[0:part] + ind[part:2 * part]) + (ind[2 * part:3 * part] + ind[3 * part:]))

        acc = lax.fori_loop(0, n_ck, body, jnp.zeros((part, tq), I32))
        return jnp.sum(acc, axis=0, keepdims=True)

    thr, cnt_thr = _kth_largest(count_ge, (1, tq), k_sel)
    has_tie = jnp.logical_and(cnt_thr > k_sel, thr > INT_MIN)
    any_tie = jnp.max(has_tie.astype(I32)) > 0
    thr_c = jnp.maximum(thr, INT_MIN + 1)

    @pl.when(jnp.logical_not(any_tie))
    def _fast():
        def body(c, _):
            rows = pl.ds(pl.multiple_of(c * ck, ck), ck)
            bias_s[rows, :] = jnp.where(keys_s[rows, :] >= thr_c, 0.0, NEG_INF)
            return 0

        lax.fori_loop(0, n_ck, body, 0)

    @pl.when(any_tie)
    def _ties():
        need = (k_sel - count_ge(thr_c + 1)).astype(F32)
        tri = jnp.where(lax.broadcasted_iota(I32, (LANES, LANES), 0)
                        >= lax.broadcasted_iota(I32, (LANES, LANES), 1), 1.0, 0.0).astype(BF16)

        def body(c, run):
            rows = pl.ds(pl.multiple_of(c * LANES, LANES), LANES)
            kchunk = keys_s[rows, :]
            eq = kchunk == thr_c
            eqf = jnp.where(eq, 1.0, 0.0)
            pre = jnp.dot(tri, eqf.astype(BF16), preferred_element_type=F32) + run
            sel = jnp.logical_or(kchunk > thr_c, jnp.logical_and(eq, pre <= need))
            bias_s[rows, :] = jnp.where(sel, 0.0, NEG_INF)
            return run + jnp.sum(eqf, axis=0, keepdims=True)

        lax.fori_loop(0, n_ck * (ck // LANES), body, jnp.zeros((1, tq), F32))

    ca = DSA_CA
    n_ca = (q0 + tq + ca - 1) // ca
    for extra in range(1, ca // ck):
        @pl.when(n_ck + extra <= n_ca * (ca // ck))
        def _mask_tail(extra=extra):
            bias_s[pl.ds(pl.multiple_of((n_ck + extra - 1) * ck, ck), ck), :] = jnp.full((ck, tq), NEG_INF, F32)

    acc_s[...] = jnp.zeros_like(acc_s)

    def att_body(j, carry):
        ms, ls = carry
        rows = pl.ds(pl.multiple_of(j * ca, ca), ca)
        bias = bias_s[rows, :]
        new_m, new_l = [], []
        for hh in range(A_HEADS):
            c = hh // A_GROUP
            m, l, acc = _flash_step_t(k_s[c, rows, :], vt_s[c, :, rows], _head_q(qt_s, hh, tq), bias,
                                      ms[hh], ls[hh], acc_s[hh])
            acc_s[hh] = acc
            new_m.append(m)
            new_l.append(l)
        return tuple(new_m), tuple(new_l)

    init = (tuple(jnp.full((1, tq), NEG_INF, F32) for _ in range(A_HEADS)),
            tuple(jnp.zeros((1, tq), F32) for _ in range(A_HEADS)))
    _, ls = lax.fori_loop(0, n_ca, att_body, init)
    _write_heads(acc_s, ls, o_ref, A_HEADS)


def dsa_prompt(h, batch, seq, n_rows):
    tq = DSA_TQ
    nq = seq // tq
    k_sel = min(DSA_TOPK, seq // 4)
    kern = functools.partial(_dsa_prompt_kernel, seq=seq, k_sel=k_sel)
    return pl.pallas_call(
        kern,
        grid=(batch, nq),
        in_specs=[pl.BlockSpec((tq, A_DIM), lambda b, i: (b * nq + i, H_AQ // A_DIM)),
                  pl.BlockSpec((tq, IDX_HEADS * IDX_DIM), lambda b, i: (b * nq + i, H_IQ // (IDX_HEADS * IDX_DIM))),
                  pl.BlockSpec((seq, LANES), lambda b, i: (b, H_IKW // LANES)),
                  pl.BlockSpec((seq, A_KV_DIM), lambda b, i: (b, H_AK // A_KV_DIM)),
                  pl.BlockSpec((seq, A_KV_DIM), lambda b, i: (b, H_AV // A_KV_DIM))],
        out_specs=pl.BlockSpec((tq, A_DIM), lambda b, i: (b * nq + i, 0)),
        out_shape=jax.ShapeDtypeStruct((n_rows, MIX_DIM), F32),
        scratch_shapes=[pltpu.VMEM((seq, LANES), BF16),
                        pltpu.VMEM((A_KV_HEADS, seq, HEAD_DIM), BF16),
                        pltpu.VMEM((A_KV_HEADS, HEAD_DIM, seq), BF16),
                        pltpu.VMEM((IDX_HEADS // 2, LANES, 2 * tq), BF16),
                        pltpu.VMEM((IDX_HEADS, tq), F32),
                        pltpu.VMEM((A_KV_HEADS, HEAD_DIM, A_GROUP * tq), BF16),
                        pltpu.VMEM((seq, tq), I32),
                        pltpu.VMEM((seq, tq), F32),
                        pltpu.VMEM((A_HEADS, HEAD_DIM, tq), F32)],
        compiler_params=_cparams(("arbitrary", "arbitrary")),
        name="dsa_prompt",
    )(h, h, h, h, h)


def _rank_select(gate, n_past_blocks, axis):
    idx = lax.broadcasted_iota(I32, gate.shape, axis)

    def body(m, rank):
        gm = jnp.sum(jnp.where(idx == m, gate, 0.0), axis=axis, keepdims=True)
        beats = jnp.logical_or(gm > gate, jnp.logical_and(gm == gate, m < idx))
        return rank + jnp.where(beats, 1, 0).astype(I32)

    rank = lax.fori_loop(0, n_past_blocks, body, jnp.zeros(gate.shape, I32))
    sel = jnp.logical_and(idx < n_past_blocks, rank < MOBA_TOPK)
    return jnp.where(sel, 0.0, NEG_INF)


def _moba_prompt_kernel(bq_ref, bk_ref, bv_ref, mix_ref, o_ref, k_s, vt_s, km_s, kmb_s, qt_s, sb_s, acc_s,
                        *, seq, nbp):
    del mix_ref
    tq, blk = MOBA_TQ, MOBA_BLOCK
    qi = pl.program_id(1)
    q0 = qi * tq
    nb = seq // blk

    @pl.when(qi == 0)
    def _prep():
        km_s[...] = jnp.zeros_like(km_s)

        def body(n, _):
            rows = pl.ds(pl.multiple_of(n * blk, blk), blk)
            kk = _prep_kv_t(bk_ref, bv_ref, k_s, vt_s, rows, B_KV_HEADS)
            km_s[pl.ds(n, 1), :] = jnp.mean(kk, axis=0, keepdims=True)
            return 0

        lax.fori_loop(0, nb, body, 0)
        for c in range(B_KV_HEADS):
            kmb_s[c] = km_s[:, c * HEAD_DIM:(c + 1) * HEAD_DIM].astype(BF16)

    own = q0 // blk
    own_rows = pl.ds(pl.multiple_of(own * blk, blk), blk)
    bq = bq_ref[...]
    _stage_queries_t(bq, qt_s, B_KV_HEADS, B_GROUP)
    gates = []
    for c in range(B_KV_HEADS):
        qg = jnp.concatenate(
            [bq[:, (c * B_GROUP + g) * HEAD_DIM:(c * B_GROUP + g + 1) * HEAD_DIM].T for g in range(B_GROUP)],
            axis=1).astype(BF16)
        gates.append(jnp.dot(kmb_s[c], qg, preferred_element_type=F32)[:nbp])
    sel_bias = _rank_select(jnp.concatenate(gates, axis=1), own, axis=0)
    for hh in range(B_HEADS):
        sb_s[hh] = sel_bias[:, hh * tq:(hh + 1) * tq]

    qpos = q0 + lax.broadcasted_iota(I32, (blk, tq), 1)
    kpos = own * blk + lax.broadcasted_iota(I32, (blk, tq), 0)
    causal_bias = jnp.where(kpos <= qpos, 0.0, NEG_INF)
    ms, ls = [], []
    for hh in range(B_HEADS):
        c = hh // B_GROUP
        m, l, acc = _flash_step_t(k_s[c, own_rows, :], vt_s[c, :, own_rows], _head_q(qt_s, hh, tq), causal_bias,
                                  jnp.full((1, tq), NEG_INF, F32), jnp.zeros((1, tq), F32),
                                  jnp.zeros((HEAD_DIM, tq), F32))
        acc_s[hh] = acc
        ms.append(m)
        ls.append(l)

    odd = own % 2

    def pair_body(p, carry):
        ms, ls = carry
        b0 = jnp.maximum(2 * p - odd, 0)
        rows = pl.ds(pl.multiple_of(b0 * blk, blk), 2 * blk)
        skip_second = jnp.where(jnp.logical_and(p == 0, odd == 1), NEG_INF, 0.0)
        new_m, new_l = [], []
        for hh in range(B_HEADS):
            c = hh // B_GROUP
            bias = jnp.concatenate(
                [jnp.broadcast_to(sb_s[hh, pl.ds(b0, 1), :], (blk, tq)),
                 jnp.broadcast_to(sb_s[hh, pl.ds(b0 + 1, 1), :] + skip_second, (blk, tq))], axis=0)
            m, l, acc = _flash_step_t(k_s[c, rows, :], vt_s[c, :, rows], _head_q(qt_s, hh, tq), bias,
                                      ms[hh], ls[hh], acc_s[hh])
            acc_s[hh] = acc
            new_m.append(m)
            new_l.append(l)
        return tuple(new_m), tuple(new_l)

    _, ls = lax.fori_loop(0, (own + 1) // 2, pair_body, (tuple(ms), tuple(ls)))
    _write_heads(acc_s, ls, o_ref, B_HEADS)


def moba_prompt(h, mix, batch, seq):
    tq = MOBA_TQ
    nq = seq // tq
    nbp = -(-(seq // MOBA_BLOCK) // SUBLANES) * SUBLANES
    return pl.pallas_call(
        functools.partial(_moba_prompt_kernel, seq=seq, nbp=nbp),
        grid=(batch, nq),
        in_specs=[pl.BlockSpec((tq, B_DIM), lambda b, i: (b * nq + i, H_BQ // B_DIM)),
                  pl.BlockSpec((seq, B_KV_DIM), lambda b, i: (b, H_BK // B_KV_DIM)),
                  pl.BlockSpec((seq, B_KV_DIM), lambda b, i: (b, H_BV // B_KV_DIM)),
                  pl.BlockSpec(memory_space=pl.ANY)],
        out_specs=pl.BlockSpec((tq, B_DIM), lambda b, i: (b * nq + i, A_DIM // B_DIM)),
        out_shape=jax.ShapeDtypeStruct(mix.shape, mix.dtype),
        input_output_aliases={3: 0},
        scratch_shapes=[pltpu.VMEM((B_KV_HEADS, seq, HEAD_DIM), BF16),
                        pltpu.VMEM((B_KV_HEADS, HEAD_DIM, seq), BF16),
                        pltpu.VMEM((LANES, B_KV_DIM), F32),
                        pltpu.VMEM((B_KV_HEADS, LANES, HEAD_DIM), BF16),
                        pltpu.VMEM((B_KV_HEADS, HEAD_DIM, B_GROUP * tq), BF16),
                        pltpu.VMEM((B_HEADS, nbp, tq), F32),
                        pltpu.VMEM((B_HEADS, HEAD_DIM, tq), F32)],
        compiler_params=_cparams(("arbitrary", "arbitrary")),
        name="moba_prompt",
    )(h, h, h, mix)


POOL_HALO = 16


def _pool_groups(u, window_sum, cnt_of, w_ref, sc_ref, o_ref):
    for g, w in enumerate(POOL_WINDOWS):
        cols = slice(g * POOL_GROUP_DIM, (g + 1) * POOL_GROUP_DIM)
        ug = u[:, cols]
        d = window_sum(g, w, ug) / cnt_of(w) - ug
        y = jnp.dot(d.astype(BF16), w_ref[g], preferred_element_type=F32)
        o_ref[:, cols] = y * sc_ref[:, cols]


def _pool_prompt_kernel(u_ref, halo_ref, w_ref, sc_ref, mix_ref, o_ref, ext_s):
    del mix_ref
    tt = POOL_TT
    i = pl.program_id(1)
    u = u_ref[...]
    ext_s[0:POOL_HALO, :] = jnp.where(i == 0, 0.0, halo_ref[...])
    ext_s[POOL_HALO:POOL_HALO + tt, :] = u
    t = i * tt + lax.broadcasted_iota(I32, (tt, 1), 0)

    def window_sum(g, w, ug):
        s = ug
        for k in range(1, w):
            s = s + ext_s[pl.ds(POOL_HALO - k, tt), g * POOL_GROUP_DIM:(g + 1) * POOL_GROUP_DIM]
        return s

    _pool_groups(u, window_sum, lambda w: jnp.minimum(w, t + 1).astype(F32), w_ref, sc_ref, o_ref)


def pool_prompt(h, mix, pool_w, pool_scale, layer, batch, seq):
    tt = POOL_TT
    nt = seq // tt
    halo_blocks = tt // POOL_HALO
    return pl.pallas_call(
        _pool_prompt_kernel,
        grid=(batch, nt),
        in_specs=[pl.BlockSpec((tt, POOL_DIM), lambda b, i: (b * nt + i, H_PU // POOL_DIM)),
                  pl.BlockSpec((POOL_HALO, POOL_DIM),
                               lambda b, i: (jnp.maximum((b * nt + i) * halo_blocks - 1, 0), H_PU // POOL_DIM)),
                  pl.BlockSpec((None, len(POOL_WINDOWS), POOL_GROUP_DIM, POOL_GROUP_DIM),
                               lambda b, i: (layer, 0, 0, 0)),
                  pl.BlockSpec((None, 1, POOL_DIM), lambda b, i: (layer, 0, 0)),
                  pl.BlockSpec(memory_space=pl.ANY)],
        out_specs=pl.BlockSpec((tt, POOL_DIM), lambda b, i: (b * nt + i, (A_DIM + B_DIM) // POOL_DIM)),
        out_shape=jax.ShapeDtypeStruct(mix.shape, mix.dtype),
        input_output_aliases={4: 0},
        scratch_shapes=[pltpu.VMEM((POOL_HALO + tt, POOL_DIM), F32)],
        compiler_params=_cparams(("arbitrary", "arbitrary")),
        name="pool_prompt",
    )(h, h, pool_w, pool_scale, mix)


def _pool_sample_kernel(u_ref, st_ref, w_ref, sc_ref, mix_ref, o_ref, ext_s, *, db, ts, n_past):
    del mix_ref
    u = u_ref[...]
    for b in range(db):
        ext_s[b, POOL_HALO - POOL_STATE:POOL_HALO, :] = st_ref[b]
        ext_s[b, POOL_HALO:POOL_HALO + ts, :] = u[b * ts:(b + 1) * ts]
    t = n_past + lax.broadcasted_iota(I32, (db * ts, 1), 0) % ts

    def window_sum(g, w, ug):
        parts = []
        for b in range(db):
            s = ug[b * ts:(b + 1) * ts]
            for k in range(1, w):
                s = s + ext_s[b, pl.ds(POOL_HALO - k, ts), g * POOL_GROUP_DIM:(g + 1) * POOL_GROUP_DIM]
            parts.append(s)
        return jnp.concatenate(parts, axis=0)

    _pool_groups(u, window_sum, lambda w: jnp.minimum(w, t + 1).astype(F32), w_ref, sc_ref, o_ref)


def pool_sample(h, mix, state_pool, pool_w, pool_scale, layer, n_prompt, db, ts, n_past):
    rows = db * ts
    return pl.pallas_call(
        functools.partial(_pool_sample_kernel, db=db, ts=ts, n_past=n_past),
        grid=(1,),
        in_specs=[pl.BlockSpec((rows, POOL_DIM), lambda i: (n_prompt // rows, H_PU // POOL_DIM)),
                  pl.BlockSpec((None, db, POOL_STATE, POOL_DIM), lambda i: (layer, 0, 0, 0)),
                  pl.BlockSpec((None, len(POOL_WINDOWS), POOL_GROUP_DIM, POOL_GROUP_DIM),
                               lambda i: (layer, 0, 0, 0)),
                  pl.BlockSpec((None, 1, POOL_DIM), lambda i: (layer, 0, 0)),
                  pl.BlockSpec(memory_space=pl.ANY)],
        out_specs=pl.BlockSpec((rows, POOL_DIM), lambda i: (n_prompt // rows, (A_DIM + B_DIM) // POOL_DIM)),
        out_shape=jax.ShapeDtypeStruct(mix.shape, mix.dtype),
        input_output_aliases={4: 0},
        scratch_shapes=[pltpu.VMEM((db, POOL_HALO + ts, POOL_DIM), F32)],
        compiler_params=_cparams(("arbitrary",)),
        name="pool_sample",
    )(h, state_pool, pool_w, pool_scale, mix)


def _memattn_heads(q_ref, head_k, head_v, o_ref):
    scale = MEM_HEAD_DIM ** -0.5
    for hh in range(MEM_HEADS):
        cols = slice(hh * MEM_HEAD_DIM, (hh + 1) * MEM_HEAD_DIM)
        qh = q_ref[:, cols].astype(BF16)
        lg = lax.dot_general(qh, head_k(hh).astype(BF16), (((1,), (1,)), ((), ())),
                             preferred_element_type=F32) * scale
        m = jnp.max(lg, axis=1, keepdims=True)
        p = jnp.exp(lg - m)
        l = jnp.sum(p, axis=1, keepdims=True)
        o_ref[:, cols] = jnp.dot(p.astype(BF16), head_v(hh).astype(BF16), preferred_element_type=F32) / l


def _memattn_prompt_kernel(q_ref, mk_ref, mv_ref, o_ref):
    _memattn_heads(q_ref, lambda hh: mk_ref[:, hh * MEM_HEAD_DIM:(hh + 1) * MEM_HEAD_DIM],
                   lambda hh: mv_ref[:, hh * MEM_HEAD_DIM:(hh + 1) * MEM_HEAD_DIM], o_ref)


def _memattn_sample_kernel(q_ref, mk_ref, mv_ref, o_in_ref, o_ref):
    del o_in_ref
    _memattn_heads(q_ref, lambda hh: _head_rows(mk_ref, hh, MEM_TOKENS, MEM_HEADS),
                   lambda hh: _head_rows(mv_ref, hh, MEM_TOKENS, MEM_HEADS), o_ref)


def memattn_prompt(q, mkv, batch, seq):
    tr = min(MEM_TR, seq)
    nt = seq // tr
    return pl.pallas_call(
        _memattn_prompt_kernel,
        grid=(batch, nt),
        in_specs=[pl.BlockSpec((tr, MEM_DIM), lambda b, i: (b * nt + i, 0)),
                  pl.BlockSpec((MEM_TOKENS, MEM_DIM), lambda b, i: (b, 0)),
                  pl.BlockSpec((MEM_TOKENS, MEM_DIM), lambda b, i: (b, 1))],
        out_specs=pl.BlockSpec((tr, MEM_DIM), lambda b, i: (b * nt + i, 0)),
        out_shape=jax.ShapeDtypeStruct(q.shape, F32),
        compiler_params=_cparams(("arbitrary", "arbitrary")),
        name="memattn_prompt",
    )(q, mkv, mkv)


def memattn_sample(q, cache_k, cache_v, o, layer, n_prompt, db, ts):
    cache_spec = pl.BlockSpec((None, None, MEM_TOKENS, MEM_HEADS, MEM_HEAD_DIM), lambda b: (layer, b, 0, 0, 0))
    return pl.pallas_call(
        _memattn_sample_kernel,
        grid=(db,),
        in_specs=[pl.BlockSpec((ts, MEM_DIM), lambda b: (n_prompt // ts + b, 0)),
                  cache_spec, cache_spec,
                  pl.BlockSpec(memory_space=pl.ANY)],
        out_specs=pl.BlockSpec((ts, MEM_DIM), lambda b: (n_prompt // ts + b, 0)),
        out_shape=jax.ShapeDtypeStruct(o.shape, o.dtype),
        input_output_aliases={3: 0},
        compiler_params=_cparams(("arbitrary",)),
        name="memattn_sample",
    )(q, cache_k, cache_v, o)


def _page_specs(npg, page_shape, layer, step_of):
    zeros = (0,) * len(page_shape)

    def spec(i):
        return pl.BlockSpec((None, None) + page_shape,
                            lambda b, j, pt: (layer, pt[b, step_of(j) * npg + i]) + zeros)
    return [spec(i) for i in range(npg)]


def _pad_rows(x, rows):
    return jnp.concatenate([x, jnp.zeros((rows - x.shape[0], x.shape[1]), x.dtype)], axis=0)


def _head_rows(ref, c, rows, n_heads):
    return ref.reshape(rows * n_heads, HEAD_DIM)[pl.ds(c, rows, stride=n_heads), :]


def _gather_head(pages, c, n_heads):
    return jnp.concatenate([_head_rows(p, c, PAGE_SIZE, n_heads) for p in pages], axis=0)


def _dsa_sample_select_kernel(pt_ref, iq_ref, ikw_ref, *refs, npg, n_steps, n_past, ts, k_sel):
    del pt_ref
    pages = refs[:npg]
    bias_ref, keys_s = refs[npg], refs[npg + 1]
    j = pl.program_id(1)
    ck = npg * PAGE_SIZE
    lp = n_past + LANES
    iq = iq_ref[...] * (IDX_DIM ** -0.5)
    iq_st = jnp.concatenate([iq[:, h * IDX_DIM:(h + 1) * IDX_DIM] for h in range(IDX_HEADS)],
                            axis=0).astype(BF16)
    ikw = ikw_ref[...]
    iw = ikw[:, IDX_DIM:IDX_DIM + IDX_HEADS] * (IDX_HEADS ** -0.5)

    def scores(ik_t):
        s = jnp.dot(iq_st, ik_t.astype(BF16), preferred_element_type=F32)
        acc = jnp.zeros((ts, ik_t.shape[1]), F32)
        for h in range(IDX_HEADS):
            acc = acc + jnp.maximum(s[h * ts:(h + 1) * ts], 0.0) * iw[:, h:h + 1]
        return acc

    ik_past = jnp.concatenate([pages[i][...] for i in range(npg)], axis=1)
    keys_s[:, pl.ds(pl.multiple_of(j * ck, ck), ck)] = _sortable_key(scores(ik_past))

    @pl.when(j == n_steps - 1)
    def _select():
        sc_new = scores(_pad_rows(ikw, LANES).T[:IDX_DIM])
        causal = lax.broadcasted_iota(I32, (ts, LANES), 1) <= lax.broadcasted_iota(I32, (ts, LANES), 0)
        keys_s[:, n_past:lp] = jnp.where(causal, _sortable_key(sc_new), INT_MIN)

        def count_ge(cand):
            return jnp.sum(jnp.where(keys_s[...] >= cand, 1, 0).astype(I32), axis=1, keepdims=True)

        thr, cnt_thr = _kth_largest(count_ge, (ts, 1), k_sel)
        has_tie = jnp.logical_and(cnt_thr > k_sel, thr > INT_MIN)
        any_tie = jnp.max(has_tie.astype(I32)) > 0
        thr_c = jnp.maximum(thr, INT_MIN + 1)

        @pl.when(jnp.logical_not(any_tie))
        def _fast():
            bias_ref[...] = jnp.where(keys_s[...] >= thr_c, 0.0, NEG_INF)

        @pl.when(any_tie)
        def _ties():
            need = (k_sel - count_ge(thr_c + 1)).astype(F32)
            tri = jnp.where(lax.broadcasted_iota(I32, (LANES, LANES), 0)
                            <= lax.broadcasted_iota(I32, (LANES, LANES), 1), 1.0, 0.0).astype(BF16)

            def body(c, run):
                cols = pl.ds(pl.multiple_of(c * LANES, LANES), LANES)
                kchunk = keys_s[:, cols]
                eq = kchunk == thr_c
                eqf = jnp.where(eq, 1.0, 0.0)
                pre = jnp.dot(eqf.astype(BF16), tri, preferred_element_type=F32) + run
                sel = jnp.logical_or(kchunk > thr_c, jnp.logical_and(eq, pre <= need))
                bias_ref[:, cols] = jnp.where(sel, 0.0, NEG_INF)
                return run + jnp.sum(eqf, axis=1, keepdims=True)

            lax.fori_loop(0, lp // LANES, body, jnp.zeros((ts, 1), F32))


def dsa_sample_select(page_table, h, cache_idx, layer, n_prompt, db, ts):
    n_pages = page_table.shape[1]
    n_past = n_pages * PAGE_SIZE
    npg = min(IDX_PAGES_PER_STEP, n_pages)
    n_steps = n_pages // npg
    k_sel = min(DSA_TOPK, (n_past + ts) // 4)
    lp = n_past + LANES
    kern = functools.partial(_dsa_sample_select_kernel, npg=npg, n_steps=n_steps, n_past=n_past, ts=ts, k_sel=k_sel)
    row_blk = n_prompt // ts
    grid_spec = pltpu.PrefetchScalarGridSpec(
        num_scalar_prefetch=1,
        grid=(db, n_steps),
        in_specs=[pl.BlockSpec((ts, IDX_HEADS * IDX_DIM), lambda b, j, pt: (row_blk + b, H_IQ // (IDX_HEADS * IDX_DIM))),
                  pl.BlockSpec((ts, LANES), lambda b, j, pt: (row_blk + b, H_IKW // LANES))]
        + _page_specs(npg, (IDX_DIM, PAGE_SIZE), layer, lambda j: j),
        out_specs=pl.BlockSpec((None, ts, lp), lambda b, j, pt: (b, 0, 0)),
        scratch_shapes=[pltpu.VMEM((ts, lp), I32)],
    )
    return pl.pallas_call(
        kern,
        grid_spec=grid_spec,
        out_shape=jax.ShapeDtypeStruct((db, ts, lp), F32),
        compiler_params=_cparams(("arbitrary", "arbitrary")),
        name="dsa_sample_select",
    )(page_table, h, h, *([cache_idx] * npg))


_NT = (((1,), (1,)), ((), ()))


def _sample_rows(group, ts):
    return -(-(group * ts) // 16) * 16


def _stack_heads_rows(q, c, group, ts, rows):
    return jnp.concatenate(
        [q[:, (c * group + g) * HEAD_DIM:(c * group + g + 1) * HEAD_DIM] for g in range(group)]
        + [jnp.zeros((rows - group * ts, HEAD_DIM), q.dtype)], axis=0)


def _tile_rows(bs, group, rows):
    return jnp.concatenate([bs] * group + [jnp.zeros((rows - group * bs.shape[0], bs.shape[1]), bs.dtype)], axis=0)


def _write_head_rows(out, o_ref, c, group, ts):
    for g in range(group):
        hcol = (c * group + g) * HEAD_DIM
        o_ref[:, hcol:hcol + HEAD_DIM] = out[g * ts:(g + 1) * ts]


def _dsa_sample_attend_kernel(pt_ref, q_ref, kn_ref, vn_ref, bias_ref, *refs, npg, n_steps, n_past, ts, rq):
    del pt_ref
    kp, vp = refs[:npg], refs[npg:2 * npg]
    o_ref = refs[2 * npg + 1]
    q_s, m_s, l_s, acc_s = refs[2 * npg + 2:]
    j = pl.program_id(1)
    ck = npg * PAGE_SIZE
    scale = HEAD_DIM ** -0.5

    @pl.when(j == 0)
    def _init():
        q = q_ref[...]
        for c in range(A_KV_HEADS):
            q_s[c] = _stack_heads_rows(q, c, A_GROUP, ts, rq).astype(BF16)
        m_s[...] = jnp.full(m_s.shape, NEG_INF, F32)
        l_s[...] = jnp.zeros_like(l_s)
        acc_s[...] = jnp.zeros_like(acc_s)

    def update(c, kc, vc, bias_rows):
        lg = lax.dot_general(q_s[c], kc.astype(BF16), _NT, preferred_element_type=F32) * scale + bias_rows
        m_old = m_s[c]
        m_new = jnp.maximum(m_old, jnp.max(lg, axis=1, keepdims=True))
        alpha = jnp.exp(m_old - m_new)
        p = jnp.exp(lg - m_new)
        l_s[c] = alpha * l_s[c] + jnp.sum(p, axis=1, keepdims=True)
        acc_s[c] = alpha * acc_s[c] + jnp.dot(p.astype(BF16), vc.astype(BF16), preferred_element_type=F32)
        m_s[c] = m_new

    bias_rows = _tile_rows(bias_ref[:, pl.ds(pl.multiple_of(j * ck, ck), ck)], A_GROUP, rq)
    for c in range(A_KV_HEADS):
        update(c, _gather_head(kp, c, A_KV_HEADS), _gather_head(vp, c, A_KV_HEADS), bias_rows)

    @pl.when(j == n_steps - 1)
    def _finish():
        bias_n = _tile_rows(bias_ref[:, n_past:n_past + LANES], A_GROUP, rq)
        for c in range(A_KV_HEADS):
            cols = slice(c * HEAD_DIM, (c + 1) * HEAD_DIM)
            update(c, _pad_rows(kn_ref[:, cols], LANES), _pad_rows(vn_ref[:, cols], LANES), bias_n)
            _write_head_rows(acc_s[c] / l_s[c], o_ref, c, A_GROUP, ts)


def dsa_sample_attend(page_table, h, bias, cache_k, cache_v, mix, layer, n_prompt, db, ts):
    n_pages = page_table.shape[1]
    n_past = n_pages * PAGE_SIZE
    npg = min(KV_PAGES_PER_STEP, n_pages)
    n_steps = n_pages // npg
    lp = n_past + LANES
    rq = _sample_rows(A_GROUP, ts)
    kern = functools.partial(_dsa_sample_attend_kernel, npg=npg, n_steps=n_steps, n_past=n_past, ts=ts, rq=rq)
    row_blk = n_prompt // ts
    kv_tail = (PAGE_SIZE, A_KV_HEADS, HEAD_DIM)
    grid_spec = pltpu.PrefetchScalarGridSpec(
        num_scalar_prefetch=1,
        grid=(db, n_steps),
        in_specs=[pl.BlockSpec((ts, A_DIM), lambda b, j, pt: (row_blk + b, H_AQ // A_DIM)),
                  pl.BlockSpec((ts, A_KV_DIM), lambda b, j, pt: (row_blk + b, H_AK // A_KV_DIM)),
                  pl.BlockSpec((ts, A_KV_DIM), lambda b, j, pt: (row_blk + b, H_AV // A_KV_DIM)),
                  pl.BlockSpec((None, ts, lp), lambda b, j, pt: (b, 0, 0))]
        + _page_specs(npg, kv_tail, layer, lambda j: j)
        + _page_specs(npg, kv_tail, layer, lambda j: j)
        + [pl.BlockSpec(memory_space=pl.ANY)],
        out_specs=pl.BlockSpec((ts, A_DIM), lambda b, j, pt: (row_blk + b, 0)),
        scratch_shapes=[pltpu.VMEM((A_KV_HEADS, rq, HEAD_DIM), BF16),
                        pltpu.VMEM((A_KV_HEADS, rq, 1), F32),
                        pltpu.VMEM((A_KV_HEADS, rq, 1), F32),
                        pltpu.VMEM((A_KV_HEADS, rq, HEAD_DIM), F32)],
    )
    return pl.pallas_call(
        kern,
        grid_spec=grid_spec,
        out_shape=jax.ShapeDtypeStruct(mix.shape, mix.dtype),
        input_output_aliases={5 + 2 * npg: 0},
        compiler_params=_cparams(("arbitrary", "arbitrary")),
        name="dsa_sample_attend",
    )(page_table, h, h, h, bias, *([cache_k] * npg), *([cache_v] * npg), mix)


def _moba_sample_k_kernel(pt_ref, q_ref, kn_ref, *refs, npg, n_steps, ts, rq):
    del pt_ref
    kp = refs[:npg]
    p_ref, pn_ref, l_ref = refs[npg:npg + 3]
    q_s, lg_s, km_s = refs[npg + 3:]
    j = pl.program_id(1)
    ck = npg * PAGE_SIZE
    blk = MOBA_BLOCK
    bps = ck // blk
    nbk = n_steps * bps
    scale = HEAD_DIM ** -0.5

    @pl.when(j == 0)
    def _init():
        q = q_ref[...]
        for c in range(B_KV_HEADS):
            q_s[c] = _stack_heads_rows(q, c, B_GROUP, ts, rq).astype(BF16)
        km_s[...] = jnp.zeros_like(km_s)

    for c in range(B_KV_HEADS):
        kc = _gather_head(kp, c, B_KV_HEADS)
        means = jnp.concatenate(
            [jnp.mean(kc[n * blk:(n + 1) * blk], axis=0, keepdims=True) for n in range(bps)], axis=0)
        km_s[c, pl.ds(pl.multiple_of(j * bps, bps), bps), :] = means
        lg_s[c, :, pl.ds(pl.multiple_of(j * ck, ck), ck)] = (
            lax.dot_general(q_s[c], kc.astype(BF16), _NT, preferred_element_type=F32) * scale)

    @pl.when(j == n_steps - 1)
    def _select_softmax():
        key_i = lax.broadcasted_iota(I32, (rq, LANES), 1)
        tok = lax.broadcasted_iota(I32, (rq, LANES), 0) % ts
        new_ok = jnp.logical_and(key_i <= tok, key_i < ts)
        blk_row = lax.broadcasted_iota(I32, (LANES, ck), 0)
        blk_of_key = lax.broadcasted_iota(I32, (LANES, ck), 1) // blk
        for c in range(B_KV_HEADS):
            cols = slice(c * HEAD_DIM, (c + 1) * HEAD_DIM)
            gate_t = lax.dot_general(km_s[c].astype(BF16), _pad_rows(q_s[c], LANES), _NT, preferred_element_type=F32)
            nbk8 = -(-nbk // SUBLANES) * SUBLANES
            sel_bias = _pad_rows(_rank_select(gate_t[:nbk8], nbk, axis=0), LANES).T[:rq].astype(BF16)
            lg_new = lax.dot_general(q_s[c], _pad_rows(kn_ref[:, cols], LANES).astype(BF16), _NT,
                                     preferred_element_type=F32) * scale
            lg_new = jnp.where(new_ok, lg_new, NEG_INF)
            m = jnp.max(lg_new, axis=1, keepdims=True)
            for t in range(n_steps):
                spread = jnp.where(blk_row == blk_of_key + t * bps, 1.0, 0.0).astype(BF16)
                lanes = slice(t * ck, (t + 1) * ck)
                lg = lg_s[c, :, lanes] + jnp.dot(sel_bias, spread, preferred_element_type=F32)
                lg_s[c, :, lanes] = lg
                m = jnp.maximum(m, jnp.max(lg, axis=1, keepdims=True))
            p_new = jnp.exp(lg_new - m)
            l = jnp.sum(p_new, axis=1, keepdims=True)
            for t in range(n_steps):
                lanes = slice(t * ck, (t + 1) * ck)
                p = jnp.exp(lg_s[c, :, lanes] - m)
                p_ref[c, :, lanes] = p.astype(BF16)
                l = l + jnp.sum(p, axis=1, keepdims=True)
            pn_ref[c] = p_new
            l_ref[c] = jnp.broadcast_to(l, (rq, LANES))


def _moba_sample_v_kernel(pt_ref, vn_ref, p_ref, pn_ref, l_ref, *refs, npg, n_steps, ts):
    del pt_ref
    vp = refs[:npg]
    o_ref = refs[npg + 1]
    acc_s = refs[npg + 2]
    j = pl.program_id(1)

    @pl.when(j == 0)
    def _init():
        acc_s[...] = jnp.zeros_like(acc_s)

    for c in range(B_KV_HEADS):
        vc = _gather_head(vp, c, B_KV_HEADS)
        acc_s[c] += jnp.dot(p_ref[c], vc.astype(BF16), preferred_element_type=F32)

    @pl.when(j == n_steps - 1)
    def _finish():
        for c in range(B_KV_HEADS):
            cols = slice(c * HEAD_DIM, (c + 1) * HEAD_DIM)
            vn = _pad_rows(vn_ref[:, cols], LANES)
            acc = acc_s[c] + jnp.dot(pn_ref[c].astype(BF16), vn.astype(BF16), preferred_element_type=F32)
            _write_head_rows(acc / l_ref[c][:, 0:1], o_ref, c, B_GROUP, ts)


def moba_sample(page_table, h, cache_k, cache_v, mix, layer, n_prompt, db, ts):
    n_pages = page_table.shape[1]
    n_past = n_pages * PAGE_SIZE
    npg = min(KV_PAGES_PER_STEP, n_pages)
    n_steps = n_pages // npg
    ck = npg * PAGE_SIZE
    rq = _sample_rows(B_GROUP, ts)
    row_blk = n_prompt // ts
    page_shape = (PAGE_SIZE, B_KV_HEADS, HEAD_DIM)
    k_spec = pltpu.PrefetchScalarGridSpec(
        num_scalar_prefetch=1,
        grid=(db, n_steps),
        in_specs=[pl.BlockSpec((ts, B_DIM), lambda b, j, pt: (row_blk + b, H_BQ // B_DIM)),
                  pl.BlockSpec((ts, B_KV_DIM), lambda b, j, pt: (row_blk + b, H_BK // B_KV_DIM))]
        + _page_specs(npg, page_shape, layer, lambda j: j),
        out_specs=[pl.BlockSpec((None, B_KV_HEADS, rq, n_past), lambda b, j, pt: (b, 0, 0, 0)),
                   pl.BlockSpec((None, B_KV_HEADS, rq, LANES), lambda b, j, pt: (b, 0, 0, 0)),
                   pl.BlockSpec((None, B_KV_HEADS, rq, LANES), lambda b, j, pt: (b, 0, 0, 0))],
        scratch_shapes=[pltpu.VMEM((B_KV_HEADS, rq, HEAD_DIM), BF16),
                        pltpu.VMEM((B_KV_HEADS, rq, n_past), F32),
                        pltpu.VMEM((B_KV_HEADS, LANES, HEAD_DIM), F32)],
    )
    p, pn, l = pl.pallas_call(
        functools.partial(_moba_sample_k_kernel, npg=npg, n_steps=n_steps, ts=ts, rq=rq),
        grid_spec=k_spec,
        out_shape=[jax.ShapeDtypeStruct((db, B_KV_HEADS, rq, n_past), BF16),
                   jax.ShapeDtypeStruct((db, B_KV_HEADS, rq, LANES), F32),
                   jax.ShapeDtypeStruct((db, B_KV_HEADS, rq, LANES), F32)],
        compiler_params=_cparams(("arbitrary", "arbitrary")),
        name="moba_sample_k",
    )(page_table, h, h, *([cache_k] * npg))
    v_spec = pltpu.PrefetchScalarGridSpec(
        num_scalar_prefetch=1,
        grid=(db, n_steps),
        in_specs=[pl.BlockSpec((ts, B_KV_DIM), lambda b, j, pt: (row_blk + b, H_BV // B_KV_DIM)),
                  pl.BlockSpec((None, B_KV_HEADS, rq, ck), lambda b, j, pt: (b, 0, 0, j)),
                  pl.BlockSpec((None, B_KV_HEADS, rq, LANES), lambda b, j, pt: (b, 0, 0, 0)),
                  pl.BlockSpec((None, B_KV_HEADS, rq, LANES), lambda b, j, pt: (b, 0, 0, 0))]
        + _page_specs(npg, page_shape, layer, lambda j: j)
        + [pl.BlockSpec(memory_space=pl.ANY)],
        out_specs=pl.BlockSpec((ts, B_DIM), lambda b, j, pt: (row_blk + b, A_DIM // B_DIM)),
        scratch_shapes=[pltpu.VMEM((B_KV_HEADS, rq, HEAD_DIM), F32)],
    )
    return pl.pallas_call(
        functools.partial(_moba_sample_v_kernel, npg=npg, n_steps=n_steps, ts=ts),
        grid_spec=v_spec,
        out_shape=jax.ShapeDtypeStruct(mix.shape, mix.dtype),
        input_output_aliases={5 + npg: 0},
        compiler_params=_cparams(("arbitrary", "arbitrary")),
        name="moba_sample_v",
    )(page_table, h, p, pn, l, *([cache_v] * npg), mix)


def _row_tile(n_rows, target):
    best = None
    for t in range(16, target + 1, 16):
        if n_rows % t == 0:
            best = t
    assert best is not None, n_rows
    return best


_DST = dict(aq=H_AQ, bq=H_BQ, iq=H_IQ, pu=H_PU, ak=H_AK, av=H_AV, bk=H_BK, bv=H_BV)
_H_USED = H_IKW + IDX_DIM + IDX_HEADS
PACK_COLS = 256


def _pack_w_in_kernel(wt_ref, o_ref):
    for name, dst in _DST.items():
        src, width = _SRC[name]
        o_ref[:, dst:dst + width] = wt_ref[src:src + width, :].T.astype(BF16)
    src = _SRC["ik"][0]
    lane = lax.broadcasted_iota(I32, (o_ref.shape[0], LANES), 1)
    tile = wt_ref[src:src + LANES, :].T
    o_ref[:, H_IKW:H_IKW + LANES] = jnp.where(lane < _H_USED - H_IKW, tile, 0.0).astype(BF16)
    o_ref[:, H_IKW + LANES:] = jnp.zeros((o_ref.shape[0], H_DIM - H_IKW - LANES), BF16)


def _pack_w_in(w_in):
    depth, d, n_in = w_in.shape
    cols = min(PACK_COLS, d)
    return pl.pallas_call(
        _pack_w_in_kernel,
        grid=(depth, d // cols),
        in_specs=[pl.BlockSpec((None, n_in, cols), lambda l, i: (l, 0, i))],
        out_specs=pl.BlockSpec((None, cols, H_DIM), lambda l, i: (l, i, 0)),
        out_shape=jax.ShapeDtypeStruct((depth, d, H_DIM), BF16),
        compiler_params=_cparams(("parallel", "parallel")),
        name="pack_w_in",
    )(jnp.swapaxes(w_in, 1, 2))


def kernel(x_prompt, x_sample, cache_a_k, cache_a_v, cache_a_idx, cache_b_k, cache_b_v, state_pool,
           cache_mem_k, cache_mem_v, page_table, mem_prompt, w_in, w_out, pool_w, pool_scale, ln1_g, ln1_b,
           w_mem_q, w_mem_k, w_mem_v, w_mem_o, ln2_g, ln2_b, w_ffn_1, w_ffn_3, w_ffn_2, ln3_g, ln3_b):
    batch, seq, d = x_prompt.shape
    db, ts, _ = x_sample.shape
    depth = w_in.shape[0]
    n_past = page_table.shape[1] * PAGE_SIZE
    n_prompt = batch * seq
    n_rows = n_prompt + db * ts
    tm = _row_tile(n_rows, 700)
    tm_p = _row_tile(n_prompt, 1100)

    w_in_p = _pack_w_in(w_in)
    w_out_b = w_out.astype(BF16)
    w_mq_b = w_mem_q.astype(BF16)
    w_mkv_b = jnp.concatenate([w_mem_k, w_mem_v], axis=-1).astype(BF16)
    w_mo_b = w_mem_o.astype(BF16)
    w1_b, w3_b, w2_b = w_ffn_1.astype(BF16), w_ffn_3.astype(BF16), w_ffn_2.astype(BF16)
    pool_w_b = pool_w.astype(BF16)
    pool_scale3 = pool_scale.reshape(depth, 1, POOL_DIM)
    ln = [a.reshape(depth, 1, d) for a in (ln1_g, ln1_b, ln2_g, ln2_b, ln3_g, ln3_b)]
    mem = mem_prompt.reshape(batch * MEM_TOKENS, d)
    idx_t = jnp.swapaxes(cache_a_idx, 2, 3)

    x = jnp.concatenate([x_prompt.reshape(n_prompt, d), x_sample.reshape(db * ts, d)], axis=0)
    outs = {k: [] for k in ("pa_i", "p_pool", "pm_k", "pm_v", "sa_k", "sa_v", "sa_i", "sb_k", "sb_v", "s_pool")}
    kv_new = None
    for l in range(depth):
        h, kv_new = project_in(x, w_in_p, l, tm_p, n_prompt, kv_new)
        h = project_rows(x, w_in_p, l, h, n_prompt // (db * ts), db * ts, H_TN)
        mkv = matmul(mem, w_mkv_b, l, _row_tile(mem.shape[0], 512), MEM_DIM)

        mix = dsa_prompt(h, batch, seq, n_rows)
        mix = moba_prompt(h, mix, batch, seq)
        mix = pool_prompt(h, mix, pool_w_b, pool_scale3, l, batch, seq)
        bias = dsa_sample_select(page_table, h, idx_t, l, n_prompt, db, ts)
        mix = dsa_sample_attend(page_table, h, bias, cache_a_k, cache_a_v, mix, l, n_prompt, db, ts)
        mix = moba_sample(page_table, h, cache_b_k, cache_b_v, mix, l, n_prompt, db, ts)
        mix = pool_sample(h, mix, state_pool, pool_w_b, pool_scale3, l, n_prompt, db, ts, n_past)

        x1 = matmul_residual_ln(mix, w_out_b, x, ln[0], ln[1], l, tm // 2, MIX_DIM)
        q = matmul(x1, w_mq_b, l, tm, MEM_DIM)
        o = memattn_prompt(q, mkv, batch, seq)
        o = memattn_sample(q, cache_mem_k, cache_mem_v, o, l, n_prompt, db, ts)
        x2 = matmul_residual_ln(o, w_mo_b, x1, ln[2], ln[3], l, tm, MEM_DIM)
        x = ffn_ln(x2, w1_b, w3_b, w2_b, ln[4], ln[5], l, tm, 512)

        hp, hs = h[:n_prompt], h[n_prompt:]

        def cols(a, off, width, lead):
            return a[:, off:off + width].reshape(lead)

        kv_s = (db, ts, A_KV_HEADS, HEAD_DIM)
        outs["pa_i"].append(cols(hp, H_IKW, IDX_DIM, (batch, seq, IDX_DIM)))
        outs["p_pool"].append(jnp.stack(
            [h[(b + 1) * seq - POOL_STATE:(b + 1) * seq, H_PU:H_PU + POOL_DIM] for b in range(batch)]))
        outs["pm_k"].append(mkv[:, :MEM_DIM].reshape(batch, MEM_TOKENS, MEM_HEADS, MEM_HEAD_DIM))
        outs["pm_v"].append(mkv[:, MEM_DIM:].reshape(batch, MEM_TOKENS, MEM_HEADS, MEM_HEAD_DIM))
        outs["sa_k"].append(cols(hs, H_AK, A_KV_DIM, kv_s))
        outs["sa_v"].append(cols(hs, H_AV, A_KV_DIM, kv_s))
        outs["sa_i"].append(cols(hs, H_IKW, IDX_DIM, (db, ts, IDX_DIM)))
        outs["sb_k"].append(cols(hs, H_BK, B_KV_DIM, kv_s))
        outs["sb_v"].append(cols(hs, H_BV, B_KV_DIM, kv_s))
        pu_s = cols(hs, H_PU, POOL_DIM, (db, ts, POOL_DIM))
        outs["s_pool"].append(jnp.concatenate([state_pool[l], pu_s], axis=1)[:, -POOL_STATE:])

    st = {k: jnp.stack(v) for k, v in outs.items()}
    for name, arr in zip(("pa_k", "pa_v", "pb_k", "pb_v"), kv_new):
        st[name] = arr.reshape(depth, batch, seq, A_KV_HEADS, HEAD_DIM)
    return (x[:n_prompt].reshape(batch, seq, d), x[n_prompt:].reshape(db, ts, d),
            st["pa_k"], st["pa_v"], st["pa_i"], st["pb_k"], st["pb_v"], st["p_pool"], st["pm_k"], st["pm_v"],
            st["sa_k"], st["sa_v"], st["sa_i"], st["sb_k"], st["sb_v"], st["s_pool"])
```

```python
import functools
import math

import jax
import jax.numpy as jnp
from jax import lax
from jax.experimental import pallas as pl
from jax.experimental.pallas import tpu as pltpu

F32 = jnp.float32
BF16 = jnp.bfloat16
I32 = jnp.int32

D_MODEL = 2048
DEPTH = 4
PAGE_SIZE = 128
HEAD_DIM = 128
A_HEADS = 6
A_KV_HEADS = 2
A_GROUP = A_HEADS // A_KV_HEADS
IDX_HEADS = 8
IDX_DIM = 64
DSA_TOPK = 256
B_HEADS = 6
B_KV_HEADS = 2
B_GROUP = B_HEADS // B_KV_HEADS
MOBA_BLOCK = 256
MOBA_TOPK = 3
POOL_WINDOWS = (2, 4, 8, 16)
POOL_GROUP_DIM = 128
POOL_DIM = len(POOL_WINDOWS) * POOL_GROUP_DIM
POOL_STATE = max(POOL_WINDOWS) - 1
A_DIM = A_HEADS * HEAD_DIM
A_KV_DIM = A_KV_HEADS * HEAD_DIM
B_DIM = B_HEADS * HEAD_DIM
B_KV_DIM = B_KV_HEADS * HEAD_DIM
MIX_DIM = A_DIM + B_DIM + POOL_DIM
MEM_TOKENS = 256
MEM_HEADS = 4
MEM_HEAD_DIM = 128
MEM_DIM = MEM_HEADS * MEM_HEAD_DIM
FFN_HIDDEN = -(-(8 * D_MODEL) // (3 * 256)) * 256
ALPHA = (2.0 * DEPTH) ** 0.25
LN_EPS = 1e-5
NEG_INF = -1e30
INT_MIN = -(2 ** 31)
SOFTMAX_C = (HEAD_DIM ** -0.5) * math.log2(math.e)

LANES = 128
SUBLANES = 8
VMEM_LIMIT_BYTES = 56 * 1024 * 1024

_SRC = dict(aq=(0, 768), ak=(768, 256), av=(1024, 256), iq=(1280, 512), ik=(1792, 64), iw=(1856, 8),
            bq=(1864, 768), bk=(2632, 256), bv=(2888, 256), pu=(3144, 512))
H_AQ, H_BQ, H_IQ, H_PU, H_AK, H_AV, H_BK, H_BV, H_IKW = 0, 768, 1536, 2048, 2560, 2816, 3072, 3328, 3584
H_DIM = 3840

DSA_TQ = 128
DSA_CK = 256
DSA_CA = 512
MOBA_TQ = 128
POOL_TT = 512
MEM_TR = 512
IDX_PAGES_PER_STEP = 64
KV_PAGES_PER_STEP = 32


def _cparams(sem):
    return pltpu.CompilerParams(dimension_semantics=sem, vmem_limit_bytes=VMEM_LIMIT_BYTES)


def _layer_norm_rows(y, g, b):
    mu = jnp.mean(y, axis=-1, keepdims=True)
    yc = y - mu
    var = jnp.mean(yc * yc, axis=-1, keepdims=True)
    return yc * lax.rsqrt(var + LN_EPS) * g + b


def _mm_kernel(x_ref, w_ref, o_ref, xb_ref):
    @pl.when(pl.program_id(1) == 0)
    def _():
        xb_ref[...] = x_ref[...].astype(BF16)

    o_ref[...] = jnp.dot(xb_ref[...], w_ref[...], preferred_element_type=F32)


def matmul(x, w, layer, tm, tn):
    m, k = x.shape
    n = w.shape[2]
    return pl.pallas_call(
        _mm_kernel,
        grid=(m // tm, n // tn),
        in_specs=[pl.BlockSpec((tm, k), lambda i, j: (i, 0)),
                  pl.BlockSpec((None, k, tn), lambda i, j: (layer, 0, j))],
        out_specs=pl.BlockSpec((tm, tn), lambda i, j: (i, j)),
        out_shape=jax.ShapeDtypeStruct((m, n), F32),
        scratch_shapes=[pltpu.VMEM((tm, k), BF16)],
        compiler_params=_cparams(("parallel", "arbitrary")),
        name="matmul",
    )(x, w)


def _mm_rows_kernel(x_ref, w_ref, o_in_ref, o_ref):
    del o_in_ref
    o_ref[...] = jnp.dot(x_ref[...].astype(BF16), w_ref[...], preferred_element_type=F32)


def project_rows(x, w, layer, out, row_block, rows, tn):
    k = x.shape[1]
    n = w.shape[2]
    return pl.pallas_call(
        _mm_rows_kernel,
        grid=(n // tn,),
        in_specs=[pl.BlockSpec((rows, k), lambda j: (row_block, 0)),
                  pl.BlockSpec((None, k, tn), lambda j: (layer, 0, j)),
                  pl.BlockSpec(memory_space=pl.ANY)],
        out_specs=pl.BlockSpec((rows, tn), lambda j: (row_block, j)),
        out_shape=jax.ShapeDtypeStruct(out.shape, out.dtype),
        input_output_aliases={2: 0},
        compiler_params=_cparams(("arbitrary",)),
        name="project_rows",
    )(x, w, out)


H_TN = 1280
_KV_PIECES = (H_AK, H_AV, H_BK, H_BV)


def _store_head_rows(ref, c, rows, n_heads, value):
    ref.reshape(rows * n_heads, HEAD_DIM)[pl.ds(c, rows, stride=n_heads), :] = value


def _project_in_kernel(x_ref, w_ref, *refs, tm):
    o_ref = refs[-6]
    kv_refs = refs[-5:-1]
    xb_ref = refs[-1]
    j = pl.program_id(1)

    @pl.when(j == 0)
    def _():
        xb_ref[...] = x_ref[...].astype(BF16)

    o_ref[...] = jnp.dot(xb_ref[...], w_ref[...], preferred_element_type=F32)
    for piece, kv_ref in zip(_KV_PIECES, kv_refs):
        step, off = divmod(piece, H_TN)

        @pl.when(j == step)
        def _(kv_ref=kv_ref, off=off):
            for c in range(A_KV_HEADS):
                lo = off + c * HEAD_DIM
                _store_head_rows(kv_ref, c, tm, A_KV_HEADS, o_ref[:, lo:lo + HEAD_DIM])


def project_in(x, w, layer, tm, n_prompt, kv_prev):
    m, k = x.shape
    depth, _, n = w.shape
    kv_shape = jax.ShapeDtypeStruct((depth, n_prompt, A_KV_HEADS, HEAD_DIM), F32)
    n_prev = 0 if kv_prev is None else len(kv_prev)
    kv_spec = pl.BlockSpec((None, tm, A_KV_HEADS, HEAD_DIM), lambda i, j: (layer, i, 0, 0))
    outs = pl.pallas_call(
        functools.partial(_project_in_kernel, tm=tm),
        grid=(n_prompt // tm, n // H_TN),
        in_specs=[pl.BlockSpec((tm, k), lambda i, j: (i, 0)),
                  pl.BlockSpec((None, k, H_TN), lambda i, j: (layer, 0, j))]
        + [pl.BlockSpec(memory_space=pl.ANY)] * n_prev,
        out_specs=[pl.BlockSpec((tm, H_TN), lambda i, j: (i, j))] + [kv_spec] * len(_KV_PIECES),
        out_shape=[jax.ShapeDtypeStruct((m, n), F32)] + [kv_shape] * len(_KV_PIECES),
        input_output_aliases={2 + i: 1 + i for i in range(n_prev)},
        scratch_shapes=[pltpu.VMEM((tm, k), BF16)],
        compiler_params=_cparams(("parallel", "arbitrary")),
        name="project_in",
    )(x, w, *(kv_prev or ()))
    return outs[0], tuple(outs[1:])


def _mm_res_ln_kernel(a_ref, w_ref, x_ref, g_ref, b_ref, o_ref, acc_ref, *, nk):
    kk = pl.program_id(1)

    @pl.when(kk == 0)
    def _():
        acc_ref[...] = jnp.zeros_like(acc_ref)

    acc_ref[...] += jnp.dot(a_ref[...].astype(BF16), w_ref[...], preferred_element_type=F32)

    @pl.when(kk == nk - 1)
    def _():
        y = ALPHA * x_ref[...] + acc_ref[...]
        o_ref[...] = _layer_norm_rows(y, g_ref[...], b_ref[...])


def matmul_residual_ln(a, w, x, g, b, layer, tm, tk):
    m, k = a.shape
    n = w.shape[2]
    nk = k // tk
    return pl.pallas_call(
        functools.partial(_mm_res_ln_kernel, nk=nk),
        grid=(m // tm, nk),
        in_specs=[pl.BlockSpec((tm, tk), lambda i, j: (i, j)),
                  pl.BlockSpec((None, tk, n), lambda i, j: (layer, j, 0)),
                  pl.BlockSpec((tm, n), lambda i, j: (i, 0)),
                  pl.BlockSpec((None, 1, n), lambda i, j: (layer, 0, 0)),
                  pl.BlockSpec((None, 1, n), lambda i, j: (layer, 0, 0))],
        out_specs=pl.BlockSpec((tm, n), lambda i, j: (i, 0)),
        out_shape=jax.ShapeDtypeStruct((m, n), F32),
        scratch_shapes=[pltpu.VMEM((tm, n), F32)],
        compiler_params=_cparams(("parallel", "arbitrary")),
        name="matmul_residual_ln",
    )(a, w, x, g, b)


def _ffn_kernel(x_ref, w1_ref, w3_ref, w2_ref, g_ref, b_ref, o_ref, xb_ref, acc_ref, *, nf):
    j = pl.program_id(1)

    @pl.when(j == 0)
    def _():
        xb_ref[...] = x_ref[...].astype(BF16)
        acc_ref[...] = jnp.zeros_like(acc_ref)

    xb = xb_ref[...]
    h1 = jnp.dot(xb, w1_ref[...], preferred_element_type=F32)
    h3 = jnp.dot(xb, w3_ref[...], preferred_element_type=F32)
    h = (h1 / (1.0 + jnp.exp(-h1))) * h3
    acc_ref[...] += jnp.dot(h.astype(BF16), w2_ref[...].astype(BF16), preferred_element_type=F32)

    @pl.when(j == nf - 1)
    def _():
        y = ALPHA * x_ref[...] + acc_ref[...]
        o_ref[...] = _layer_norm_rows(y, g_ref[...], b_ref[...])


def ffn_ln(x, w1, w3, w2, g, b, layer, tm, tf):
    m, d = x.shape
    f = w1.shape[2]
    nf = f // tf
    return pl.pallas_call(
        functools.partial(_ffn_kernel, nf=nf),
        grid=(m // tm, nf),
        in_specs=[pl.BlockSpec((tm, d), lambda i, j: (i, 0)),
                  pl.BlockSpec((None, d, tf), lambda i, j: (layer, 0, j)),
                  pl.BlockSpec((None, d, tf), lambda i, j: (layer, 0, j)),
                  pl.BlockSpec((None, tf, d), lambda i, j: (layer, j, 0)),
                  pl.BlockSpec((None, 1, d), lambda i, j: (layer, 0, 0)),
                  pl.BlockSpec((None, 1, d), lambda i, j: (layer, 0, 0))],
        out_specs=pl.BlockSpec((tm, d), lambda i, j: (i, 0)),
        out_shape=jax.ShapeDtypeStruct((m, d), F32),
        scratch_shapes=[pltpu.VMEM((tm, d), BF16), pltpu.VMEM((tm, d), F32)],
        compiler_params=_cparams(("parallel", "arbitrary")),
        name="ffn_ln",
    )(x, w1, w3, w2, g, b)


def _sortable_key(score):
    score = jnp.where(score == 0.0, 0.0, score)
    bits = pltpu.bitcast(score, I32)
    return jnp.where(bits < 0, bits ^ 0x7FFFFFFF, bits)


def _kth_largest(count_ge, shape, k_sel):
    c0 = count_ge(jnp.zeros(shape, I32))
    ok0 = c0 >= k_sel
    base = jnp.where(ok0, 0, INT_MIN).astype(I32)
    cnt = jnp.where(ok0, c0, jnp.iinfo(jnp.int32).max).astype(I32)

    def bit_body(i, carry):
        base, cnt = carry
        cand = base | jnp.left_shift(jnp.int32(1), 30 - i)
        c = count_ge(cand)
        ok = c >= k_sel
        return jnp.where(ok, cand, base), jnp.where(ok, c, cnt)

    return lax.fori_loop(0, 31, bit_body, (base, cnt))


def _flash_step_t(kc, vt, qt, bias, m, l, acc):
    lg = jnp.dot(kc, qt, preferred_element_type=F32) + bias
    m_new = jnp.maximum(m, jnp.max(lg, axis=0, keepdims=True))
    alpha = jnp.exp2(m - m_new)
    p = jnp.exp2(lg - m_new)
    l_new = alpha * l + jnp.sum(p, axis=0, keepdims=True)
    acc_new = alpha * acc + jnp.dot(vt, p.astype(BF16), preferred_element_type=F32)
    return m_new, l_new, acc_new


def _prep_kv_t(k_ref, v_ref, k_s, vt_s, rows, n_heads):
    kk = k_ref[rows, :]
    vv = v_ref[rows, :]
    for c in range(n_heads):
        cols = slice(c * HEAD_DIM, (c + 1) * HEAD_DIM)
        k_s[c, rows, :] = kk[:, cols].astype(BF16)
        vt_s[c, :, rows] = vv[:, cols].T.astype(BF16)
    return kk


def _stage_queries_t(q, qt_s, n_kv, group):
    q = q * SOFTMAX_C
    for c in range(n_kv):
        qt_s[c] = jnp.concatenate(
            [q[:, (c * group + g) * HEAD_DIM:(c * group + g + 1) * HEAD_DIM].T for g in range(group)],
            axis=1).astype(BF16)


def _head_q(qt_s, hh, tq):
    c, g = divmod(hh, A_GROUP)
    return qt_s[c, :, g * tq:(g + 1) * tq]


def _write_heads(acc_s, ls, o_ref, n_heads):
    for hh in range(n_heads):
        o_ref[:, hh * HEAD_DIM:(hh + 1) * HEAD_DIM] = (acc_s[hh] / ls[hh]).T


def _dsa_prompt_kernel(aq_ref, iq_ref, ikw_ref, ak_ref, av_ref, o_ref,
                       ik_s, k_s, vt_s, iqt_s, iwt_s, qt_s, keys_s, bias_s, acc_s, *, seq, k_sel):
    tq, ck = DSA_TQ, DSA_CK
    qi = pl.program_id(1)
    q0 = qi * tq
    prep_rows = min(512, seq)

    @pl.when(qi == 0)
    def _prep():
        lane = lax.broadcasted_iota(I32, (prep_rows, LANES), 1)

        def body(r, _):
            rows = pl.ds(pl.multiple_of(r * prep_rows, prep_rows), prep_rows)
            ik_s[rows, :] = jnp.where(lane < IDX_DIM, ikw_ref[rows, :], 0.0).astype(BF16)
            _prep_kv_t(ak_ref, av_ref, k_s, vt_s, rows, A_KV_HEADS)
            return 0

        lax.fori_loop(0, seq // prep_rows, body, 0)

    n_ck = (q0 + tq + ck - 1) // ck
    ikw_q = ikw_ref[pl.ds(pl.multiple_of(q0, tq), tq), :]
    iwt_s[...] = ikw_q.T[IDX_DIM:IDX_DIM + IDX_HEADS] * (IDX_HEADS ** -0.5)
    iqt = (iq_ref[...] * (IDX_DIM ** -0.5)).T
    zpad = jnp.zeros((LANES - IDX_DIM, 2 * tq), F32)
    for hp in range(IDX_HEADS // 2):
        pair = jnp.concatenate([iqt[(2 * hp) * IDX_DIM:(2 * hp + 1) * IDX_DIM],
                                iqt[(2 * hp + 1) * IDX_DIM:(2 * hp + 2) * IDX_DIM]], axis=1)
        iqt_s[hp] = jnp.concatenate([pair, zpad], axis=0).astype(BF16)
    _stage_queries_t(aq_ref[...], qt_s, A_KV_HEADS, A_GROUP)

    qpos = q0 + lax.broadcasted_iota(I32, (ck, tq), 1)
    krow = lax.broadcasted_iota(I32, (ck, tq), 0)

    def score_chunk(off):
        ikc = ik_s[pl.ds(off, ck), :]
        acc = jnp.zeros((ck, tq), F32)
        for hp in range(IDX_HEADS // 2):
            s = jnp.dot(ikc, iqt_s[hp], preferred_element_type=F32)
            acc = acc + (jnp.maximum(s[:, :tq], 0.0) * iwt_s[2 * hp:2 * hp + 1, :]
                         + jnp.maximum(s[:, tq:], 0.0) * iwt_s[2 * hp + 1:2 * hp + 2, :])
        keys_s[pl.ds(off, ck), :] = jnp.where(off + krow <= qpos, _sortable_key(acc), INT_MIN)

    def score_body(c, _):
        off = pl.multiple_of(c * (2 * ck), 2 * ck)
        score_chunk(off)
        score_chunk(off + ck)
        return 0

    lax.fori_loop(0, (n_ck + 1) // 2, score_body, 0)

    part = ck // 4

    def count_ge(cand):
        def body(c, acc):
            ind = jnp.where(keys_s[pl.ds(pl.multiple_of(c * ck, ck), ck), :] >= cand, 1, 0).astype(I32)
            return acc + ((ind[0:part] + ind[part:2 * part]) + (ind[2 * part:3 * part] + ind[3 * part:]))

        acc = lax.fori_loop(0, n_ck, body, jnp.zeros((part, tq), I32))
        return jnp.sum(acc, axis=0, keepdims=True)

    thr, cnt_thr = _kth_largest(count_ge, (1, tq), k_sel)
    has_tie = jnp.logical_and(cnt_thr > k_sel, thr > INT_MIN)
    any_tie = jnp.max(has_tie.astype(I32)) > 0
    thr_c = jnp.maximum(thr, INT_MIN + 1)

    @pl.when(jnp.logical_not(any_tie))
    def _fast():
        def body(c, _):
            rows = pl.ds(pl.multiple_of(c * ck, ck), ck)
            bias_s[rows, :] = jnp.where(keys_s[rows, :] >= thr_c, 0.0, NEG_INF)
            return 0

        lax.fori_loop(0, n_ck, body, 0)

    @pl.when(any_tie)
    def _ties():
        need = (k_sel - count_ge(thr_c + 1)).astype(F32)
        tri = jnp.where(lax.broadcasted_iota(I32, (LANES, LANES), 0)
                        >= lax.broadcasted_iota(I32, (LANES, LANES), 1), 1.0, 0.0).astype(BF16)

        def body(c, run):
            rows = pl.ds(pl.multiple_of(c * LANES, LANES), LANES)
            kchunk = keys_s[rows, :]
            eq = kchunk == thr_c
            eqf = jnp.where(eq, 1.0, 0.0)
            pre = jnp.dot(tri, eqf.astype(BF16), preferred_element_type=F32) + run
            sel = jnp.logical_or(kchunk > thr_c, jnp.logical_and(eq, pre <= need))
            bias_s[rows, :] = jnp.where(sel, 0.0, NEG_INF)
            return run + jnp.sum(eqf, axis=0, keepdims=True)

        lax.fori_loop(0, n_ck * (ck // LANES), body, jnp.zeros((1, tq), F32))

    ca = DSA_CA
    n_ca = (q0 + tq + ca - 1) // ca
    for extra in range(1, ca // ck):
        @pl.when(n_ck + extra <= n_ca * (ca // ck))
        def _mask_tail(extra=extra):
            bias_s[pl.ds(pl.multiple_of((n_ck + extra - 1) * ck, ck), ck), :] = jnp.full((ck, tq), NEG_INF, F32)

    acc_s[...] = jnp.zeros_like(acc_s)

    def att_body(j, carry):
        ms, ls = carry
        rows = pl.ds(pl.multiple_of(j * ca, ca), ca)
        bias = bias_s[rows, :]
        new_m, new_l = [], []
        for hh in range(A_HEADS):
            c = hh // A_GROUP
            m, l, acc = _flash_step_t(k_s[c, rows, :], vt_s[c, :, rows], _head_q(qt_s, hh, tq), bias,
                                      ms[hh], ls[hh], acc_s[hh])
            acc_s[hh] = acc
            new_m.append(m)
            new_l.append(l)
        return tuple(new_m), tuple(new_l)

    init = (tuple(jnp.full((1, tq), NEG_INF, F32) for _ in range(A_HEADS)),
            tuple(jnp.zeros((1, tq), F32) for _ in range(A_HEADS)))
    _, ls = lax.fori_loop(0, n_ca, att_body, init)
    _write_heads(acc_s, ls, o_ref, A_HEADS)


def dsa_prompt(h, batch, seq, n_rows):
    tq = DSA_TQ
    nq = seq // tq
    k_sel = min(DSA_TOPK, seq // 4)
    kern = functools.partial(_dsa_prompt_kernel, seq=seq, k_sel=k_sel)
    return pl.pallas_call(
        kern,
        grid=(batch, nq),
        in_specs=[pl.BlockSpec((tq, A_DIM), lambda b, i: (b * nq + i, H_AQ // A_DIM)),
                  pl.BlockSpec((tq, IDX_HEADS * IDX_DIM), lambda b, i: (b * nq + i, H_IQ // (IDX_HEADS * IDX_DIM))),
                  pl.BlockSpec((seq, LANES), lambda b, i: (b, H_IKW // LANES)),
                  pl.BlockSpec((seq, A_KV_DIM), lambda b, i: (b, H_AK // A_KV_DIM)),
                  pl.BlockSpec((seq, A_KV_DIM), lambda b, i: (b, H_AV // A_KV_DIM))],
        out_specs=pl.BlockSpec((tq, A_DIM), lambda b, i: (b * nq + i, 0)),
        out_shape=jax.ShapeDtypeStruct((n_rows, MIX_DIM), F32),
        scratch_shapes=[pltpu.VMEM((seq, LANES), BF16),
                        pltpu.VMEM((A_KV_HEADS, seq, HEAD_DIM), BF16),
                        pltpu.VMEM((A_KV_HEADS, HEAD_DIM, seq), BF16),
                        pltpu.VMEM((IDX_HEADS // 2, LANES, 2 * tq), BF16),
                        pltpu.VMEM((IDX_HEADS, tq), F32),
                        pltpu.VMEM((A_KV_HEADS, HEAD_DIM, A_GROUP * tq), BF16),
                        pltpu.VMEM((seq, tq), I32),
                        pltpu.VMEM((seq, tq), F32),
                        pltpu.VMEM((A_HEADS, HEAD_DIM, tq), F32)],
        compiler_params=_cparams(("arbitrary", "arbitrary")),
        name="dsa_prompt",
    )(h, h, h, h, h)


def _rank_select(gate, n_past_blocks, axis):
    idx = lax.broadcasted_iota(I32, gate.shape, axis)

    def body(m, rank):
        gm = jnp.sum(jnp.where(idx == m, gate, 0.0), axis=axis, keepdims=True)
        beats = jnp.logical_or(gm > gate, jnp.logical_and(gm == gate, m < idx))
        return rank + jnp.where(beats, 1, 0).astype(I32)

    rank = lax.fori_loop(0, n_past_blocks, body, jnp.zeros(gate.shape, I32))
    sel = jnp.logical_and(idx < n_past_blocks, rank < MOBA_TOPK)
    return jnp.where(sel, 0.0, NEG_INF)


def _moba_prompt_kernel(bq_ref, bk_ref, bv_ref, mix_ref, o_ref, k_s, vt_s, km_s, kmb_s, qt_s, sb_s, acc_s,
                        *, seq, nbp):
    del mix_ref
    tq, blk = MOBA_TQ, MOBA_BLOCK
    qi = pl.program_id(1)
    q0 = qi * tq
    nb = seq // blk

    @pl.when(qi == 0)
    def _prep():
        km_s[...] = jnp.zeros_like(km_s)

        def body(n, _):
            rows = pl.ds(pl.multiple_of(n * blk, blk), blk)
            kk = _prep_kv_t(bk_ref, bv_ref, k_s, vt_s, rows, B_KV_HEADS)
            km_s[pl.ds(n, 1), :] = jnp.mean(kk, axis=0, keepdims=True)
            return 0

        lax.fori_loop(0, nb, body, 0)
        for c in range(B_KV_HEADS):
            kmb_s[c] = km_s[:, c * HEAD_DIM:(c + 1) * HEAD_DIM].astype(BF16)

    own = q0 // blk
    own_rows = pl.ds(pl.multiple_of(own * blk, blk), blk)
    bq = bq_ref[...]
    _stage_queries_t(bq, qt_s, B_KV_HEADS, B_GROUP)
    gates = []
    for c in range(B_KV_HEADS):
        qg = jnp.concatenate(
            [bq[:, (c * B_GROUP + g) * HEAD_DIM:(c * B_GROUP + g + 1) * HEAD_DIM].T for g in range(B_GROUP)],
            axis=1).astype(BF16)
        gates.append(jnp.dot(kmb_s[c], qg, preferred_element_type=F32)[:nbp])
    sel_bias = _rank_select(jnp.concatenate(gates, axis=1), own, axis=0)
    for hh in range(B_HEADS):
        sb_s[hh] = sel_bias[:, hh * tq:(hh + 1) * tq]

    qpos = q0 + lax.broadcasted_iota(I32, (blk, tq), 1)
    kpos = own * blk + lax.broadcasted_iota(I32, (blk, tq), 0)
    causal_bias = jnp.where(kpos <= qpos, 0.0, NEG_INF)
    ms, ls = [], []
    for hh in range(B_HEADS):
        c = hh // B_GROUP
        m, l, acc = _flash_step_t(k_s[c, own_rows, :], vt_s[c, :, own_rows], _head_q(qt_s, hh, tq), causal_bias,
                                  jnp.full((1, tq), NEG_INF, F32), jnp.zeros((1, tq), F32),
                                  jnp.zeros((HEAD_DIM, tq), F32))
        acc_s[hh] = acc
        ms.append(m)
        ls.append(l)

    odd = own % 2

    def pair_body(p, carry):
        ms, ls = carry
        b0 = jnp.maximum(2 * p - odd, 0)
        rows = pl.ds(pl.multiple_of(b0 * blk, blk), 2 * blk)
        skip_second = jnp.where(jnp.logical_and(p == 0, odd == 1), NEG_INF, 0.0)
        new_m, new_l = [], []
        for hh in range(B_HEADS):
            c = hh // B_GROUP
            bias = jnp.concatenate(
                [jnp.broadcast_to(sb_s[hh, pl.ds(b0, 1), :], (blk, tq)),
                 jnp.broadcast_to(sb_s[hh, pl.ds(b0 + 1, 1), :] + skip_second, (blk, tq))], axis=0)
            m, l, acc = _flash_step_t(k_s[c, rows, :], vt_s[c, :, rows], _head_q(qt_s, hh, tq), bias,
                                      ms[hh], ls[hh], acc_s[hh])
            acc_s[hh] = acc
            new_m.append(m)
            new_l.append(l)
        return tuple(new_m), tuple(new_l)

    _, ls = lax.fori_loop(0, (own + 1) // 2, pair_body, (tuple(ms), tuple(ls)))
    _write_heads(acc_s, ls, o_ref, B_HEADS)


def moba_prompt(h, mix, batch, seq):
    tq = MOBA_TQ
    nq = seq // tq
    nbp = -(-(seq // MOBA_BLOCK) // SUBLANES) * SUBLANES
    return pl.pallas_call(
        functools.partial(_moba_prompt_kernel, seq=seq, nbp=nbp),
        grid=(batch, nq),
        in_specs=[pl.BlockSpec((tq, B_DIM), lambda b, i: (b * nq + i, H_BQ // B_DIM)),
                  pl.BlockSpec((seq, B_KV_DIM), lambda b, i: (b, H_BK // B_KV_DIM)),
                  pl.BlockSpec((seq, B_KV_DIM), lambda b, i: (b, H_BV // B_KV_DIM)),
                  pl.BlockSpec(memory_space=pl.ANY)],
        out_specs=pl.BlockSpec((tq, B_DIM), lambda b, i: (b * nq + i, A_DIM // B_DIM)),
        out_shape=jax.ShapeDtypeStruct(mix.shape, mix.dtype),
        input_output_aliases={3: 0},
        scratch_shapes=[pltpu.VMEM((B_KV_HEADS, seq, HEAD_DIM), BF16),
                        pltpu.VMEM((B_KV_HEADS, HEAD_DIM, seq), BF16),
                        pltpu.VMEM((LANES, B_KV_DIM), F32),
                        pltpu.VMEM((B_KV_HEADS, LANES, HEAD_DIM), BF16),
                        pltpu.VMEM((B_KV_HEADS, HEAD_DIM, B_GROUP * tq), BF16),
                        pltpu.VMEM((B_HEADS, nbp, tq), F32),
                        pltpu.VMEM((B_HEADS, HEAD_DIM, tq), F32)],
        compiler_params=_cparams(("arbitrary", "arbitrary")),
        name="moba_prompt",
    )(h, h, h, mix)


POOL_HALO = 16


def _pool_groups(u, window_sum, cnt_of, w_ref, sc_ref, o_ref):
    for g, w in enumerate(POOL_WINDOWS):
        cols = slice(g * POOL_GROUP_DIM, (g + 1) * POOL_GROUP_DIM)
        ug = u[:, cols]
        d = window_sum(g, w, ug) / cnt_of(w) - ug
        y = jnp.dot(d.astype(BF16), w_ref[g], preferred_element_type=F32)
        o_ref[:, cols] = y * sc_ref[:, cols]


def _pool_prompt_kernel(u_ref, halo_ref, w_ref, sc_ref, mix_ref, o_ref, ext_s):
    del mix_ref
    tt = POOL_TT
    i = pl.program_id(1)
    u = u_ref[...]
    ext_s[0:POOL_HALO, :] = jnp.where(i == 0, 0.0, halo_ref[...])
    ext_s[POOL_HALO:POOL_HALO + tt, :] = u
    t = i * tt + lax.broadcasted_iota(I32, (tt, 1), 0)

    def window_sum(g, w, ug):
        s = ug
        for k in range(1, w):
            s = s + ext_s[pl.ds(POOL_HALO - k, tt), g * POOL_GROUP_DIM:(g + 1) * POOL_GROUP_DIM]
        return s

    _pool_groups(u, window_sum, lambda w: jnp.minimum(w, t + 1).astype(F32), w_ref, sc_ref, o_ref)


def pool_prompt(h, mix, pool_w, pool_scale, layer, batch, seq):
    tt = POOL_TT
    nt = seq // tt
    halo_blocks = tt // POOL_HALO
    return pl.pallas_call(
        _pool_prompt_kernel,
        grid=(batch, nt),
        in_specs=[pl.BlockSpec((tt, POOL_DIM), lambda b, i: (b * nt + i, H_PU // POOL_DIM)),
                  pl.BlockSpec((POOL_HALO, POOL_DIM),
                               lambda b, i: (jnp.maximum((b * nt + i) * halo_blocks - 1, 0), H_PU // POOL_DIM)),
                  pl.BlockSpec((None, len(POOL_WINDOWS), POOL_GROUP_DIM, POOL_GROUP_DIM),
                               lambda b, i: (layer, 0, 0, 0)),
                  pl.BlockSpec((None, 1, POOL_DIM), lambda b, i: (layer, 0, 0)),
                  pl.BlockSpec(memory_space=pl.ANY)],
        out_specs=pl.BlockSpec((tt, POOL_DIM), lambda b, i: (b * nt + i, (A_DIM + B_DIM) // POOL_DIM)),
        out_shape=jax.ShapeDtypeStruct(mix.shape, mix.dtype),
        input_output_aliases={4: 0},
        scratch_shapes=[pltpu.VMEM((POOL_HALO + tt, POOL_DIM), F32)],
        compiler_params=_cparams(("arbitrary", "arbitrary")),
        name="pool_prompt",
    )(h, h, pool_w, pool_scale, mix)


def _pool_sample_kernel(u_ref, st_ref, w_ref, sc_ref, mix_ref, o_ref, ext_s, *, db, ts, n_past):
    del mix_ref
    u = u_ref[...]
    for b in range(db):
        ext_s[b, POOL_HALO - POOL_STATE:POOL_HALO, :] = st_ref[b]
        ext_s[b, POOL_HALO:POOL_HALO + ts, :] = u[b * ts:(b + 1) * ts]
    t = n_past + lax.broadcasted_iota(I32, (db * ts, 1), 0) % ts

    def window_sum(g, w, ug):
        parts = []
        for b in range(db):
            s = ug[b * ts:(b + 1) * ts]
            for k in range(1, w):
                s = s + ext_s[b, pl.ds(POOL_HALO - k, ts), g * POOL_GROUP_DIM:(g + 1) * POOL_GROUP_DIM]
            parts.append(s)
        return jnp.concatenate(parts, axis=0)

    _pool_groups(u, window_sum, lambda w: jnp.minimum(w, t + 1).astype(F32), w_ref, sc_ref, o_ref)


def pool_sample(h, mix, state_pool, pool_w, pool_scale, layer, n_prompt, db, ts, n_past):
    rows = db * ts
    return pl.pallas_call(
        functools.partial(_pool_sample_kernel, db=db, ts=ts, n_past=n_past),
        grid=(1,),
        in_specs=[pl.BlockSpec((rows, POOL_DIM), lambda i: (n_prompt // rows, H_PU // POOL_DIM)),
                  pl.BlockSpec((None, db, POOL_STATE, POOL_DIM), lambda i: (layer, 0, 0, 0)),
                  pl.BlockSpec((None, len(POOL_WINDOWS), POOL_GROUP_DIM, POOL_GROUP_DIM),
                               lambda i: (layer, 0, 0, 0)),
                  pl.BlockSpec((None, 1, POOL_DIM), lambda i: (layer, 0, 0)),
                  pl.BlockSpec(memory_space=pl.ANY)],
        out_specs=pl.BlockSpec((rows, POOL_DIM), lambda i: (n_prompt // rows, (A_DIM + B_DIM) // POOL_DIM)),
        out_shape=jax.ShapeDtypeStruct(mix.shape, mix.dtype),
        input_output_aliases={4: 0},
        scratch_shapes=[pltpu.VMEM((db, POOL_HALO + ts, POOL_DIM), F32)],
        compiler_params=_cparams(("arbitrary",)),
        name="pool_sample",
    )(h, state_pool, pool_w, pool_scale, mix)


def _memattn_heads(q_ref, head_k, head_v, o_ref):
    scale = MEM_HEAD_DIM ** -0.5
    for hh in range(MEM_HEADS):
        cols = slice(hh * MEM_HEAD_DIM, (hh + 1) * MEM_HEAD_DIM)
        qh = q_ref[:, cols].astype(BF16)
        lg = lax.dot_general(qh, head_k(hh).astype(BF16), (((1,), (1,)), ((), ())),
                             preferred_element_type=F32) * scale
        m = jnp.max(lg, axis=1, keepdims=True)
        p = jnp.exp(lg - m)
        l = jnp.sum(p, axis=1, keepdims=True)
        o_ref[:, cols] = jnp.dot(p.astype(BF16), head_v(hh).astype(BF16), preferred_element_type=F32) / l


def _memattn_prompt_kernel(q_ref, mk_ref, mv_ref, o_ref):
    _memattn_heads(q_ref, lambda hh: mk_ref[:, hh * MEM_HEAD_DIM:(hh + 1) * MEM_HEAD_DIM],
                   lambda hh: mv_ref[:, hh * MEM_HEAD_DIM:(hh + 1) * MEM_HEAD_DIM], o_ref)


def _memattn_sample_kernel(q_ref, mk_ref, mv_ref, o_in_ref, o_ref):
    del o_in_ref
    _memattn_heads(q_ref, lambda hh: _head_rows(mk_ref, hh, MEM_TOKENS, MEM_HEADS),
                   lambda hh: _head_rows(mv_ref, hh, MEM_TOKENS, MEM_HEADS), o_ref)


def memattn_prompt(q, mkv, batch, seq):
    tr = min(MEM_TR, seq)
    nt = seq // tr
    return pl.pallas_call(
        _memattn_prompt_kernel,
        grid=(batch, nt),
        in_specs=[pl.BlockSpec((tr, MEM_DIM), lambda b, i: (b * nt + i, 0)),
                  pl.BlockSpec((MEM_TOKENS, MEM_DIM), lambda b, i: (b, 0)),
                  pl.BlockSpec((MEM_TOKENS, MEM_DIM), lambda b, i: (b, 1))],
        out_specs=pl.BlockSpec((tr, MEM_DIM), lambda b, i: (b * nt + i, 0)),
        out_shape=jax.ShapeDtypeStruct(q.shape, F32),
        compiler_params=_cparams(("arbitrary", "arbitrary")),
        name="memattn_prompt",
    )(q, mkv, mkv)


def memattn_sample(q, cache_k, cache_v, o, layer, n_prompt, db, ts):
    cache_spec = pl.BlockSpec((None, None, MEM_TOKENS, MEM_HEADS, MEM_HEAD_DIM), lambda b: (layer, b, 0, 0, 0))
    return pl.pallas_call(
        _memattn_sample_kernel,
        grid=(db,),
        in_specs=[pl.BlockSpec((ts, MEM_DIM), lambda b: (n_prompt // ts + b, 0)),
                  cache_spec, cache_spec,
                  pl.BlockSpec(memory_space=pl.ANY)],
        out_specs=pl.BlockSpec((ts, MEM_DIM), lambda b: (n_prompt // ts + b, 0)),
        out_shape=jax.ShapeDtypeStruct(o.shape, o.dtype),
        input_output_aliases={3: 0},
        compiler_params=_cparams(("arbitrary",)),
        name="memattn_sample",
    )(q, cache_k, cache_v, o)


def _page_specs(npg, page_shape, layer, step_of):
    zeros = (0,) * len(page_shape)

    def spec(i):
        return pl.BlockSpec((None, None) + page_shape,
                            lambda b, j, pt: (layer, pt[b, step_of(j) * npg + i]) + zeros)
    return [spec(i) for i in range(npg)]


def _pad_rows(x, rows):
    return jnp.concatenate([x, jnp.zeros((rows - x.shape[0], x.shape[1]), x.dtype)], axis=0)


def _head_rows(ref, c, rows, n_heads):
    return ref.reshape(rows * n_heads, HEAD_DIM)[pl.ds(c, rows, stride=n_heads), :]


def _gather_head(pages, c, n_heads):
    return jnp.concatenate([_head_rows(p, c, PAGE_SIZE, n_heads) for p in pages], axis=0)


def _dsa_sample_select_kernel(pt_ref, iq_ref, ikw_ref, *refs, npg, n_steps, n_past, ts, k_sel):
    del pt_ref
    pages = refs[:npg]
    bias_ref, keys_s = refs[npg], refs[npg + 1]
    j = pl.program_id(1)
    ck = npg * PAGE_SIZE
    lp = n_past + LANES
    iq = iq_ref[...] * (IDX_DIM ** -0.5)
    iq_st = jnp.concatenate([iq[:, h * IDX_DIM:(h + 1) * IDX_DIM] for h in range(IDX_HEADS)],
                            axis=0).astype(BF16)
    ikw = ikw_ref[...]
    iw = ikw[:, IDX_DIM:IDX_DIM + IDX_HEADS] * (IDX_HEADS ** -0.5)

    def scores(ik_t):
        s = jnp.dot(iq_st, ik_t.astype(BF16), preferred_element_type=F32)
        acc = jnp.zeros((ts, ik_t.shape[1]), F32)
        for h in range(IDX_HEADS):
            acc = acc + jnp.maximum(s[h * ts:(h + 1) * ts], 0.0) * iw[:, h:h + 1]
        return acc

    ik_past = jnp.concatenate([pages[i][...] for i in range(npg)], axis=1)
    keys_s[:, pl.ds(pl.multiple_of(j * ck, ck), ck)] = _sortable_key(scores(ik_past))

    @pl.when(j == n_steps - 1)
    def _select():
        sc_new = scores(_pad_rows(ikw, LANES).T[:IDX_DIM])
        causal = lax.broadcasted_iota(I32, (ts, LANES), 1) <= lax.broadcasted_iota(I32, (ts, LANES), 0)
        keys_s[:, n_past:lp] = jnp.where(causal, _sortable_key(sc_new), INT_MIN)

        def count_ge(cand):
            return jnp.sum(jnp.where(keys_s[...] >= cand, 1, 0).astype(I32), axis=1, keepdims=True)

        thr, cnt_thr = _kth_largest(count_ge, (ts, 1), k_sel)
        has_tie = jnp.logical_and(cnt_thr > k_sel, thr > INT_MIN)
        any_tie = jnp.max(has_tie.astype(I32)) > 0
        thr_c = jnp.maximum(thr, INT_MIN + 1)

        @pl.when(jnp.logical_not(any_tie))
        def _fast():
            bias_ref[...] = jnp.where(keys_s[...] >= thr_c, 0.0, NEG_INF)

        @pl.when(any_tie)
        def _ties():
            need = (k_sel - count_ge(thr_c + 1)).astype(F32)
            tri = jnp.where(lax.broadcasted_iota(I32, (LANES, LANES), 0)
                            <= lax.broadcasted_iota(I32, (LANES, LANES), 1), 1.0, 0.0).astype(BF16)

            def body(c, run):
                cols = pl.ds(pl.multiple_of(c * LANES, LANES), LANES)
                kchunk = keys_s[:, cols]
                eq = kchunk == thr_c
                eqf = jnp.where(eq, 1.0, 0.0)
                pre = jnp.dot(eqf.astype(BF16), tri, preferred_element_type=F32) + run
                sel = jnp.logical_or(kchunk > thr_c, jnp.logical_and(eq, pre <= need))
                bias_ref[:, cols] = jnp.where(sel, 0.0, NEG_INF)
                return run + jnp.sum(eqf, axis=1, keepdims=True)

            lax.fori_loop(0, lp // LANES, body, jnp.zeros((ts, 1), F32))


def dsa_sample_select(page_table, h, cache_idx, layer, n_prompt, db, ts):
    n_pages = page_table.shape[1]
    n_past = n_pages * PAGE_SIZE
    npg = min(IDX_PAGES_PER_STEP, n_pages)
    n_steps = n_pages // npg
    k_sel = min(DSA_TOPK, (n_past + ts) // 4)
    lp = n_past + LANES
    kern = functools.partial(_dsa_sample_select_kernel, npg=npg, n_steps=n_steps, n_past=n_past, ts=ts, k_sel=k_sel)
    row_blk = n_prompt // ts
    grid_spec = pltpu.PrefetchScalarGridSpec(
        num_scalar_prefetch=1,
        grid=(db, n_steps),
        in_specs=[pl.BlockSpec((ts, IDX_HEADS * IDX_DIM), lambda b, j, pt: (row_blk + b, H_IQ // (IDX_HEADS * IDX_DIM))),
                  pl.BlockSpec((ts, LANES), lambda b, j, pt: (row_blk + b, H_IKW // LANES))]
        + _page_specs(npg, (IDX_DIM, PAGE_SIZE), layer, lambda j: j),
        out_specs=pl.BlockSpec((None, ts, lp), lambda b, j, pt: (b, 0, 0)),
        scratch_shapes=[pltpu.VMEM((ts, lp), I32)],
    )
    return pl.pallas_call(
        kern,
        grid_spec=grid_spec,
        out_shape=jax.ShapeDtypeStruct((db, ts, lp), F32),
        compiler_params=_cparams(("arbitrary", "arbitrary")),
        name="dsa_sample_select",
    )(page_table, h, h, *([cache_idx] * npg))


_NT = (((1,), (1,)), ((), ()))


def _sample_rows(group, ts):
    return -(-(group * ts) // 16) * 16


def _stack_heads_rows(q, c, group, ts, rows):
    return jnp.concatenate(
        [q[:, (c * group + g) * HEAD_DIM:(c * group + g + 1) * HEAD_DIM] for g in range(group)]
        + [jnp.zeros((rows - group * ts, HEAD_DIM), q.dtype)], axis=0)


def _tile_rows(bs, group, rows):
    return jnp.concatenate([bs] * group + [jnp.zeros((rows - group * bs.shape[0], bs.shape[1]), bs.dtype)], axis=0)


def _write_head_rows(out, o_ref, c, group, ts):
    for g in range(group):
        hcol = (c * group + g) * HEAD_DIM
        o_ref[:, hcol:hcol + HEAD_DIM] = out[g * ts:(g + 1) * ts]


def _dsa_sample_attend_kernel(pt_ref, q_ref, kn_ref, vn_ref, bias_ref, *refs, npg, n_steps, n_past, ts, rq):
    del pt_ref
    kp, vp = refs[:npg], refs[npg:2 * npg]
    o_ref = refs[2 * npg + 1]
    q_s, m_s, l_s, acc_s = refs[2 * npg + 2:]
    j = pl.program_id(1)
    ck = npg * PAGE_SIZE
    scale = HEAD_DIM ** -0.5

    @pl.when(j == 0)
    def _init():
        q = q_ref[...]
        for c in range(A_KV_HEADS):
            q_s[c] = _stack_heads_rows(q, c, A_GROUP, ts, rq).astype(BF16)
        m_s[...] = jnp.full(m_s.shape, NEG_INF, F32)
        l_s[...] = jnp.zeros_like(l_s)
        acc_s[...] = jnp.zeros_like(acc_s)

    def update(c, kc, vc, bias_rows):
        lg = lax.dot_general(q_s[c], kc.astype(BF16), _NT, preferred_element_type=F32) * scale + bias_rows
        m_old = m_s[c]
        m_new = jnp.maximum(m_old, jnp.max(lg, axis=1, keepdims=True))
        alpha = jnp.exp(m_old - m_new)
        p = jnp.exp(lg - m_new)
        l_s[c] = alpha * l_s[c] + jnp.sum(p, axis=1, keepdims=True)
        acc_s[c] = alpha * acc_s[c] + jnp.dot(p.astype(BF16), vc.astype(BF16), preferred_element_type=F32)
        m_s[c] = m_new

    bias_rows = _tile_rows(bias_ref[:, pl.ds(pl.multiple_of(j * ck, ck), ck)], A_GROUP, rq)
    for c in range(A_KV_HEADS):
        update(c, _gather_head(kp, c, A_KV_HEADS), _gather_head(vp, c, A_KV_HEADS), bias_rows)

    @pl.when(j == n_steps - 1)
    def _finish():
        bias_n = _tile_rows(bias_ref[:, n_past:n_past + LANES], A_GROUP, rq)
        for c in range(A_KV_HEADS):
            cols = slice(c * HEAD_DIM, (c + 1) * HEAD_DIM)
            update(c, _pad_rows(kn_ref[:, cols], LANES), _pad_rows(vn_ref[:, cols], LANES), bias_n)
            _write_head_rows(acc_s[c] / l_s[c], o_ref, c, A_GROUP, ts)


def dsa_sample_attend(page_table, h, bias, cache_k, cache_v, mix, layer, n_prompt, db, ts):
    n_pages = page_table.shape[1]
    n_past = n_pages * PAGE_SIZE
    npg = min(KV_PAGES_PER_STEP, n_pages)
    n_steps = n_pages // npg
    lp = n_past + LANES
    rq = _sample_rows(A_GROUP, ts)
    kern = functools.partial(_dsa_sample_attend_kernel, npg=npg, n_steps=n_steps, n_past=n_past, ts=ts, rq=rq)
    row_blk = n_prompt // ts
    kv_tail = (PAGE_SIZE, A_KV_HEADS, HEAD_DIM)
    grid_spec = pltpu.PrefetchScalarGridSpec(
        num_scalar_prefetch=1,
        grid=(db, n_steps),
        in_specs=[pl.BlockSpec((ts, A_DIM), lambda b, j, pt: (row_blk + b, H_AQ // A_DIM)),
                  pl.BlockSpec((ts, A_KV_DIM), lambda b, j, pt: (row_blk + b, H_AK // A_KV_DIM)),
                  pl.BlockSpec((ts, A_KV_DIM), lambda b, j, pt: (row_blk + b, H_AV // A_KV_DIM)),
                  pl.BlockSpec((None, ts, lp), lambda b, j, pt: (b, 0, 0))]
        + _page_specs(npg, kv_tail, layer, lambda j: j)
        + _page_specs(npg, kv_tail, layer, lambda j: j)
        + [pl.BlockSpec(memory_space=pl.ANY)],
        out_specs=pl.BlockSpec((ts, A_DIM), lambda b, j, pt: (row_blk + b, 0)),
        scratch_shapes=[pltpu.VMEM((A_KV_HEADS, rq, HEAD_DIM), BF16),
                        pltpu.VMEM((A_KV_HEADS, rq, 1), F32),
                        pltpu.VMEM((A_KV_HEADS, rq, 1), F32),
                        pltpu.VMEM((A_KV_HEADS, rq, HEAD_DIM), F32)],
    )
    return pl.pallas_call(
        kern,
        grid_spec=grid_spec,
        out_shape=jax.ShapeDtypeStruct(mix.shape, mix.dtype),
        input_output_aliases={5 + 2 * npg: 0},
        compiler_params=_cparams(("arbitrary", "arbitrary")),
        name="dsa_sample_attend",
    )(page_table, h, h, h, bias, *([cache_k] * npg), *([cache_v] * npg), mix)


def _moba_sample_k_kernel(pt_ref, q_ref, kn_ref, *refs, npg, n_steps, ts, rq):
    del pt_ref
    kp = refs[:npg]
    p_ref, pn_ref, l_ref = refs[npg:npg + 3]
    q_s, lg_s, km_s = refs[npg + 3:]
    j = pl.program_id(1)
    ck = npg * PAGE_SIZE
    blk = MOBA_BLOCK
    bps = ck // blk
    nbk = n_steps * bps
    scale = HEAD_DIM ** -0.5

    @pl.when(j == 0)
    def _init():
        q = q_ref[...]
        for c in range(B_KV_HEADS):
            q_s[c] = _stack_heads_rows(q, c, B_GROUP, ts, rq).astype(BF16)
        km_s[...] = jnp.zeros_like(km_s)

    for c in range(B_KV_HEADS):
        kc = _gather_head(kp, c, B_KV_HEADS)
        means = jnp.concatenate(
            [jnp.mean(kc[n * blk:(n + 1) * blk], axis=0, keepdims=True) for n in range(bps)], axis=0)
        km_s[c, pl.ds(pl.multiple_of(j * bps, bps), bps), :] = means
        lg_s[c, :, pl.ds(pl.multiple_of(j * ck, ck), ck)] = (
            lax.dot_general(q_s[c], kc.astype(BF16), _NT, preferred_element_type=F32) * scale)

    @pl.when(j == n_steps - 1)
    def _select_softmax():
        key_i = lax.broadcasted_iota(I32, (rq, LANES), 1)
        tok = lax.broadcasted_iota(I32, (rq, LANES), 0) % ts
        new_ok = jnp.logical_and(key_i <= tok, key_i < ts)
        blk_row = lax.broadcasted_iota(I32, (LANES, ck), 0)
        blk_of_key = lax.broadcasted_iota(I32, (LANES, ck), 1) // blk
        for c in range(B_KV_HEADS):
            cols = slice(c * HEAD_DIM, (c + 1) * HEAD_DIM)
            gate_t = lax.dot_general(km_s[c].astype(BF16), _pad_rows(q_s[c], LANES), _NT, preferred_element_type=F32)
            nbk8 = -(-nbk // SUBLANES) * SUBLANES
            sel_bias = _pad_rows(_rank_select(gate_t[:nbk8], nbk, axis=0), LANES).T[:rq].astype(BF16)
            lg_new = lax.dot_general(q_s[c], _pad_rows(kn_ref[:, cols], LANES).astype(BF16), _NT,
                                     preferred_element_type=F32) * scale
            lg_new = jnp.where(new_ok, lg_new, NEG_INF)
            m = jnp.max(lg_new, axis=1, keepdims=True)
            for t in range(n_steps):
                spread = jnp.where(blk_row == blk_of_key + t * bps, 1.0, 0.0).astype(BF16)
                lanes = slice(t * ck, (t + 1) * ck)
                lg = lg_s[c, :, lanes] + jnp.dot(sel_bias, spread, preferred_element_type=F32)
                lg_s[c, :, lanes] = lg
                m = jnp.maximum(m, jnp.max(lg, axis=1, keepdims=True))
            p_new = jnp.exp(lg_new - m)
            l = jnp.sum(p_new, axis=1, keepdims=True)
            for t in range(n_steps):
                lanes = slice(t * ck, (t + 1) * ck)
                p = jnp.exp(lg_s[c, :, lanes] - m)
                p_ref[c, :, lanes] = p.astype(BF16)
                l = l + jnp.sum(p, axis=1, keepdims=True)
            pn_ref[c] = p_new
            l_ref[c] = jnp.broadcast_to(l, (rq, LANES))


def _moba_sample_v_kernel(pt_ref, vn_ref, p_ref, pn_ref, l_ref, *refs, npg, n_steps, ts):
    del pt_ref
    vp = refs[:npg]
    o_ref = refs[npg + 1]
    acc_s = refs[npg + 2]
    j = pl.program_id(1)

    @pl.when(j == 0)
    def _init():
        acc_s[...] = jnp.zeros_like(acc_s)

    for c in range(B_KV_HEADS):
        vc = _gather_head(vp, c, B_KV_HEADS)
        acc_s[c] += jnp.dot(p_ref[c], vc.astype(BF16), preferred_element_type=F32)

    @pl.when(j == n_steps - 1)
    def _finish():
        for c in range(B_KV_HEADS):
            cols = slice(c * HEAD_DIM, (c + 1) * HEAD_DIM)
            vn = _pad_rows(vn_ref[:, cols], LANES)
            acc = acc_s[c] + jnp.dot(pn_ref[c].astype(BF16), vn.astype(BF16), preferred_element_type=F32)
            _write_head_rows(acc / l_ref[c][:, 0:1], o_ref, c, B_GROUP, ts)


def moba_sample(page_table, h, cache_k, cache_v, mix, layer, n_prompt, db, ts):
    n_pages = page_table.shape[1]
    n_past = n_pages * PAGE_SIZE
    npg = min(KV_PAGES_PER_STEP, n_pages)
    n_steps = n_pages // npg
    ck = npg * PAGE_SIZE
    rq = _sample_rows(B_GROUP, ts)
    row_blk = n_prompt // ts
    page_shape = (PAGE_SIZE, B_KV_HEADS, HEAD_DIM)
    k_spec = pltpu.PrefetchScalarGridSpec(
        num_scalar_prefetch=1,
        grid=(db, n_steps),
        in_specs=[pl.BlockSpec((ts, B_DIM), lambda b, j, pt: (row_blk + b, H_BQ // B_DIM)),
                  pl.BlockSpec((ts, B_KV_DIM), lambda b, j, pt: (row_blk + b, H_BK // B_KV_DIM))]
        + _page_specs(npg, page_shape, layer, lambda j: j),
        out_specs=[pl.BlockSpec((None, B_KV_HEADS, rq, n_past), lambda b, j, pt: (b, 0, 0, 0)),
                   pl.BlockSpec((None, B_KV_HEADS, rq, LANES), lambda b, j, pt: (b, 0, 0, 0)),
                   pl.BlockSpec((None, B_KV_HEADS, rq, LANES), lambda b, j, pt: (b, 0, 0, 0))],
        scratch_shapes=[pltpu.VMEM((B_KV_HEADS, rq, HEAD_DIM), BF16),
                        pltpu.VMEM((B_KV_HEADS, rq, n_past), F32),
                        pltpu.VMEM((B_KV_HEADS, LANES, HEAD_DIM), F32)],
    )
    p, pn, l = pl.pallas_call(
        functools.partial(_moba_sample_k_kernel, npg=npg, n_steps=n_steps, ts=ts, rq=rq),
        grid_spec=k_spec,
        out_shape=[jax.ShapeDtypeStruct((db, B_KV_HEADS, rq, n_past), BF16),
                   jax.ShapeDtypeStruct((db, B_KV_HEADS, rq, LANES), F32),
                   jax.ShapeDtypeStruct((db, B_KV_HEADS, rq, LANES), F32)],
        compiler_params=_cparams(("arbitrary", "arbitrary")),
        name="moba_sample_k",
    )(page_table, h, h, *([cache_k] * npg))
    v_spec = pltpu.PrefetchScalarGridSpec(
        num_scalar_prefetch=1,
        grid=(db, n_steps),
        in_specs=[pl.BlockSpec((ts, B_KV_DIM), lambda b, j, pt: (row_blk + b, H_BV // B_KV_DIM)),
                  pl.BlockSpec((None, B_KV_HEADS, rq, ck), lambda b, j, pt: (b, 0, 0, j)),
                  pl.BlockSpec((None, B_KV_HEADS, rq, LANES), lambda b, j, pt: (b, 0, 0, 0)),
                  pl.BlockSpec((None, B_KV_HEADS, rq, LANES), lambda b, j, pt: (b, 0, 0, 0))]
        + _page_specs(npg, page_shape, layer, lambda j: j)
        + [pl.BlockSpec(memory_space=pl.ANY)],
        out_specs=pl.BlockSpec((ts, B_DIM), lambda b, j, pt: (row_blk + b, A_DIM // B_DIM)),
        scratch_shapes=[pltpu.VMEM((B_KV_HEADS, rq, HEAD_DIM), F32)],
    )
    return pl.pallas_call(
        functools.partial(_moba_sample_v_kernel, npg=npg, n_steps=n_steps, ts=ts),
        grid_spec=v_spec,
        out_shape=jax.ShapeDtypeStruct(mix.shape, mix.dtype),
        input_output_aliases={5 + npg: 0},
        compiler_params=_cparams(("arbitrary", "arbitrary")),
        name="moba_sample_v",
    )(page_table, h, p, pn, l, *([cache_v] * npg), mix)


def _row_tile(n_rows, target):
    best = None
    for t in range(16, target + 1, 16):
        if n_rows % t == 0:
            best = t
    assert best is not None, n_rows
    return best


_DST = dict(aq=H_AQ, bq=H_BQ, iq=H_IQ, pu=H_PU, ak=H_AK, av=H_AV, bk=H_BK, bv=H_BV)
_H_USED = H_IKW + IDX_DIM + IDX_HEADS
PACK_COLS = 256


def _pack_w_in_kernel(wt_ref, o_ref):
    for name, dst in _DST.items():
        src, width = _SRC[name]
        o_ref[:, dst:dst + width] = wt_ref[src:src + width, :].T.astype(BF16)
    src = _SRC["ik"][0]
    lane = lax.broadcasted_iota(I32, (o_ref.shape[0], LANES), 1)
    tile = wt_ref[src:src + LANES, :].T
    o_ref[:, H_IKW:H_IKW + LANES] = jnp.where(lane < _H_USED - H_IKW, tile, 0.0).astype(BF16)
    o_ref[:, H_IKW + LANES:] = jnp.zeros((o_ref.shape[0], H_DIM - H_IKW - LANES), BF16)


def _pack_w_in(w_in):
    depth, d, n_in = w_in.shape
    cols = min(PACK_COLS, d)
    return pl.pallas_call(
        _pack_w_in_kernel,
        grid=(depth, d // cols),
        in_specs=[pl.BlockSpec((None, n_in, cols), lambda l, i: (l, 0, i))],
        out_specs=pl.BlockSpec((None, cols, H_DIM), lambda l, i: (l, i, 0)),
        out_shape=jax.ShapeDtypeStruct((depth, d, H_DIM), BF16),
        compiler_params=_cparams(("parallel", "parallel")),
        name="pack_w_in",
    )(jnp.swapaxes(w_in, 1, 2))


def kernel(x_prompt, x_sample, cache_a_k, cache_a_v, cache_a_idx, cache_b_k, cache_b_v, state_pool,
           cache_mem_k, cache_mem_v, page_table, mem_prompt, w_in, w_out, pool_w, pool_scale, ln1_g, ln1_b,
           w_mem_q, w_mem_k, w_mem_v, w_mem_o, ln2_g, ln2_b, w_ffn_1, w_ffn_3, w_ffn_2, ln3_g, ln3_b):
    batch, seq, d = x_prompt.shape
    db, ts, _ = x_sample.shape
    depth = w_in.shape[0]
    n_past = page_table.shape[1] * PAGE_SIZE
    n_prompt = batch * seq
    n_rows = n_prompt + db * ts
    tm = _row_tile(n_rows, 700)
    tm_p = _row_tile(n_prompt, 1100)

    w_in_p = _pack_w_in(w_in)
    w_out_b = w_out.astype(BF16)
    w_mq_b = w_mem_q.astype(BF16)
    w_mkv_b = jnp.concatenate([w_mem_k, w_mem_v], axis=-1).astype(BF16)
    w_mo_b = w_mem_o.astype(BF16)
    w1_b, w3_b = w_ffn_1.astype(BF16), w_ffn_3.astype(BF16)
    pool_w_b = pool_w.astype(BF16)
    pool_scale3 = pool_scale.reshape(depth, 1, POOL_DIM)
    ln = [a.reshape(depth, 1, d) for a in (ln1_g, ln1_b, ln2_g, ln2_b, ln3_g, ln3_b)]
    mem = mem_prompt.reshape(batch * MEM_TOKENS, d)
    idx_t = jnp.swapaxes(cache_a_idx, 2, 3)

    x = jnp.concatenate([x_prompt.reshape(n_prompt, d), x_sample.reshape(db * ts, d)], axis=0)
    outs = {k: [] for k in ("pa_i", "p_pool", "pm_k", "pm_v", "sa_k", "sa_v", "sa_i", "sb_k", "sb_v", "s_pool")}
    kv_new = None
    for l in range(depth):
        h, kv_new = project_in(x, w_in_p, l, tm_p, n_prompt, kv_new)
        h = project_rows(x, w_in_p, l, h, n_prompt // (db * ts), db * ts, H_TN)
        mkv = matmul(mem, w_mkv_b, l, _row_tile(mem.shape[0], 512), MEM_DIM)

        mix = dsa_prompt(h, batch, seq, n_rows)
        mix = moba_prompt(h, mix, batch, seq)
        mix = pool_prompt(h, mix, pool_w_b, pool_scale3, l, batch, seq)
        bias = dsa_sample_select(page_table, h, idx_t, l, n_prompt, db, ts)
        mix = dsa_sample_attend(page_table, h, bias, cache_a_k, cache_a_v, mix, l, n_prompt, db, ts)
        mix = moba_sample(page_table, h, cache_b_k, cache_b_v, mix, l, n_prompt, db, ts)
        mix = pool_sample(h, mix, state_pool, pool_w_b, pool_scale3, l, n_prompt, db, ts, n_past)

        x1 = matmul_residual_ln(mix, w_out_b, x, ln[0], ln[1], l, tm // 2, MIX_DIM)
        q = matmul(x1, w_mq_b, l, tm, MEM_DIM)
        o = memattn_prompt(q, mkv, batch, seq)
        o = memattn_sample(q, cache_mem_k, cache_mem_v, o, l, n_prompt, db, ts)
        x2 = matmul_residual_ln(o, w_mo_b, x1, ln[2], ln[3], l, tm, MEM_DIM)
        x = ffn_ln(x2, w1_b, w3_b, w_ffn_2, ln[4], ln[5], l, tm, 512)

        hp, hs = h[:n_prompt], h[n_prompt:]

        def cols(a, off, width, lead):
            return a[:, off:off + width].reshape(lead)

        kv_s = (db, ts, A_KV_HEADS, HEAD_DIM)
        outs["pa_i"].append(cols(hp, H_IKW, IDX_DIM, (batch, seq, IDX_DIM)))
        outs["p_pool"].append(jnp.stack(
            [h[(b + 1) * seq - POOL_STATE:(b + 1) * seq, H_PU:H_PU + POOL_DIM] for b in range(batch)]))
        outs["pm_k"].append(mkv[:, :MEM_DIM].reshape(batch, MEM_TOKENS, MEM_HEADS, MEM_HEAD_DIM))
        outs["pm_v"].append(mkv[:, MEM_DIM:].reshape(batch, MEM_TOKENS, MEM_HEADS, MEM_HEAD_DIM))
        outs["sa_k"].append(cols(hs, H_AK, A_KV_DIM, kv_s))
        outs["sa_v"].append(cols(hs, H_AV, A_KV_DIM, kv_s))
        outs["sa_i"].append(cols(hs, H_IKW, IDX_DIM, (db, ts, IDX_DIM)))
        outs["sb_k"].append(cols(hs, H_BK, B_KV_DIM, kv_s))
        outs["sb_v"].append(cols(hs, H_BV, B_KV_DIM, kv_s))
        pu_s = cols(hs, H_PU, POOL_DIM, (db, ts, POOL_DIM))
        outs["s_pool"].append(jnp.concatenate([state_pool[l], pu_s], axis=1)[:, -POOL_STATE:])

    st = {k: jnp.stack(v) for k, v in outs.items()}
    for name, arr in zip(("pa_k", "pa_v", "pb_k", "pb_v"), kv_new):
        st[name] = arr.reshape(depth, batch, seq, A_KV_HEADS, HEAD_DIM)
    return (x[:n_prompt].reshape(batch, seq, d), x[n_prompt:].reshape(db, ts, d),
            st["pa_k"], st["pa_v"], st["pa_i"], st["pb_k"], st["pb_v"], st["p_pool"], st["pm_k"], st["pm_v"],
            st["sa_k"], st["sa_v"], st["sa_i"], st["sb_k"], st["sb_v"], st["s_pool"])
```
